```python
import jax, jax.numpy as jnp
from jax import lax
import numpy as np

D_MODEL = 2048
BATCH = 4
SEQ = 2048
DEPTH = 1
DEC_BATCH = 8
DEC_SEQ = 4
PAST_LEN = 16384
PAGE_SIZE = 128

MIX_WIDTH = D_MODEL
ATTN_WIDTH = MIX_WIDTH // 2
POOL_WIDTH = MIX_WIDTH - ATTN_WIDTH
HEAD_DIM = 128
N_HEADS = ATTN_WIDTH // HEAD_DIM
MOBA_BLOCK = 256
MOBA_TOPK = 3
Q_CHUNK = 16
POOL_WINDOWS = (2, 4, 8, 16)
N_POOL_GROUPS = len(POOL_WINDOWS)
POOL_GROUP_W = POOL_WIDTH // N_POOL_GROUPS
POOL_HIST = max(POOL_WINDOWS) - 1
IN_WIDTH = 3 * ATTN_WIDTH + POOL_WIDTH
D_FF = -(-8 * D_MODEL // (3 * 256)) * 256
RMS_EPS = 1e-6
NEG = -1e30

kernel_name = 'moba_pool_hybrid_step'


def rmsnorm(x, g):
    xf = x.astype(jnp.float32)
    y = xf * lax.rsqrt(jnp.mean(xf * xf, axis=-1, keepdims=True) + RMS_EPS)
    return (y * g.astype(jnp.float32)).astype(x.dtype)


def alibi_slopes():
    return jnp.exp2(-8.0 * jnp.arange(1, N_HEADS + 1, dtype=jnp.float32) / N_HEADS)


def moba_select(q, q_pos, k_mean):
    s = jnp.einsum('bhqd,bnhd->bhqn', q.astype(jnp.float32), k_mean)
    nb = k_mean.shape[1]
    q_blk = q_pos // MOBA_BLOCK
    past = jnp.arange(nb)[None, :] < q_blk[:, None]
    s = jnp.where(past, s, NEG)
    _, idx = lax.top_k(s, min(MOBA_TOPK, nb))
    valid = idx < q_blk[:, None]
    return idx, valid


def moba_attend(q, q_pos, k_sel, v_sel, idx, valid, k_own, v_own, own_start, slopes):
    B, H, Tq, D = q.shape
    KS = idx.shape[-1]
    n_sel = KS * MOBA_BLOCK
    scale = D ** -0.5
    offs = jnp.arange(MOBA_BLOCK, dtype=jnp.int32)
    s_sel = jnp.einsum('bhqd,bhqjkd->bhqjk', q, k_sel, preferred_element_type=jnp.float32) * scale
    dist_sel = (q_pos[:, None, None] - (idx[..., None] * MOBA_BLOCK + offs)).astype(jnp.float32)
    s_sel = jnp.where(valid[..., None], s_sel - slopes[:, None, None, None] * dist_sel, NEG)
    s_own = jnp.einsum('bhqd,bkhd->bhqk', q, k_own, preferred_element_type=jnp.float32) * scale
    dist_own = (q_pos[:, None] - (own_start + offs)[None, :]).astype(jnp.float32)
    s_own = jnp.where(dist_own >= 0, s_own - slopes[:, None, None] * dist_own, NEG)
    s = jnp.concatenate([s_sel.reshape(B, H, Tq, n_sel), s_own], axis=-1)
    p = jax.nn.softmax(s, axis=-1).astype(v_own.dtype)
    o = (jnp.einsum('bhqjk,bhqjkd->bhqd', p[..., :n_sel].reshape(B, H, Tq, KS, MOBA_BLOCK), v_sel,
                    preferred_element_type=jnp.float32)
         + jnp.einsum('bhqk,bkhd->bhqd', p[..., n_sel:], v_own, preferred_element_type=jnp.float32))
    return o.astype(q.dtype)


def moba_prompt(q, k, v, slopes):
    B, T, H, D = q.shape
    nb = -(-T // MOBA_BLOCK)
    pad = ((0, 0), (0, nb * MOBA_BLOCK - T), (0, 0), (0, 0))
    kb = jnp.pad(k, pad).reshape(B, nb, MOBA_BLOCK, H, D)
    vb = jnp.pad(v, pad).reshape(B, nb, MOBA_BLOCK, H, D)
    k_mean = jnp.mean(kb.astype(jnp.float32), axis=2)
    qh = q.transpose(0, 2, 1, 3)
    bi = jnp.arange(B)[:, None, None, None]
    hi = jnp.arange(H)[None, :, None, None]

    def chunk(c):
        start = c * Q_CHUNK
        q_c = lax.dynamic_slice_in_dim(qh, start, Q_CHUNK, axis=2)
        q_pos = start + jnp.arange(Q_CHUNK, dtype=jnp.int32)
        idx, valid = moba_select(q_c, q_pos, k_mean)
        k_sel = kb[bi, idx, :, hi, :]
        v_sel = vb[bi, idx, :, hi, :]
        ob = start // MOBA_BLOCK
        k_own = lax.dynamic_index_in_dim(kb, ob, axis=1, keepdims=False)
        v_own = lax.dynamic_index_in_dim(vb, ob, axis=1, keepdims=False)
        return moba_attend(q_c, q_pos, k_sel, v_sel, idx, valid, k_own, v_own, ob * MOBA_BLOCK, slopes)

    o = lax.map(chunk, jnp.arange(T // Q_CHUNK, dtype=jnp.int32))
    return o.transpose(1, 0, 3, 2, 4).reshape(B, T, H * D)


def moba_sample(q, k_new, v_new, cache_k, cache_v, page_table, slopes):
    DB, Tn, H, D = q.shape
    n_pages = page_table.shape[1]
    past = n_pages * PAGE_SIZE
    k_past = cache_k[page_table].reshape(DB, past, H, D)
    L = past + Tn
    nb = -(-L // MOBA_BLOCK)
    kb = jnp.pad(jnp.concatenate([k_past, k_new.astype(k_past.dtype)], axis=1),
                 ((0, 0), (0, nb * MOBA_BLOCK - L), (0, 0), (0, 0))).reshape(DB, nb, MOBA_BLOCK, H, D)
    k_mean = jnp.mean(kb.astype(jnp.float32), axis=2)
    qh = q.transpose(0, 2, 1, 3)
    q_pos = past + jnp.arange(Tn, dtype=jnp.int32)
    idx, valid = moba_select(qh, q_pos, k_mean)
    KS = idx.shape[-1]
    bi = jnp.arange(DB)[:, None, None, None]
    hi = jnp.arange(H)[None, :, None, None]
    k_sel = kb[bi, idx, :, hi, :]
    ppb = MOBA_BLOCK // PAGE_SIZE
    lp = jnp.minimum(idx[..., None] * ppb + jnp.arange(ppb), n_pages - 1)
    phys = page_table[bi[..., None], lp]
    v_sel = cache_v[phys, :, hi[..., None], :].reshape(DB, H, Tn, KS, MOBA_BLOCK, D)
    ob = past // MOBA_BLOCK
    own_start = ob * MOBA_BLOCK
    k_own = kb[:, ob]
    v_own_past = cache_v[page_table[:, own_start // PAGE_SIZE:]].reshape(DB, past - own_start, H, D)
    v_own = jnp.pad(jnp.concatenate([v_own_past, v_new.astype(v_own_past.dtype)], axis=1),
                    ((0, 0), (0, MOBA_BLOCK - (past - own_start) - Tn), (0, 0), (0, 0)))
    o = moba_attend(qh, q_pos, k_sel, v_sel, idx, valid, k_own, v_own, own_start, slopes)
    return o.transpose(0, 2, 1, 3).reshape(DB, Tn, H * D)


def pool_mixer(u, hist, start_pos, w_pool, pool_scale):
    B, T, C = u.shape
    uf = u.astype(jnp.float32)
    ext = jnp.concatenate([hist.astype(jnp.float32), uf], axis=1)
    csum = jnp.concatenate([jnp.zeros((B, 1, C), jnp.float32), lax.cumsum(ext, axis=1)], axis=1)
    pos = start_pos + jnp.arange(T, dtype=jnp.int32)
    outs = []
    for g, w in enumerate(POOL_WINDOWS):
        sl = slice(g * POOL_GROUP_W, (g + 1) * POOL_GROUP_W)
        win_sum = csum[:, POOL_HIST + 1:, sl] - csum[:, POOL_HIST + 1 - w:POOL_HIST + 1 - w + T, sl]
        cnt = jnp.minimum(pos + 1, w).astype(jnp.float32)[None, :, None]
        outs.append(win_sum / cnt - uf[..., sl])
    d = jnp.concatenate(outs, axis=-1).reshape(B, T, N_POOL_GROUPS, POOL_GROUP_W)
    y = jnp.einsum('btgc,gce->btge', d, w_pool.astype(jnp.float32)).reshape(B, T, C) * pool_scale.astype(jnp.float32)
    new_hist = ext[:, -POOL_HIST:]
    return y.astype(u.dtype), new_hist.astype(u.dtype)


def trunk_layer(x, hist, start_pos, attend, norm1_g, w_in, w_pool, pool_scale, w_out,
                norm2_g, w_gate, w_up, w_down):
    B, T, _ = x.shape
    h = rmsnorm(x, norm1_g)
    z = h @ w_in
    q, k, v, u = jnp.split(z, [ATTN_WIDTH, 2 * ATTN_WIDTH, 3 * ATTN_WIDTH], axis=-1)
    q = q.reshape(B, T, N_HEADS, HEAD_DIM)
    k = k.reshape(B, T, N_HEADS, HEAD_DIM)
    v = v.reshape(B, T, N_HEADS, HEAD_DIM)
    a = attend(q, k, v)
    p, new_hist = pool_mixer(u, hist, start_pos, w_pool, pool_scale)
    x = x + jnp.concatenate([a, p.astype(a.dtype)], axis=-1) @ w_out
    h2 = rmsnorm(x, norm2_g)
    x = x + (jax.nn.silu(h2 @ w_gate) * (h2 @ w_up)) @ w_down
    return x, k, v, new_hist


def setup_inputs(seed: int = 0) -> dict:
    key = jax.random.key(seed)
    ks = jax.random.split(key, 16)
    nrm = jax.random.normal
    f32 = jnp.float32
    n_pages = PAST_LEN // PAGE_SIZE
    n_used = DEC_BATCH * n_pages
    n_pool = n_used + n_used // 4
    page_table = jax.random.permutation(ks[5], n_pool)[:n_used].reshape(DEC_BATCH, n_pages).astype(jnp.int32)
    return {
        'x_prompt': nrm(ks[0], (BATCH, SEQ, D_MODEL), f32),
        'x_sample': nrm(ks[1], (DEC_BATCH, DEC_SEQ, D_MODEL), f32),
        'cache_k': nrm(ks[2], (DEPTH, n_pool, PAGE_SIZE, N_HEADS, HEAD_DIM), f32),
        'cache_v': nrm(ks[3], (DEPTH, n_pool, PAGE_SIZE, N_HEADS, HEAD_DIM), f32),
        'state_pool': nrm(ks[4], (DEPTH, DEC_BATCH, POOL_HIST, POOL_WIDTH), f32),
        'page_table': page_table,
        'norm1_g': 1.0 + 0.1 * nrm(ks[6], (DEPTH, D_MODEL), f32),
        'w_in': nrm(ks[7], (DEPTH, D_MODEL, IN_WIDTH), f32) * D_MODEL ** -0.5,
        'w_pool': nrm(ks[8], (DEPTH, N_POOL_GROUPS, POOL_GROUP_W, POOL_GROUP_W), f32) * POOL_GROUP_W ** -0.5,
        'pool_scale': 1.0 + 0.1 * nrm(ks[9], (DEPTH, POOL_WIDTH), f32),
        'w_out': nrm(ks[10], (DEPTH, MIX_WIDTH, D_MODEL), f32) * MIX_WIDTH ** -0.5,
        'norm2_g': 1.0 + 0.1 * nrm(ks[11], (DEPTH, D_MODEL), f32),
        'w_gate': nrm(ks[12], (DEPTH, D_MODEL, D_FF), f32) * D_MODEL ** -0.5,
        'w_up': nrm(ks[13], (DEPTH, D_MODEL, D_FF), f32) * D_MODEL ** -0.5,
        'w_down': nrm(ks[14], (DEPTH, D_FF, D_MODEL), f32) * D_FF ** -0.5,
        'norm_f_g': 1.0 + 0.1 * nrm(ks[15], (D_MODEL,), f32),
    }


def reference(x_prompt, x_sample, cache_k, cache_v, state_pool, page_table, norm1_g, w_in, w_pool,
              pool_scale, w_out, norm2_g, w_gate, w_up, w_down, norm_f_g):
    slopes = alibi_slopes()
    past_len = page_table.shape[1] * PAGE_SIZE
    zero_hist = jnp.zeros((x_prompt.shape[0], POOL_HIST, POOL_WIDTH), x_prompt.dtype)
    hp, hs = x_prompt, x_sample
    kp, vp, sp, ksm, vsm, ssm = [], [], [], [], [], []
    for l in range(DEPTH):
        lw = (norm1_g[l], w_in[l], w_pool[l], pool_scale[l], w_out[l], norm2_g[l],
              w_gate[l], w_up[l], w_down[l])
        hp, k1, v1, s1 = trunk_layer(hp, zero_hist, 0,
                                     lambda q, k, v: moba_prompt(q, k, v, slopes), *lw)
        ck, cv = cache_k[l], cache_v[l]
        hs, k2, v2, s2 = trunk_layer(hs, state_pool[l], past_len,
                                     lambda q, k, v: moba_sample(q, k, v, ck, cv, page_table, slopes), *lw)
        kp.append(k1); vp.append(v1); sp.append(s1)
        ksm.append(k2); vsm.append(v2); ssm.append(s2)
    y_prompt = rmsnorm(hp, norm_f_g)
    y_sample = rmsnorm(hs, norm_f_g)
    k_prompt = jnp.stack(kp)
    v_prompt = jnp.stack(vp)
    pool_prompt = jnp.stack(sp)
    k_sample = jnp.stack(ksm)
    v_sample = jnp.stack(vsm)
    pool_sample = jnp.stack(ssm)
    return (y_prompt, y_sample, k_prompt, v_prompt, pool_prompt, k_sample, v_sample, pool_sample)
```

```python
import functools

import jax
import jax.numpy as jnp
from jax import lax
from jax.experimental import pallas as pl
from jax.experimental.pallas import tpu as pltpu

D_MODEL = 2048
HEAD_DIM = 128
N_HEADS = 8
ATTN_WIDTH = N_HEADS * HEAD_DIM
POOL_WIDTH = D_MODEL - ATTN_WIDTH
IN_WIDTH = 3 * ATTN_WIDTH + POOL_WIDTH
MOBA_BLOCK = 256
MOBA_TOPK = 3
PAGE_SIZE = 128
PAGES_PER_BLOCK = MOBA_BLOCK // PAGE_SIZE
POOL_WINDOWS = (2, 4, 8, 16)
POOL_GROUP_W = POOL_WIDTH // len(POOL_WINDOWS)
POOL_HIST = max(POOL_WINDOWS) - 1
HIST_PAD = 16
RMS_EPS = 1e-6
NEG = -1e30
ATTN_SCALE = HEAD_DIM ** -0.5

V7X_VMEM_BYTES = 64 * 1024 * 1024
VMEM_LIMIT_BYTES = V7X_VMEM_BYTES - 8 * 1024 * 1024

F32 = jnp.float32
BF16 = jnp.bfloat16
_NT = (((1,), (1,)), ((), ()))


def _rmsnorm(x, g):
    y = x * lax.rsqrt(jnp.mean(x * x, axis=-1, keepdims=True) + RMS_EPS)
    return y * g


def _params(*semantics):
    return pltpu.CompilerParams(dimension_semantics=semantics, vmem_limit_bytes=VMEM_LIMIT_BYTES)


def _proj_prompt_kernel(x_ref, g_ref, w_ref, q_ref, k_ref, v_ref, u_ref, kb_ref, vb_ref, km_ref, *, tm):
    h = _rmsnorm(x_ref[...], g_ref[...]).astype(BF16)
    for s, o_ref in enumerate((q_ref, k_ref, v_ref, u_ref)):
        z = jnp.dot(h, w_ref[:, s * ATTN_WIDTH:(s + 1) * ATTN_WIDTH], preferred_element_type=F32)
        o_ref[...] = z
        if s == 1:
            kb_ref[...] = z.astype(BF16)
            for blk in range(tm // MOBA_BLOCK):
                rows = z[blk * MOBA_BLOCK:(blk + 1) * MOBA_BLOCK]
                km_ref[blk] = jnp.sum(rows, axis=0, keepdims=True) * (1.0 / MOBA_BLOCK)
        if s == 2:
            vb_ref[...] = z.astype(BF16)


def _proj_prompt(x, g, w_bf, tm=256):
    m = x.shape[0]
    row = lambda i: (i, 0)
    f32_out = jax.ShapeDtypeStruct((m, ATTN_WIDTH), F32)
    bf_out = jax.ShapeDtypeStruct((m, ATTN_WIDTH), BF16)
    km_out = jax.ShapeDtypeStruct((m // MOBA_BLOCK, 1, ATTN_WIDTH), F32)
    blk = pl.BlockSpec((tm, ATTN_WIDTH), row)
    return pl.pallas_call(
        functools.partial(_proj_prompt_kernel, tm=tm),
        grid=(m // tm,),
        in_specs=[
            pl.BlockSpec((tm, D_MODEL), row),
            pl.BlockSpec((1, D_MODEL), lambda i: (0, 0)),
            pl.BlockSpec((D_MODEL, IN_WIDTH), lambda i: (0, 0), pipeline_mode=pl.Buffered(1)),
        ],
        out_specs=[blk, blk, blk, blk, blk, blk,
                   pl.BlockSpec((tm // MOBA_BLOCK, 1, ATTN_WIDTH), lambda i: (i, 0, 0))],
        out_shape=[f32_out, f32_out, f32_out, f32_out, bf_out, bf_out, km_out],
        compiler_params=_params("arbitrary"),
        name="proj_prompt",
    )(x, g, w_bf)


def _proj_sample_kernel(x_ref, g_ref, w_ref, k_ref, v_ref, u_ref, qh_ref, kh_ref, vh_ref):
    h = _rmsnorm(x_ref[...], g_ref[...]).astype(BF16)
    zs = [jnp.dot(h, w_ref[:, s * ATTN_WIDTH:(s + 1) * ATTN_WIDTH], preferred_element_type=F32)
          for s in range(4)]
    k_ref[...] = zs[1]
    v_ref[...] = zs[2]
    u_ref[...] = zs[3]
    for hd in range(N_HEADS):
        ls = slice(hd * HEAD_DIM, (hd + 1) * HEAD_DIM)
        qh_ref[hd] = zs[0][:, ls]
        kh_ref[hd] = zs[1][:, ls]
        vh_ref[hd] = zs[2][:, ls]


def _proj_sample(x, g, w_bf):
    m = x.shape[0]
    f32_out = jax.ShapeDtypeStruct((m, ATTN_WIDTH), F32)
    hm_out = jax.ShapeDtypeStruct((N_HEADS, m, HEAD_DIM), F32)
    return pl.pallas_call(
        _proj_sample_kernel,
        out_shape=[f32_out, f32_out, f32_out, hm_out, hm_out, hm_out],
        compiler_params=pltpu.CompilerParams(vmem_limit_bytes=VMEM_LIMIT_BYTES),
        name="proj_sample",
    )(x, g, w_bf)


def _topk_block_bias(scores, n_valid):
    nb = scores.shape[1]
    n_iota = lax.broadcasted_iota(jnp.int32, scores.shape, 1)
    past = n_iota < n_valid
    s = jnp.where(past, scores, NEG)
    rank = jnp.zeros(scores.shape, jnp.int32)
    for m in range(nb):
        sm = s[:, m:m + 1]
        tie_lower = jnp.logical_and(sm == s, n_iota > m)
        rank = rank + jnp.where(sm > s, 1, jnp.where(tie_lower, 1, 0))
    chosen = jnp.logical_and(past, rank < MOBA_TOPK)
    return jnp.where(chosen, 0.0, NEG)


def _attn_prompt_kernel(slopes_ref, q_ref, kb_ref, vb_ref, km_ref, o_ref, m_ref, l_ref, acc_ref, *, nb):
    hd = pl.program_id(1)
    qi = pl.program_id(2)
    slope = slopes_ref[hd]
    q = q_ref[...]
    sel_scores = lax.dot_general(q, km_ref[...], _NT, precision=lax.Precision.HIGHEST,
                                 preferred_element_type=F32)
    sel_bias = _topk_block_bias(sel_scores, qi)
    qs = (q * ATTN_SCALE).astype(BF16)

    shape = (MOBA_BLOCK, MOBA_BLOCK)
    rel = (lax.broadcasted_iota(jnp.int32, shape, 0) - lax.broadcasted_iota(jnp.int32, shape, 1))
    rel_f = rel.astype(F32)

    start = pl.multiple_of(qi * MOBA_BLOCK, MOBA_BLOCK)
    s = lax.dot_general(qs, kb_ref[pl.ds(start, MOBA_BLOCK), :], _NT, preferred_element_type=F32)
    s = jnp.where(rel >= 0, s - slope * rel_f, NEG)
    m0 = jnp.max(s, axis=-1, keepdims=True)
    p = jnp.exp(s - m0)
    m_ref[...] = m0
    l_ref[...] = jnp.sum(p, axis=-1, keepdims=True)
    acc_ref[...] = jnp.dot(p.astype(BF16), vb_ref[pl.ds(start, MOBA_BLOCK), :], preferred_element_type=F32)

    for n in range(nb - 1):
        @pl.when(n < qi)
        def _():
            rows = slice(n * MOBA_BLOCK, (n + 1) * MOBA_BLOCK)
            dist = rel_f + ((qi - n) * MOBA_BLOCK).astype(F32)
            sn = lax.dot_general(qs, kb_ref[rows, :], _NT, preferred_element_type=F32)
            sn = sn - slope * dist + sel_bias[:, n:n + 1]
            m_old = m_ref[...]
            m_new = jnp.maximum(m_old, jnp.max(sn, axis=-1, keepdims=True))
            alpha = jnp.exp(m_old - m_new)
            pn = jnp.exp(sn - m_new)
            l_ref[...] = alpha * l_ref[...] + jnp.sum(pn, axis=-1, keepdims=True)
            acc_ref[...] = alpha * acc_ref[...] + jnp.dot(pn.astype(BF16), vb_ref[rows, :],
                                                          preferred_element_type=F32)
            m_ref[...] = m_new

    o_ref[...] = (acc_ref[...] / l_ref[...]).astype(o_ref.dtype)


def _attn_prompt(slopes, q, kb, vb, km, batch, seq):
    nb = seq // MOBA_BLOCK
    grid_spec = pltpu.PrefetchScalarGridSpec(
        num_scalar_prefetch=1,
        grid=(batch, N_HEADS, nb),
        in_specs=[
            pl.BlockSpec((MOBA_BLOCK, HEAD_DIM), lambda b, h, i, sl: (b * nb + i, h)),
            pl.BlockSpec((seq, HEAD_DIM), lambda b, h, i, sl: (b, h)),
            pl.BlockSpec((seq, HEAD_DIM), lambda b, h, i, sl: (b, h)),
            pl.BlockSpec((None, nb, HEAD_DIM), lambda b, h, i, sl: (b, 0, h)),
        ],
        out_specs=pl.BlockSpec((MOBA_BLOCK, HEAD_DIM), lambda b, h, i, sl: (b * nb + i, h)),
        scratch_shapes=[pltpu.VMEM((MOBA_BLOCK, 1), F32), pltpu.VMEM((MOBA_BLOCK, 1), F32),
                        pltpu.VMEM((MOBA_BLOCK, HEAD_DIM), F32)],
    )
    return pl.pallas_call(
        functools.partial(_attn_prompt_kernel, nb=nb),
        grid_spec=grid_spec,
        out_shape=jax.ShapeDtypeStruct((batch * seq, ATTN_WIDTH), BF16),
        compiler_params=_params("arbitrary", "arbitrary", "arbitrary"),
        name="attn_prompt",
    )(slopes, q, kb, vb, km)


def _pool_diff(ext_ref, r0, n, pos, d_ref, d0):
    for g, w in enumerate(POOL_WINDOWS):
        ls = slice(g * POOL_GROUP_W, (g + 1) * POOL_GROUP_W)
        cur = ext_ref[pl.ds(r0, n), ls]
        win = cur
        for j in range(1, w):
            win = win + ext_ref[pl.ds(r0 - j, n), ls]
        cnt = jnp.minimum(pos + 1, w).astype(F32)
        d_ref[pl.ds(d0, n), ls] = win / cnt - cur


def _mix_out(a_bf, d_ref, wp_ref, ps_ref, wo_ref):
    ys = []
    for g in range(len(POOL_WINDOWS)):
        ls = slice(g * POOL_GROUP_W, (g + 1) * POOL_GROUP_W)
        y = jnp.dot(d_ref[:, ls].astype(BF16), wp_ref[g], preferred_element_type=F32)
        ys.append((y * ps_ref[:, ls]).astype(BF16))
    mix = jnp.concatenate([a_bf] + ys, axis=-1)
    return jnp.dot(mix, wo_ref[...], preferred_element_type=F32)


def _mid_prompt_kernel(a_ref, u_ref, uprev_ref, x_ref, wp_ref, ps_ref, wo_ref, g2_ref,
                       x1_ref, h2_ref, hist_ref, ext_ref, d_ref, *, tm, tiles_per_seq):
    t_in_seq = lax.rem(pl.program_id(0), tiles_per_seq)
    ext_ref[0:HIST_PAD, :] = jnp.where(t_in_seq == 0, 0.0, uprev_ref[...])
    ext_ref[HIST_PAD:HIST_PAD + tm, :] = u_ref[...]
    pos = t_in_seq * tm + lax.broadcasted_iota(jnp.int32, (tm, 1), 0)
    _pool_diff(ext_ref, HIST_PAD, tm, pos, d_ref, 0)
    x1 = x_ref[...] + _mix_out(a_ref[...], d_ref, wp_ref, ps_ref, wo_ref)
    x1_ref[...] = x1
    h2_ref[...] = _rmsnorm(x1, g2_ref[...]).astype(BF16)

    @pl.when(t_in_seq == tiles_per_seq - 1)
    def _():
        hist_ref[...] = ext_ref[pl.ds(HIST_PAD + tm - POOL_HIST, POOL_HIST), :]


def _mid_prompt(a, u, x, wp_bf, ps, wo_bf, g2, batch, seq, tm=256):
    m = x.shape[0]
    tiles_per_seq = seq // tm
    row = lambda i: (i, 0)
    const2 = lambda i: (0, 0)
    return pl.pallas_call(
        functools.partial(_mid_prompt_kernel, tm=tm, tiles_per_seq=tiles_per_seq),
        grid=(m // tm,),
        in_specs=[
            pl.BlockSpec((tm, ATTN_WIDTH), row),
            pl.BlockSpec((tm, POOL_WIDTH), row),
            pl.BlockSpec((HIST_PAD, POOL_WIDTH), lambda i: (jnp.maximum(i * (tm // HIST_PAD) - 1, 0), 0)),
            pl.BlockSpec((tm, D_MODEL), row),
            pl.BlockSpec(wp_bf.shape, lambda i: (0, 0, 0), pipeline_mode=pl.Buffered(1)),
            pl.BlockSpec((1, POOL_WIDTH), const2),
            pl.BlockSpec((D_MODEL, D_MODEL), const2, pipeline_mode=pl.Buffered(1)),
            pl.BlockSpec((1, D_MODEL), const2),
        ],
        out_specs=[
            pl.BlockSpec((tm, D_MODEL), row),
            pl.BlockSpec((tm, D_MODEL), row),
            pl.BlockSpec((None, POOL_HIST, POOL_WIDTH), lambda i: (i // tiles_per_seq, 0, 0)),
        ],
        out_shape=[
            jax.ShapeDtypeStruct((m, D_MODEL), F32),
            jax.ShapeDtypeStruct((m, D_MODEL), BF16),
            jax.ShapeDtypeStruct((batch, POOL_HIST, POOL_WIDTH), F32),
        ],
        scratch_shapes=[pltpu.VMEM((HIST_PAD + tm, POOL_WIDTH), F32), pltpu.VMEM((tm, POOL_WIDTH), F32)],
        compiler_params=_params("arbitrary"),
        name="mid_prompt",
    )(a, u, u, x, wp_bf, ps, wo_bf, g2)


def _mid_sample_kernel(a_ref, u_ref, hist_ref, x_ref, wp_ref, ps_ref, wo_ref, g2_ref,
                       x1_ref, h2_ref, newhist_ref, ext_ref, d_ref, *, n_seq, t_new, start_pos):
    stride = POOL_HIST + t_new
    stride += -stride % 8
    pos = start_pos + lax.broadcasted_iota(jnp.int32, (t_new, 1), 0)
    for b in range(n_seq):
        base = b * stride
        ext_ref[base:base + POOL_HIST, :] = hist_ref[b]
        ext_ref[base + POOL_HIST:base + POOL_HIST + t_new, :] = u_ref[b * t_new:(b + 1) * t_new, :]
        _pool_diff(ext_ref, base + POOL_HIST, t_new, pos, d_ref, b * t_new)
        newhist_ref[b] = ext_ref[pl.ds(base + t_new, POOL_HIST), :]
    x1 = x_ref[...] + _mix_out(a_ref[...].astype(BF16), d_ref, wp_ref, ps_ref, wo_ref)
    x1_ref[...] = x1
    h2_ref[...] = _rmsnorm(x1, g2_ref[...]).astype(BF16)


def _mid_sample(a, u, hist, x, wp_bf, ps, wo_bf, g2, n_seq, t_new, start_pos):
    m = x.shape[0]
    stride = POOL_HIST + t_new
    stride += -stride % 8
    return pl.pallas_call(
        functools.partial(_mid_sample_kernel, n_seq=n_seq, t_new=t_new, start_pos=start_pos),
        out_shape=[
            jax.ShapeDtypeStruct((m, D_MODEL), F32),
            jax.ShapeDtypeStruct((m, D_MODEL), BF16),
            jax.ShapeDtypeStruct((n_seq, POOL_HIST, POOL_WIDTH), F32),
        ],
        scratch_shapes=[pltpu.VMEM((n_seq * stride, POOL_WIDTH), F32), pltpu.VMEM((m, POOL_WIDTH), F32)],
        compiler_params=pltpu.CompilerParams(vmem_limit_bytes=VMEM_LIMIT_BYTES),
        name="mid_sample",
    )(a, u, hist, x, wp_bf, ps, wo_bf, g2)


def _ffn_kernel(h2_ref, wg_ref, wu_ref, wd_ref, x1_ref, gf_ref, y_ref):
    j = pl.program_id(1)
    h2 = h2_ref[...]
    gate = jnp.dot(h2, wg_ref[...], preferred_element_type=F32)
    up = jnp.dot(h2, wu_ref[...], preferred_element_type=F32)
    act = (gate * jax.nn.sigmoid(gate) * up).astype(BF16)
    part = jnp.dot(act, wd_ref[...], preferred_element_type=F32)

    @pl.when(j == 0)
    def _():
        y_ref[...] = part

    @pl.when(j > 0)
    def _():
        y_ref[...] = y_ref[...] + part

    @pl.when(j == pl.num_programs(1) - 1)
    def _():
        y_ref[...] = _rmsnorm(x1_ref[...] + y_ref[...], gf_ref[...])


def _ffn(h2, wg_bf, wu_bf, wd_bf, x1, gf, tm, tf=512):
    m = h2.shape[0]
    d_ff = wg_bf.shape[1]
    row = lambda i, j: (i, 0)
    return pl.pallas_call(
        _ffn_kernel,
        grid=(m // tm, d_ff // tf),
        in_specs=[
            pl.BlockSpec((tm, D_MODEL), row),
            pl.BlockSpec((D_MODEL, tf), lambda i, j: (0, j)),
            pl.BlockSpec((D_MODEL, tf), lambda i, j: (0, j)),
            pl.BlockSpec((tf, D_MODEL), lambda i, j: (j, 0)),
            pl.BlockSpec((tm, D_MODEL), row),
            pl.BlockSpec((1, D_MODEL), lambda i, j: (0, 0)),
        ],
        out_specs=pl.BlockSpec((tm, D_MODEL), row),
        out_shape=jax.ShapeDtypeStruct((m, D_MODEL), F32),
        compiler_params=_params("arbitrary", "arbitrary"),
        name="ffn",
    )(h2, wg_bf, wu_bf, wd_bf, x1, gf)


def _kmean_sample_kernel(pt_ref, *refs, pages_per_step):
    page_refs, o_ref = refs[:pages_per_step], refs[pages_per_step]
    for jb in range(pages_per_step // PAGES_PER_BLOCK):
        tot = jnp.sum(page_refs[PAGES_PER_BLOCK * jb][...], axis=0)
        for r in range(1, PAGES_PER_BLOCK):
            tot = tot + jnp.sum(page_refs[PAGES_PER_BLOCK * jb + r][...], axis=0)
        mean = tot * (1.0 / MOBA_BLOCK)
        for hd in range(N_HEADS):
            o_ref[jb:jb + 1, hd * HEAD_DIM:(hd + 1) * HEAD_DIM] = mean[hd:hd + 1, :]


def _kmean_sample(page_table_flat, cache_k, n_seq, n_pages, pages_per_step=16):
    steps = n_pages // pages_per_step
    blocks_per_step = pages_per_step // PAGES_PER_BLOCK

    def page_spec(r):
        return pl.BlockSpec(
            (None, None, PAGE_SIZE, N_HEADS, HEAD_DIM),
            lambda b, s, pt: (0, pt[b * n_pages + s * pages_per_step + r], 0, 0, 0))

    grid_spec = pltpu.PrefetchScalarGridSpec(
        num_scalar_prefetch=1,
        grid=(n_seq, steps),
        in_specs=[page_spec(r) for r in range(pages_per_step)],
        out_specs=pl.BlockSpec((None, blocks_per_step, ATTN_WIDTH), lambda b, s, pt: (b, s, 0)),
    )
    return pl.pallas_call(
        functools.partial(_kmean_sample_kernel, pages_per_step=pages_per_step),
        grid_spec=grid_spec,
        out_shape=jax.ShapeDtypeStruct((n_seq, n_pages // PAGES_PER_BLOCK, ATTN_WIDTH), F32),
        compiler_params=_params("arbitrary", "arbitrary"),
        name="kmean_sample",
    )(page_table_flat, *([cache_k] * pages_per_step))


def _select_sample_kernel(qh_ref, km_ref, idx_ref, sc_ref, *, n_seq, t_new, n_past_blocks):
    lanes = sc_ref.shape[1]
    lane = lax.broadcasted_iota(jnp.int32, sc_ref.shape, 1)
    sc_ref[...] = jnp.where(lane == n_past_blocks, NEG, -jnp.inf)
    for b in range(n_seq):
        for hd in range(N_HEADS):
            qbh = qh_ref[hd, b * t_new:(b + 1) * t_new, :]
            kmbh = km_ref[b, :, hd * HEAD_DIM:(hd + 1) * HEAD_DIM]
            r0 = (b * N_HEADS + hd) * t_new
            sc_ref[r0:r0 + t_new, 0:n_past_blocks] = lax.dot_general(
                qbh, kmbh, _NT, precision=lax.Precision.HIGHEST, preferred_element_type=F32)
    s = sc_ref[...]
    for r in range(MOBA_TOPK):
        mx = jnp.max(s, axis=-1, keepdims=True)
        am = jnp.min(jnp.where(s == mx, lane, lanes), axis=-1, keepdims=True)
        idx_ref[:, r:r + 1] = am
        s = jnp.where(lane == am, -jnp.inf, s)


def _select_sample(qh, km, n_seq, t_new):
    n_past_blocks = km.shape[1]
    rows = n_seq * N_HEADS * t_new
    return pl.pallas_call(
        functools.partial(_select_sample_kernel, n_seq=n_seq, t_new=t_new, n_past_blocks=n_past_blocks),
        out_shape=jax.ShapeDtypeStruct((rows, MOBA_TOPK), jnp.int32),
        scratch_shapes=[pltpu.VMEM((rows, 128), F32)],
        compiler_params=pltpu.CompilerParams(vmem_limit_bytes=VMEM_LIMIT_BYTES),
        name="select_sample",
    )(qh, km)


def _attn_sample_kernel(pt_ref, idx_ref, slopes_ref, qh_ref, kh_ref, vh_ref, ck_hbm, cv_hbm, o_ref,
                        kbuf, vbuf, sem, *, n_seq, t_new, n_pages):
    n_past_blocks = n_pages // PAGES_PER_BLOCK
    past_len = n_pages * PAGE_SIZE
    n_slots = t_new * MOBA_TOPK
    b = pl.program_id(0)
    hd = pl.program_id(1)
    step = b * N_HEADS + hd
    buf = lax.rem(step, 2)

    def block_of(bb, hh, qj):
        return idx_ref[(bb * N_HEADS + hh) * n_slots + qj]

    def gather(bb, hh, sl, start):
        for qj in range(n_slots):
            blk = jnp.minimum(block_of(bb, hh, qj), n_past_blocks - 1)
            for half in range(PAGES_PER_BLOCK):
                phys = pt_ref[bb * n_pages + blk * PAGES_PER_BLOCK + half]
                rows = pl.ds(half * PAGE_SIZE, PAGE_SIZE)
                for src, dst, s in ((ck_hbm, kbuf, 0), (cv_hbm, vbuf, 1)):
                    cp = pltpu.make_async_copy(src.at[0, phys, :, hh, :], dst.at[sl, qj, rows, :], sem.at[s, sl])
                    if start:
                        cp.start()
                    else:
                        cp.wait()

    @pl.when(step == 0)
    def _():
        gather(b, hd, buf, True)

    @pl.when(step + 1 < n_seq * N_HEADS)
    def _():
        nxt = step + 1
        gather(nxt // N_HEADS, lax.rem(nxt, N_HEADS), 1 - buf, True)

    gather(b, hd, buf, False)

    slope = slopes_ref[hd]
    rows = pl.ds(b * t_new, t_new)
    qs = (qh_ref[rows, :] * ATTN_SCALE).astype(BF16)
    row = lax.broadcasted_iota(jnp.int32, (t_new, MOBA_BLOCK), 0)
    col = lax.broadcasted_iota(jnp.int32, (t_new, MOBA_BLOCK), 1)

    pieces = []
    for qj in range(n_slots):
        blk = block_of(b, hd, qj)
        s = lax.dot_general(qs, kbuf[buf, qj].astype(BF16), _NT, preferred_element_type=F32)
        dist = (past_len + row - (blk * MOBA_BLOCK + col)).astype(F32)
        keep = jnp.logical_and(row == qj // MOBA_TOPK, blk < n_past_blocks)
        pieces.append(jnp.where(keep, s - slope * dist, NEG))
    own_rel = (lax.broadcasted_iota(jnp.int32, (t_new, t_new), 0)
               - lax.broadcasted_iota(jnp.int32, (t_new, t_new), 1))
    s_own = lax.dot_general(qs, kh_ref[rows, :].astype(BF16), _NT, preferred_element_type=F32)
    s_own = jnp.where(own_rel >= 0, s_own - slope * own_rel.astype(F32), NEG)

    m = jnp.max(s_own, axis=-1, keepdims=True)
    for s in pieces:
        m = jnp.maximum(m, jnp.max(s, axis=-1, keepdims=True))
    p_own = jnp.exp(s_own - m)
    l = jnp.sum(p_own, axis=-1, keepdims=True)
    acc = jnp.dot(p_own.astype(BF16), vh_ref[rows, :].astype(BF16), preferred_element_type=F32)
    for qj, s in enumerate(pieces):
        p = jnp.exp(s - m)
        l = l + jnp.sum(p, axis=-1, keepdims=True)
        acc = acc + jnp.dot(p.astype(BF16), vbuf[buf, qj].astype(BF16), preferred_element_type=F32)
    o_ref[...] = acc / l


def _attn_sample(page_table_flat, idx_flat, slopes, qh, kh, vh, cache_k, cache_v, n_seq, t_new, n_pages):
    n_slots = t_new * MOBA_TOPK
    head_blk = pl.BlockSpec((None, n_seq * t_new, HEAD_DIM), lambda b, h, pt, ix, sl: (h, 0, 0))
    grid_spec = pltpu.PrefetchScalarGridSpec(
        num_scalar_prefetch=3,
        grid=(n_seq, N_HEADS),
        in_specs=[head_blk, head_blk, head_blk,
                  pl.BlockSpec(memory_space=pl.ANY), pl.BlockSpec(memory_space=pl.ANY)],
        out_specs=pl.BlockSpec((None, t_new, HEAD_DIM), lambda b, h, pt, ix, sl: (b, 0, h)),
        scratch_shapes=[
            pltpu.VMEM((2, n_slots, MOBA_BLOCK, HEAD_DIM), F32),
            pltpu.VMEM((2, n_slots, MOBA_BLOCK, HEAD_DIM), F32),
            pltpu.SemaphoreType.DMA((2, 2)),
        ],
    )
    return pl.pallas_call(
        functools.partial(_attn_sample_kernel, n_seq=n_seq, t_new=t_new, n_pages=n_pages),
        grid_spec=grid_spec,
        out_shape=jax.ShapeDtypeStruct((n_seq, t_new, ATTN_WIDTH), F32),
        compiler_params=_params("arbitrary", "arbitrary"),
        name="attn_sample",
    )(page_table_flat, idx_flat, slopes, qh, kh, vh, cache_k, cache_v)


def kernel(x_prompt, x_sample, cache_k, cache_v, state_pool, page_table, norm1_g, w_in, w_pool, pool_scale,
           w_out, norm2_g, w_gate, w_up, w_down, norm_f_g):
    depth = w_in.shape[0]
    assert depth == 1, "single-layer step"
    batch, seq, _ = x_prompt.shape
    n_seq, t_new, _ = x_sample.shape
    n_pages = page_table.shape[1]
    past_len = n_pages * PAGE_SIZE
    assert seq % MOBA_BLOCK == 0 and past_len % MOBA_BLOCK == 0 and t_new <= MOBA_BLOCK

    slopes = jnp.exp2(-8.0 * jnp.arange(1, N_HEADS + 1, dtype=F32) / N_HEADS)
    g1 = norm1_g[0].reshape(1, D_MODEL)
    g2 = norm2_g[0].reshape(1, D_MODEL)
    gf = norm_f_g.reshape(1, D_MODEL)
    ps = pool_scale[0].reshape(1, POOL_WIDTH)
    w_in_bf = w_in[0].astype(BF16)
    w_pool_bf = w_pool[0].astype(BF16)
    w_out_bf = w_out[0].astype(BF16)
    w_gate_bf = w_gate[0].astype(BF16)
    w_up_bf = w_up[0].astype(BF16)
    w_down_bf = w_down[0].astype(BF16)

    xp = x_prompt.reshape(batch * seq, D_MODEL)
    q, k, v, u, kb, vb, km = _proj_prompt(xp, g1, w_in_bf)
    km = km.reshape(batch, seq // MOBA_BLOCK, ATTN_WIDTH)
    a = _attn_prompt(slopes, q, kb, vb, km, batch, seq)
    x1, h2, hist_p = _mid_prompt(a, u, xp, w_pool_bf, ps, w_out_bf, g2, batch, seq)
    y_p = _ffn(h2, w_gate_bf, w_up_bf, w_down_bf, x1, gf, tm=512)

    xs = x_sample.reshape(n_seq * t_new, D_MODEL)
    pt_flat = page_table.reshape(-1)
    k_s, v_s, u_s, qh, kh, vh = _proj_sample(xs, g1, w_in_bf)
    km_s = _kmean_sample(pt_flat, cache_k, n_seq, n_pages)
    idx = _select_sample(qh, km_s, n_seq, t_new)
    a_s = _attn_sample(pt_flat, idx.reshape(-1), slopes, qh, kh, vh, cache_k, cache_v, n_seq, t_new, n_pages)
    x1_s, h2_s, hist_s = _mid_sample(a_s.reshape(n_seq * t_new, ATTN_WIDTH), u_s, state_pool[0], xs,
                                     w_pool_bf, ps, w_out_bf, g2, n_seq, t_new, past_len)
    y_s = _ffn(h2_s, w_gate_bf, w_up_bf, w_down_bf, x1_s, gf, tm=n_seq * t_new)

    kv_p = (depth, batch, seq, N_HEADS, HEAD_DIM)
    kv_s = (depth, n_seq, t_new, N_HEADS, HEAD_DIM)
    return (y_p.reshape(batch, seq, D_MODEL), y_s.reshape(n_seq, t_new, D_MODEL),
            k.reshape(kv_p), v.reshape(kv_p), hist_p[None],
            k_s.reshape(kv_s), v_s.reshape(kv_s), hist_s[None])
```

```python
import functools

import jax
import jax.numpy as jnp
from jax import lax
from jax.experimental import pallas as pl
from jax.experimental.pallas import tpu as pltpu

D_MODEL = 2048
HEAD_DIM = 128
N_HEADS = 8
ATTN_WIDTH = N_HEADS * HEAD_DIM
POOL_WIDTH = D_MODEL - ATTN_WIDTH
IN_WIDTH = 3 * ATTN_WIDTH + POOL_WIDTH
MOBA_BLOCK = 256
MOBA_TOPK = 3
PAGE_SIZE = 128
PAGES_PER_BLOCK = MOBA_BLOCK // PAGE_SIZE
POOL_WINDOWS = (2, 4, 8, 16)
POOL_GROUP_W = POOL_WIDTH // len(POOL_WINDOWS)
POOL_HIST = max(POOL_WINDOWS) - 1
HIST_PAD = 16
RMS_EPS = 1e-6
NEG = -1e30
ATTN_SCALE = HEAD_DIM ** -0.5

V7X_VMEM_BYTES = 64 * 1024 * 1024
VMEM_LIMIT_BYTES = V7X_VMEM_BYTES - 8 * 1024 * 1024

F32 = jnp.float32
BF16 = jnp.bfloat16
_NT = (((1,), (1,)), ((), ()))


def _rmsnorm(x, g):
    y = x * lax.rsqrt(jnp.mean(x * x, axis=-1, keepdims=True) + RMS_EPS)
    return y * g


def _params(*semantics):
    return pltpu.CompilerParams(dimension_semantics=semantics, vmem_limit_bytes=VMEM_LIMIT_BYTES)


def _proj_prompt_kernel(x_ref, g_ref, w_ref, k_ref, v_ref, u_ref, qh_ref, kbh_ref, vbh_ref, km_ref, *, tm):
    h = _rmsnorm(x_ref[...], g_ref[...]).astype(BF16)

    def seg(s):
        return jnp.dot(h, w_ref[:, s * ATTN_WIDTH:(s + 1) * ATTN_WIDTH], preferred_element_type=F32)

    zq = seg(0)
    for hd in range(N_HEADS):
        qh_ref[hd] = zq[:, hd * HEAD_DIM:(hd + 1) * HEAD_DIM]
    zk = seg(1)
    k_ref[...] = zk
    for hd in range(N_HEADS):
        kbh_ref[hd] = zk[:, hd * HEAD_DIM:(hd + 1) * HEAD_DIM].astype(BF16)
    for blk in range(tm // MOBA_BLOCK):
        rows = zk[blk * MOBA_BLOCK:(blk + 1) * MOBA_BLOCK]
        km_ref[blk] = jnp.sum(rows, axis=0, keepdims=True) * (1.0 / MOBA_BLOCK)
    zv = seg(2)
    v_ref[...] = zv
    for hd in range(N_HEADS):
        vbh_ref[hd] = zv[:, hd * HEAD_DIM:(hd + 1) * HEAD_DIM].astype(BF16)
    u_ref[...] = seg(3)


def _proj_prompt(x, g, w_bf, tm=256):
    m = x.shape[0]
    row = lambda i: (i, 0)
    f32_out = jax.ShapeDtypeStruct((m, ATTN_WIDTH), F32)
    km_out = jax.ShapeDtypeStruct((m // MOBA_BLOCK, 1, ATTN_WIDTH), F32)
    blk = pl.BlockSpec((tm, ATTN_WIDTH), row)
    hm_blk = pl.BlockSpec((N_HEADS, tm, HEAD_DIM), lambda i: (0, i, 0))
    return pl.pallas_call(
        functools.partial(_proj_prompt_kernel, tm=tm),
        grid=(m // tm,),
        in_specs=[
            pl.BlockSpec((tm, D_MODEL), row),
            pl.BlockSpec((1, D_MODEL), lambda i: (0, 0)),
            pl.BlockSpec((D_MODEL, IN_WIDTH), lambda i: (0, 0), pipeline_mode=pl.Buffered(1)),
        ],
        out_specs=[blk, blk, blk, hm_blk, hm_blk, hm_blk,
                   pl.BlockSpec((tm // MOBA_BLOCK, 1, ATTN_WIDTH), lambda i: (i, 0, 0))],
        out_shape=[f32_out, f32_out, f32_out,
                   jax.ShapeDtypeStruct((N_HEADS, m, HEAD_DIM), F32),
                   jax.ShapeDtypeStruct((N_HEADS, m, HEAD_DIM), BF16),
                   jax.ShapeDtypeStruct((N_HEADS, m, HEAD_DIM), BF16), km_out],
        compiler_params=_params("arbitrary"),
        name="proj_prompt",
    )(x, g, w_bf)


def _proj_sample_kernel(x_ref, g_ref, w_ref, k_ref, v_ref, u_ref, qh_ref, kh_ref, vh_ref):
    h = _rmsnorm(x_ref[...], g_ref[...]).astype(BF16)
    zs = [jnp.dot(h, w_ref[:, s * ATTN_WIDTH:(s + 1) * ATTN_WIDTH], preferred_element_type=F32)
          for s in range(4)]
    k_ref[...] = zs[1]
    v_ref[...] = zs[2]
    u_ref[...] = zs[3]
    for hd in range(N_HEADS):
        ls = slice(hd * HEAD_DIM, (hd + 1) * HEAD_DIM)
        qh_ref[hd] = zs[0][:, ls]
        kh_ref[hd] = zs[1][:, ls]
        vh_ref[hd] = zs[2][:, ls]


def _proj_sample(x, g, w_bf):
    m = x.shape[0]
    f32_out = jax.ShapeDtypeStruct((m, ATTN_WIDTH), F32)
    hm_out = jax.ShapeDtypeStruct((N_HEADS, m, HEAD_DIM), F32)
    return pl.pallas_call(
        _proj_sample_kernel,
        out_shape=[f32_out, f32_out, f32_out, hm_out, hm_out, hm_out],
        compiler_params=pltpu.CompilerParams(vmem_limit_bytes=VMEM_LIMIT_BYTES),
        name="proj_sample",
    )(x, g, w_bf)


def _alibi_key_table(slopes, seq):
    nb = seq // MOBA_BLOCK
    t = jnp.arange(seq, dtype=jnp.int32)
    blk, local = t // MOBA_BLOCK, t % MOBA_BLOCK
    onehot = (blk[:, None] == jnp.arange(nb, dtype=jnp.int32)[None, :]).astype(F32)
    onehot = jnp.broadcast_to(onehot[None], (N_HEADS, seq, nb))
    pos = jnp.stack([local, blk * MOBA_BLOCK], axis=-1).astype(F32)[None] * slopes[:, None, None]
    pad = jnp.zeros((N_HEADS, seq, HEAD_DIM - nb - 2), F32)
    return jnp.concatenate([onehot, pos, pad], axis=-1).astype(BF16)


def _topk_block_bias_t(scores_t, n_past):
    nb = scores_t.shape[0]
    n_iota = lax.broadcasted_iota(jnp.int32, scores_t.shape, 0)
    past = n_iota < n_past
    s = jnp.where(past, scores_t, NEG)
    rank = jnp.zeros(scores_t.shape, jnp.int32)
    for m in range(nb):
        sm = s[m:m + 1, :]
        tie_lower = jnp.logical_and(sm == s, n_iota > m)
        rank = rank + jnp.where(sm > s, 1, jnp.where(tie_lower, 1, 0))
    keep = jnp.logical_or(jnp.logical_and(past, rank < MOBA_TOPK), n_iota == n_past)
    return jnp.where(keep, 0.0, NEG)


def _attn_prompt_kernel(q_ref, kb_ref, vb_ref, km_ref, kx_ref, o_ref, *, nb):
    km = km_ref[...]
    shape = (MOBA_BLOCK, MOBA_BLOCK)
    causal = lax.broadcasted_iota(jnp.int32, shape, 0) >= lax.broadcasted_iota(jnp.int32, shape, 1)
    ones_rows = jnp.where(lax.broadcasted_iota(jnp.int32, (8, MOBA_BLOCK), 0) < 2, 1.0, 0.0)
    zero_rows = jnp.zeros((HEAD_DIM - nb - 8, MOBA_BLOCK), F32)
    for qi in range(nb):
        rows = slice(qi * MOBA_BLOCK, (qi + 1) * MOBA_BLOCK)
        n_keys = (qi + 1) * MOBA_BLOCK
        q = q_ref[rows, :]
        sel_t = lax.dot_general(km, q, _NT, precision=lax.Precision.HIGHEST, preferred_element_type=F32)
        bias_t = _topk_block_bias_t(sel_t, qi)
        extra = jnp.concatenate([bias_t, ones_rows, zero_rows], axis=0).T.astype(BF16)
        qs = (q * ATTN_SCALE).astype(BF16)
        s = (lax.dot_general(qs, kb_ref[0:n_keys, :], _NT, preferred_element_type=F32)
             + lax.dot_general(extra, kx_ref[0:n_keys, :], _NT, preferred_element_type=F32))
        s_own = jnp.where(causal, s[:, qi * MOBA_BLOCK:], NEG)
        s = s_own if qi == 0 else jnp.concatenate([s[:, :qi * MOBA_BLOCK], s_own], axis=-1)
        m = jnp.max(s, axis=-1, keepdims=True)
        p = jnp.exp(s - m)
        l = jnp.sum(p, axis=-1, keepdims=True)
        o = jnp.dot(p.astype(BF16), vb_ref[0:n_keys, :], preferred_element_type=F32)
        o_ref[rows, :] = (o / l).astype(o_ref.dtype)


def _attn_prompt(qh, kbh, vbh, km, kx, batch, seq):
    nb = seq // MOBA_BLOCK
    assert nb == 8, "bias rows fill one sublane tile"
    slab = pl.BlockSpec((None, seq, HEAD_DIM), lambda b, h: (h, b, 0))
    return pl.pallas_call(
        functools.partial(_attn_prompt_kernel, nb=nb),
        grid=(batch, N_HEADS),
        in_specs=[
            slab, slab, slab,
            pl.BlockSpec((None, nb, HEAD_DIM), lambda b, h: (b, 0, h)),
            pl.BlockSpec((None, seq, HEAD_DIM), lambda b, h: (h, 0, 0)),
        ],
        out_specs=slab,
        out_shape=jax.ShapeDtypeStruct((N_HEADS, batch * seq, HEAD_DIM), BF16),
        compiler_params=_params("arbitrary", "arbitrary"),
        name="attn_prompt",
    )(qh, kbh, vbh, km, kx)


def _pool_diff(ext_ref, r0, n, pos, d_ref, d0):
    for g, w in enumerate(POOL_WINDOWS):
        ls = slice(g * POOL_GROUP_W, (g + 1) * POOL_GROUP_W)
        cur = ext_ref[pl.ds(r0, n), ls]
        win = cur
        for j in range(1, w):
            win = win + ext_ref[pl.ds(r0 - j, n), ls]
        cnt = jnp.minimum(pos + 1, w).astype(F32)
        d_ref[pl.ds(d0, n), ls] = win / cnt - cur


def _mix_out(a_bf, d_ref, wp_ref, ps_ref, wo_ref):
    ys = []
    for g in range(len(POOL_WINDOWS)):
        ls = slice(g * POOL_GROUP_W, (g + 1) * POOL_GROUP_W)
        y = jnp.dot(d_ref[:, ls].astype(BF16), wp_ref[g], preferred_element_type=F32)
        ys.append((y * ps_ref[:, ls]).astype(BF16))
    mix = jnp.concatenate([a_bf] + ys, axis=-1)
    return jnp.dot(mix, wo_ref[...], preferred_element_type=F32)


def _mid_prompt_kernel(a_ref, u_ref, uprev_ref, x_ref, wp_ref, ps_ref, wo_ref, g2_ref,
                       x1_ref, h2_ref, hist_ref, ext_ref, d_ref, *, tm, tiles_per_seq):
    t_in_seq = lax.rem(pl.program_id(0), tiles_per_seq)
    ext_ref[0:HIST_PAD, :] = jnp.where(t_in_seq == 0, 0.0, uprev_ref[...])
    ext_ref[HIST_PAD:HIST_PAD + tm, :] = u_ref[...]
    pos = t_in_seq * tm + lax.broadcasted_iota(jnp.int32, (tm, 1), 0)
    _pool_diff(ext_ref, HIST_PAD, tm, pos, d_ref, 0)
    a = jnp.concatenate([a_ref[hd] for hd in range(N_HEADS)], axis=-1)
    x1 = x_ref[...] + _mix_out(a, d_ref, wp_ref, ps_ref, wo_ref)
    x1_ref[...] = x1
    h2_ref[...] = _rmsnorm(x1, g2_ref[...]).astype(BF16)

    @pl.when(t_in_seq == tiles_per_seq - 1)
    def _():
        hist_ref[...] = ext_ref[pl.ds(HIST_PAD + tm - POOL_HIST, POOL_HIST), :]


def _mid_prompt(a, u, x, wp_bf, ps, wo_bf, g2, batch, seq, tm=256):
    m = x.shape[0]
    tiles_per_seq = seq // tm
    row = lambda i: (i, 0)
    const2 = lambda i: (0, 0)
    return pl.pallas_call(
        functools.partial(_mid_prompt_kernel, tm=tm, tiles_per_seq=tiles_per_seq),
        grid=(m // tm,),
        in_specs=[
            pl.BlockSpec((N_HEADS, tm, HEAD_DIM), lambda i: (0, i, 0)),
            pl.BlockSpec((tm, POOL_WIDTH), row),
            pl.BlockSpec((HIST_PAD, POOL_WIDTH), lambda i: (jnp.maximum(i * (tm // HIST_PAD) - 1, 0), 0)),
            pl.BlockSpec((tm, D_MODEL), row),
            pl.BlockSpec(wp_bf.shape, lambda i: (0, 0, 0), pipeline_mode=pl.Buffered(1)),
            pl.BlockSpec((1, POOL_WIDTH), const2),
            pl.BlockSpec((D_MODEL, D_MODEL), const2, pipeline_mode=pl.Buffered(1)),
            pl.BlockSpec((1, D_MODEL), const2),
        ],
        out_specs=[
            pl.BlockSpec((tm, D_MODEL), row),
            pl.BlockSpec((tm, D_MODEL), row),
            pl.BlockSpec((None, POOL_HIST, POOL_WIDTH), lambda i: (i // tiles_per_seq, 0, 0)),
        ],
        out_shape=[
            jax.ShapeDtypeStruct((m, D_MODEL), F32),
            jax.ShapeDtypeStruct((m, D_MODEL), BF16),
            jax.ShapeDtypeStruct((batch, POOL_HIST, POOL_WIDTH), F32),
        ],
        scratch_shapes=[pltpu.VMEM((HIST_PAD + tm, POOL_WIDTH), F32), pltpu.VMEM((tm, POOL_WIDTH), F32)],
        compiler_params=_params("arbitrary"),
        name="mid_prompt",
    )(a, u, u, x, wp_bf, ps, wo_bf, g2)


def _mid_sample_kernel(a_ref, u_ref, hist_ref, x_ref, wp_ref, ps_ref, wo_ref, g2_ref,
                       x1_ref, h2_ref, newhist_ref, ext_ref, d_ref, *, n_seq, t_new, start_pos):
    stride = POOL_HIST + t_new
    stride += -stride % 8
    pos = start_pos + lax.broadcasted_iota(jnp.int32, (t_new, 1), 0)
    for b in range(n_seq):
        base = b * stride
        ext_ref[base:base + POOL_HIST, :] = hist_ref[b]
        ext_ref[base + POOL_HIST:base + POOL_HIST + t_new, :] = u_ref[b * t_new:(b + 1) * t_new, :]
        _pool_diff(ext_ref, base + POOL_HIST, t_new, pos, d_ref, b * t_new)
        newhist_ref[b] = ext_ref[pl.ds(base + t_new, POOL_HIST), :]
    x1 = x_ref[...] + _mix_out(a_ref[...].astype(BF16), d_ref, wp_ref, ps_ref, wo_ref)
    x1_ref[...] = x1
    h2_ref[...] = _rmsnorm(x1, g2_ref[...]).astype(BF16)


def _mid_sample(a, u, hist, x, wp_bf, ps, wo_bf, g2, n_seq, t_new, start_pos):
    m = x.shape[0]
    stride = POOL_HIST + t_new
    stride += -stride % 8
    return pl.pallas_call(
        functools.partial(_mid_sample_kernel, n_seq=n_seq, t_new=t_new, start_pos=start_pos),
        out_shape=[
            jax.ShapeDtypeStruct((m, D_MODEL), F32),
            jax.ShapeDtypeStruct((m, D_MODEL), BF16),
            jax.ShapeDtypeStruct((n_seq, POOL_HIST, POOL_WIDTH), F32),
        ],
        scratch_shapes=[pltpu.VMEM((n_seq * stride, POOL_WIDTH), F32), pltpu.VMEM((m, POOL_WIDTH), F32)],
        compiler_params=pltpu.CompilerParams(vmem_limit_bytes=VMEM_LIMIT_BYTES),
        name="mid_sample",
    )(a, u, hist, x, wp_bf, ps, wo_bf, g2)


def _ffn_kernel(h2_ref, wg_ref, wu_ref, wd_ref, x1_ref, gf_ref, y_ref):
    j = pl.program_id(1)
    h2 = h2_ref[...]
    gate = jnp.dot(h2, wg_ref[...], preferred_element_type=F32)
    up = jnp.dot(h2, wu_ref[...], preferred_element_type=F32)
    act = (gate * jax.nn.sigmoid(gate) * up).astype(BF16)
    part = jnp.dot(act, wd_ref[...], preferred_element_type=F32)

    @pl.when(j == 0)
    def _():
        y_ref[...] = part

    @pl.when(j > 0)
    def _():
        y_ref[...] = y_ref[...] + part

    @pl.when(j == pl.num_programs(1) - 1)
    def _():
        y_ref[...] = _rmsnorm(x1_ref[...] + y_ref[...], gf_ref[...])


def _ffn(h2, wg_bf, wu_bf, wd_bf, x1, gf, tm, tf=512):
    m = h2.shape[0]
    d_ff = wg_bf.shape[1]
    row = lambda i, j: (i, 0)
    return pl.pallas_call(
        _ffn_kernel,
        grid=(m // tm, d_ff // tf),
        in_specs=[
            pl.BlockSpec((tm, D_MODEL), row),
            pl.BlockSpec((D_MODEL, tf), lambda i, j: (0, j)),
            pl.BlockSpec((D_MODEL, tf), lambda i, j: (0, j)),
            pl.BlockSpec((tf, D_MODEL), lambda i, j: (j, 0)),
            pl.BlockSpec((tm, D_MODEL), row),
            pl.BlockSpec((1, D_MODEL), lambda i, j: (0, 0)),
        ],
        out_specs=pl.BlockSpec((tm, D_MODEL), row),
        out_shape=jax.ShapeDtypeStruct((m, D_MODEL), F32),
        compiler_params=_params("arbitrary", "arbitrary"),
        name="ffn",
    )(h2, wg_bf, wu_bf, wd_bf, x1, gf)


def _kmean_sample_kernel(pt_ref, *refs, pages_per_step):
    page_refs, o_ref = refs[:pages_per_step], refs[pages_per_step]
    for jb in range(pages_per_step // PAGES_PER_BLOCK):
        tot = jnp.sum(page_refs[PAGES_PER_BLOCK * jb][...], axis=0)
        for r in range(1, PAGES_PER_BLOCK):
            tot = tot + jnp.sum(page_refs[PAGES_PER_BLOCK * jb + r][...], axis=0)
        mean = tot * (1.0 / MOBA_BLOCK)
        for hd in range(N_HEADS):
            o_ref[jb:jb + 1, hd * HEAD_DIM:(hd + 1) * HEAD_DIM] = mean[hd:hd + 1, :]


def _kmean_sample(page_table_flat, cache_k, n_seq, n_pages, pages_per_step=16):
    steps = n_pages // pages_per_step
    blocks_per_step = pages_per_step // PAGES_PER_BLOCK

    def page_spec(r):
        return pl.BlockSpec(
            (None, None, PAGE_SIZE, N_HEADS, HEAD_DIM),
            lambda b, s, pt: (0, pt[b * n_pages + s * pages_per_step + r], 0, 0, 0))

    grid_spec = pltpu.PrefetchScalarGridSpec(
        num_scalar_prefetch=1,
        grid=(n_seq, steps),
        in_specs=[page_spec(r) for r in range(pages_per_step)],
        out_specs=pl.BlockSpec((None, blocks_per_step, ATTN_WIDTH), lambda b, s, pt: (b, s, 0)),
    )
    return pl.pallas_call(
        functools.partial(_kmean_sample_kernel, pages_per_step=pages_per_step),
        grid_spec=grid_spec,
        out_shape=jax.ShapeDtypeStruct((n_seq, n_pages // PAGES_PER_BLOCK, ATTN_WIDTH), F32),
        compiler_params=_params("arbitrary", "arbitrary"),
        name="kmean_sample",
    )(page_table_flat, *([cache_k] * pages_per_step))


def _select_sample_kernel(qh_ref, km_ref, idx_ref, sc_ref, *, n_seq, t_new, n_past_blocks):
    lanes = sc_ref.shape[1]
    lane = lax.broadcasted_iota(jnp.int32, sc_ref.shape, 1)
    sc_ref[...] = jnp.where(lane == n_past_blocks, NEG, -jnp.inf)
    for b in range(n_seq):
        for hd in range(N_HEADS):
            qbh = qh_ref[hd, b * t_new:(b + 1) * t_new, :]
            kmbh = km_ref[b, :, hd * HEAD_DIM:(hd + 1) * HEAD_DIM]
            r0 = (b * N_HEADS + hd) * t_new
            sc_ref[r0:r0 + t_new, 0:n_past_blocks] = lax.dot_general(
                qbh, kmbh, _NT, precision=lax.Precision.HIGHEST, preferred_element_type=F32)
    s = sc_ref[...]
    for r in range(MOBA_TOPK):
        mx = jnp.max(s, axis=-1, keepdims=True)
        am = jnp.min(jnp.where(s == mx, lane, lanes), axis=-1, keepdims=True)
        idx_ref[:, r:r + 1] = am
        s = jnp.where(lane == am, -jnp.inf, s)


def _select_sample(qh, km, n_seq, t_new):
    n_past_blocks = km.shape[1]
    rows = n_seq * N_HEADS * t_new
    return pl.pallas_call(
        functools.partial(_select_sample_kernel, n_seq=n_seq, t_new=t_new, n_past_blocks=n_past_blocks),
        out_shape=jax.ShapeDtypeStruct((rows, MOBA_TOPK), jnp.int32),
        scratch_shapes=[pltpu.VMEM((rows, 128), F32)],
        compiler_params=pltpu.CompilerParams(vmem_limit_bytes=VMEM_LIMIT_BYTES),
        name="select_sample",
    )(qh, km)


def _attn_sample_kernel(pt_ref, idx_ref, slopes_ref, qh_ref, kh_ref, vh_ref, ck_hbm, cv_hbm, o_ref,
                        kbuf, vbuf, sem, *, n_seq, t_new, n_pages):
    n_past_blocks = n_pages // PAGES_PER_BLOCK
    past_len = n_pages * PAGE_SIZE
    n_slots = t_new * MOBA_TOPK
    b = pl.program_id(0)
    hd = pl.program_id(1)
    step = b * N_HEADS + hd
    buf = lax.rem(step, 2)

    def block_of(bb, hh, qj):
        return idx_ref[(bb * N_HEADS + hh) * n_slots + qj]

    def gather(bb, hh, sl, start):
        for qj in range(n_slots):
            blk = jnp.minimum(block_of(bb, hh, qj), n_past_blocks - 1)
            for half in range(PAGES_PER_BLOCK):
                phys = pt_ref[bb * n_pages + blk * PAGES_PER_BLOCK + half]
                rows = pl.ds(half * PAGE_SIZE, PAGE_SIZE)
                for src, dst, s in ((ck_hbm, kbuf, 0), (cv_hbm, vbuf, 1)):
                    cp = pltpu.make_async_copy(src.at[0, phys, :, hh, :], dst.at[sl, qj, rows, :], sem.at[s, sl])
                    if start:
                        cp.start()
                    else:
                        cp.wait()

    @pl.when(step == 0)
    def _():
        gather(b, hd, buf, True)

    @pl.when(step + 1 < n_seq * N_HEADS)
    def _():
        nxt = step + 1
        gather(nxt // N_HEADS, lax.rem(nxt, N_HEADS), 1 - buf, True)

    gather(b, hd, buf, False)

    slope = slopes_ref[hd]
    rows = pl.ds(b * t_new, t_new)
    qs = (qh_ref[rows, :] * ATTN_SCALE).astype(BF16)
    row = lax.broadcasted_iota(jnp.int32, (t_new, MOBA_BLOCK), 0)
    col = lax.broadcasted_iota(jnp.int32, (t_new, MOBA_BLOCK), 1)

    pieces = []
    for qj in range(n_slots):
        blk = block_of(b, hd, qj)
        s = lax.dot_general(qs, kbuf[buf, qj].astype(BF16), _NT, preferred_element_type=F32)
        dist = (past_len + row - (blk * MOBA_BLOCK + col)).astype(F32)
        keep = jnp.logical_and(row == qj // MOBA_TOPK, blk < n_past_blocks)
        pieces.append(jnp.where(keep, s - slope * dist, NEG))
    own_rel = (lax.broadcasted_iota(jnp.int32, (t_new, t_new), 0)
               - lax.broadcasted_iota(jnp.int32, (t_new, t_new), 1))
    s_own = lax.dot_general(qs, kh_ref[rows, :].astype(BF16), _NT, preferred_element_type=F32)
    s_own = jnp.where(own_rel >= 0, s_own - slope * own_rel.astype(F32), NEG)

    m = jnp.max(s_own, axis=-1, keepdims=True)
    for s in pieces:
        m = jnp.maximum(m, jnp.max(s, axis=-1, keepdims=True))
    p_own = jnp.exp(s_own - m)
    l = jnp.sum(p_own, axis=-1, keepdims=True)
    acc = jnp.dot(p_own.astype(BF16), vh_ref[rows, :].astype(BF16), preferred_element_type=F32)
    for qj, s in enumerate(pieces):
        p = jnp.exp(s - m)
        l = l + jnp.sum(p, axis=-1, keepdims=True)
        acc = acc + jnp.dot(p.astype(BF16), vbuf[buf, qj].astype(BF16), preferred_element_type=F32)
    o_ref[...] = acc / l


def _attn_sample(page_table_flat, idx_flat, slopes, qh, kh, vh, cache_k, cache_v, n_seq, t_new, n_pages):
    n_slots = t_new * MOBA_TOPK
    head_blk = pl.BlockSpec((None, n_seq * t_new, HEAD_DIM), lambda b, h, pt, ix, sl: (h, 0, 0))
    grid_spec = pltpu.PrefetchScalarGridSpec(
        num_scalar_prefetch=3,
        grid=(n_seq, N_HEADS),
        in_specs=[head_blk, head_blk, head_blk,
                  pl.BlockSpec(memory_space=pl.ANY), pl.BlockSpec(memory_space=pl.ANY)],
        out_specs=pl.BlockSpec((None, t_new, HEAD_DIM), lambda b, h, pt, ix, sl: (b, 0, h)),
        scratch_shapes=[
            pltpu.VMEM((2, n_slots, MOBA_BLOCK, HEAD_DIM), F32),
            pltpu.VMEM((2, n_slots, MOBA_BLOCK, HEAD_DIM), F32),
            pltpu.SemaphoreType.DMA((2, 2)),
        ],
    )
    return pl.pallas_call(
        functools.partial(_attn_sample_kernel, n_seq=n_seq, t_new=t_new, n_pages=n_pages),
        grid_spec=grid_spec,
        out_shape=jax.ShapeDtypeStruct((n_seq, t_new, ATTN_WIDTH), F32),
        compiler_params=_params("arbitrary", "arbitrary"),
        name="attn_sample",
    )(page_table_flat, idx_flat, slopes, qh, kh, vh, cache_k, cache_v)


def kernel(x_prompt, x_sample, cache_k, cache_v, state_pool, page_table, norm1_g, w_in, w_pool, pool_scale,
           w_out, norm2_g, w_gate, w_up, w_down, norm_f_g):
    depth = w_in.shape[0]
    assert depth == 1, "single-layer step"
    batch, seq, _ = x_prompt.shape
    n_seq, t_new, _ = x_sample.shape
    n_pages = page_table.shape[1]
    past_len = n_pages * PAGE_SIZE
    assert seq % MOBA_BLOCK == 0 and past_len % MOBA_BLOCK == 0 and t_new <= MOBA_BLOCK

    slopes = jnp.exp2(-8.0 * jnp.arange(1, N_HEADS + 1, dtype=F32) / N_HEADS)
    g1 = norm1_g[0].reshape(1, D_MODEL)
    g2 = norm2_g[0].reshape(1, D_MODEL)
    gf = norm_f_g.reshape(1, D_MODEL)
    ps = pool_scale[0].reshape(1, POOL_WIDTH)
    w_in_bf = w_in[0].astype(BF16)
    w_pool_bf = w_pool[0].astype(BF16)
    w_out_bf = w_out[0].astype(BF16)
    w_gate_bf = w_gate[0].astype(BF16)
    w_up_bf = w_up[0].astype(BF16)
    w_down_bf = w_down[0].astype(BF16)

    xp = x_prompt.reshape(batch * seq, D_MODEL)
    k, v, u, qh_p, kbh, vbh, km = _proj_prompt(xp, g1, w_in_bf)
    km = km.reshape(batch, seq // MOBA_BLOCK, ATTN_WIDTH)
    a = _attn_prompt(qh_p, kbh, vbh, km, _alibi_key_table(slopes, seq), batch, seq)
    x1, h2, hist_p = _mid_prompt(a, u, xp, w_pool_bf, ps, w_out_bf, g2, batch, seq)
    y_p = _ffn(h2, w_gate_bf, w_up_bf, w_down_bf, x1, gf, tm=512)

    xs = x_sample.reshape(n_seq * t_new, D_MODEL)
    pt_flat = page_table.reshape(-1)
    k_s, v_s, u_s, qh, kh, vh = _proj_sample(xs, g1, w_in_bf)
    km_s = _kmean_sample(pt_flat, cache_k, n_seq, n_pages)
    idx = _select_sample(qh, km_s, n_seq, t_new)
    a_s = _attn_sample(pt_flat, idx.reshape(-1), slopes, qh, kh, vh, cache_k, cache_v, n_seq, t_new, n_pages)
    x1_s, h2_s, hist_s = _mid_sample(a_s.reshape(n_seq * t_new, ATTN_WIDTH), u_s, state_pool[0], xs,
                                     w_pool_bf, ps, w_out_bf, g2, n_seq, t_new, past_len)
    y_s = _ffn(h2_s, w_gate_bf, w_up_bf, w_down_bf, x1_s, gf, tm=n_seq * t_new)

    kv_p = (depth, batch, seq, N_HEADS, HEAD_DIM)
    kv_s = (depth, n_seq, t_new, N_HEADS, HEAD_DIM)
    return (y_p.reshape(batch, seq, D_MODEL), y_s.reshape(n_seq, t_new, D_MODEL),
            k.reshape(kv_p), v.reshape(kv_p), hist_p[None],
            k_s.reshape(kv_s), v_s.reshape(kv_s), hist_s[None])
```

```python
import functools

import jax
import jax.numpy as jnp
from jax import lax
from jax.experimental import pallas as pl
from jax.experimental.pallas import tpu as pltpu

D_MODEL = 2048
HEAD_DIM = 128
N_HEADS = 8
ATTN_WIDTH = N_HEADS * HEAD_DIM
POOL_WIDTH = D_MODEL - ATTN_WIDTH
IN_WIDTH = 3 * ATTN_WIDTH + POOL_WIDTH
MOBA_BLOCK = 256
MOBA_TOPK = 3
PAGE_SIZE = 128
PAGES_PER_BLOCK = MOBA_BLOCK // PAGE_SIZE
POOL_WINDOWS = (2, 4, 8, 16)
POOL_GROUP_W = POOL_WIDTH // len(POOL_WINDOWS)
POOL_HIST = max(POOL_WINDOWS) - 1
HIST_PAD = 16
RMS_EPS = 1e-6
NEG = -1e30
ATTN_SCALE = HEAD_DIM ** -0.5

V7X_VMEM_BYTES = 64 * 1024 * 1024
VMEM_LIMIT_BYTES = V7X_VMEM_BYTES - 8 * 1024 * 1024

F32 = jnp.float32
BF16 = jnp.bfloat16
_NT = (((1,), (1,)), ((), ()))


def _rmsnorm(x, g):
    y = x * lax.rsqrt(jnp.mean(x * x, axis=-1, keepdims=True) + RMS_EPS)
    return y * g


def _params(*semantics):
    return pltpu.CompilerParams(dimension_semantics=semantics, vmem_limit_bytes=VMEM_LIMIT_BYTES)


def _proj_prompt_kernel(x_ref, g_ref, w_ref, k_ref, v_ref, u_ref, qh_ref, kbh_ref, vbh_ref, km_ref, *, tm):
    h = _rmsnorm(x_ref[...], g_ref[...]).astype(BF16)

    def seg(s):
        return jnp.dot(h, w_ref[:, s * ATTN_WIDTH:(s + 1) * ATTN_WIDTH], preferred_element_type=F32)

    zq = seg(0)
    for hd in range(N_HEADS):
        qh_ref[hd] = zq[:, hd * HEAD_DIM:(hd + 1) * HEAD_DIM]
    zk = seg(1)
    k_ref[...] = zk
    for hd in range(N_HEADS):
        kbh_ref[hd] = zk[:, hd * HEAD_DIM:(hd + 1) * HEAD_DIM].astype(BF16)
    for blk in range(tm // MOBA_BLOCK):
        rows = zk[blk * MOBA_BLOCK:(blk + 1) * MOBA_BLOCK]
        km_ref[blk] = jnp.sum(rows, axis=0, keepdims=True) * (1.0 / MOBA_BLOCK)
    zv = seg(2)
    v_ref[...] = zv
    for hd in range(N_HEADS):
        vbh_ref[hd] = zv[:, hd * HEAD_DIM:(hd + 1) * HEAD_DIM].astype(BF16)
    u_ref[...] = seg(3)


def _proj_prompt(x, g, w_bf, tm=256):
    m = x.shape[0]
    row = lambda i: (i, 0)
    f32_out = jax.ShapeDtypeStruct((m, ATTN_WIDTH), F32)
    km_out = jax.ShapeDtypeStruct((m // MOBA_BLOCK, 1, ATTN_WIDTH), F32)
    blk = pl.BlockSpec((tm, ATTN_WIDTH), row)
    hm_blk = pl.BlockSpec((N_HEADS, tm, HEAD_DIM), lambda i: (0, i, 0))
    return pl.pallas_call(
        functools.partial(_proj_prompt_kernel, tm=tm),
        grid=(m // tm,),
        in_specs=[
            pl.BlockSpec((tm, D_MODEL), row),
            pl.BlockSpec((1, D_MODEL), lambda i: (0, 0)),
            pl.BlockSpec((D_MODEL, IN_WIDTH), lambda i: (0, 0), pipeline_mode=pl.Buffered(1)),
        ],
        out_specs=[blk, blk, blk, hm_blk, hm_blk, hm_blk,
                   pl.BlockSpec((tm // MOBA_BLOCK, 1, ATTN_WIDTH), lambda i: (i, 0, 0))],
        out_shape=[f32_out, f32_out, f32_out,
                   jax.ShapeDtypeStruct((N_HEADS, m, HEAD_DIM), F32),
                   jax.ShapeDtypeStruct((N_HEADS, m, HEAD_DIM), BF16),
                   jax.ShapeDtypeStruct((N_HEADS, m, HEAD_DIM), BF16), km_out],
        compiler_params=_params("arbitrary"),
        name="proj_prompt",
    )(x, g, w_bf)


def _proj_sample_kernel(x_ref, g_ref, w_ref, k_ref, v_ref, u_ref, qh_ref, kh_ref, vh_ref):
    h = _rmsnorm(x_ref[...], g_ref[...]).astype(BF16)
    zs = [jnp.dot(h, w_ref[:, s * ATTN_WIDTH:(s + 1) * ATTN_WIDTH], preferred_element_type=F32)
          for s in range(4)]
    k_ref[...] = zs[1]
    v_ref[...] = zs[2]
    u_ref[...] = zs[3]
    for hd in range(N_HEADS):
        ls = slice(hd * HEAD_DIM, (hd + 1) * HEAD_DIM)
        qh_ref[hd] = zs[0][:, ls]
        kh_ref[hd] = zs[1][:, ls]
        vh_ref[hd] = zs[2][:, ls]


def _proj_sample(x, g, w_bf):
    m = x.shape[0]
    f32_out = jax.ShapeDtypeStruct((m, ATTN_WIDTH), F32)
    hm_out = jax.ShapeDtypeStruct((N_HEADS, m, HEAD_DIM), F32)
    return pl.pallas_call(
        _proj_sample_kernel,
        out_shape=[f32_out, f32_out, f32_out, hm_out, hm_out, hm_out],
        compiler_params=pltpu.CompilerParams(vmem_limit_bytes=VMEM_LIMIT_BYTES),
        name="proj_sample",
    )(x, g, w_bf)


def _alibi_key_table(slopes, seq):
    nb = seq // MOBA_BLOCK
    t = jnp.arange(seq, dtype=jnp.int32)
    blk, local = t // MOBA_BLOCK, t % MOBA_BLOCK
    onehot = (blk[:, None] == jnp.arange(nb, dtype=jnp.int32)[None, :]).astype(F32)
    onehot = jnp.broadcast_to(onehot[None], (N_HEADS, seq, nb))
    pos = jnp.stack([local, blk * MOBA_BLOCK], axis=-1).astype(F32)[None] * slopes[:, None, None]
    pad = jnp.zeros((N_HEADS, seq, HEAD_DIM - nb - 2), F32)
    return jnp.concatenate([onehot, pos, pad], axis=-1).astype(BF16)


def _topk_block_bias_t(scores_t, n_past):
    nb = scores_t.shape[0]
    n_iota = lax.broadcasted_iota(jnp.int32, scores_t.shape, 0)
    past = n_iota < n_past
    s = jnp.where(past, scores_t, NEG)
    rank = jnp.zeros(scores_t.shape, jnp.int32)
    for m in range(nb):
        sm = s[m:m + 1, :]
        tie_lower = jnp.logical_and(sm == s, n_iota > m)
        rank = rank + jnp.where(sm > s, 1, jnp.where(tie_lower, 1, 0))
    keep = jnp.logical_or(jnp.logical_and(past, rank < MOBA_TOPK), n_iota == n_past)
    return jnp.where(keep, 0.0, NEG)


def _attn_prompt_kernel(q_ref, kb_ref, vb_ref, km_ref, kx_ref, o_ref, qa_ref, ka_ref, s_ref, p_ref, acc_ref, *, nb):
    seq = nb * MOBA_BLOCK
    km = km_ref[...]
    shape = (MOBA_BLOCK, MOBA_BLOCK)
    causal = lax.broadcasted_iota(jnp.int32, shape, 0) >= lax.broadcasted_iota(jnp.int32, shape, 1)

    ka_ref[:, 0:HEAD_DIM] = kb_ref[...]
    ka_ref[:, HEAD_DIM:] = kx_ref[...]
    ones_rows = jnp.where(lax.broadcasted_iota(jnp.int32, (8, MOBA_BLOCK), 0) < 2, 1.0, 0.0)
    zero_rows = jnp.zeros((HEAD_DIM - nb - 8, MOBA_BLOCK), F32)
    for qi in range(nb):
        rows = slice(qi * MOBA_BLOCK, (qi + 1) * MOBA_BLOCK)
        q = q_ref[rows, :]
        sel_t = lax.dot_general(km, q, _NT, precision=lax.Precision.HIGHEST, preferred_element_type=F32)
        bias_t = _topk_block_bias_t(sel_t, qi)
        qa_ref[rows, 0:HEAD_DIM] = (q * ATTN_SCALE).astype(BF16)
        qa_ref[rows, HEAD_DIM:] = jnp.concatenate([bias_t, ones_rows, zero_rows], axis=0).T.astype(BF16)

    for n in range(nb):
        r0 = n * MOBA_BLOCK
        s = lax.dot_general(qa_ref[r0:seq, :], ka_ref[r0:r0 + MOBA_BLOCK, :], _NT, preferred_element_type=F32)
        s_ref[n, r0:r0 + MOBA_BLOCK, :] = jnp.where(causal, s[0:MOBA_BLOCK], NEG)
        if r0 + MOBA_BLOCK < seq:
            s_ref[n, r0 + MOBA_BLOCK:seq, :] = s[MOBA_BLOCK:]

    inv_l = []
    for qi in range(nb):
        rows = slice(qi * MOBA_BLOCK, (qi + 1) * MOBA_BLOCK)
        m = jnp.max(s_ref[0, rows, :], axis=-1, keepdims=True)
        for n in range(1, qi + 1):
            m = jnp.maximum(m, jnp.max(s_ref[n, rows, :], axis=-1, keepdims=True))
        l = jnp.zeros((MOBA_BLOCK, 1), F32)
        for n in range(qi + 1):
            p = jnp.exp(s_ref[n, rows, :] - m)
            l = l + jnp.sum(p, axis=-1, keepdims=True)
            p_ref[n, rows, :] = p.astype(BF16)
        inv_l.append(1.0 / l)

    for n in range(nb):
        r0 = n * MOBA_BLOCK
        pv = jnp.dot(p_ref[n, r0:seq, :], vb_ref[r0:r0 + MOBA_BLOCK, :], preferred_element_type=F32)
        if n == 0:
            acc_ref[...] = pv
        else:
            acc_ref[r0:seq, :] += pv
    for qi in range(nb):
        rows = slice(qi * MOBA_BLOCK, (qi + 1) * MOBA_BLOCK)
        o_ref[rows, :] = (acc_ref[rows, :] * inv_l[qi]).astype(o_ref.dtype)


def _attn_prompt(qh, kbh, vbh, km, kx, batch, seq):
    nb = seq // MOBA_BLOCK
    assert nb == 8, "bias rows fill one sublane tile"
    slab = pl.BlockSpec((None, seq, HEAD_DIM), lambda b, h: (h, b, 0))
    return pl.pallas_call(
        functools.partial(_attn_prompt_kernel, nb=nb),
        grid=(batch, N_HEADS),
        in_specs=[
            slab, slab, slab,
            pl.BlockSpec((None, nb, HEAD_DIM), lambda b, h: (b, 0, h)),
            pl.BlockSpec((None, seq, HEAD_DIM), lambda b, h: (h, 0, 0)),
        ],
        out_specs=slab,
        out_shape=jax.ShapeDtypeStruct((N_HEADS, batch * seq, HEAD_DIM), BF16),
        scratch_shapes=[
            pltpu.VMEM((seq, 2 * HEAD_DIM), BF16),
            pltpu.VMEM((seq, 2 * HEAD_DIM), BF16),
            pltpu.VMEM((nb, seq, MOBA_BLOCK), F32),
            pltpu.VMEM((nb, seq, MOBA_BLOCK), BF16),
            pltpu.VMEM((seq, HEAD_DIM), F32),
        ],
        compiler_params=_params("arbitrary", "arbitrary"),
        name="attn_prompt",
    )(qh, kbh, vbh, km, kx)


def _pool_diff(ext_ref, r0, n, pos, d_ref, d0):
    for g, w in enumerate(POOL_WINDOWS):
        ls = slice(g * POOL_GROUP_W, (g + 1) * POOL_GROUP_W)
        cur = ext_ref[pl.ds(r0, n), ls]
        win = cur
        for j in range(1, w):
            win = win + ext_ref[pl.ds(r0 - j, n), ls]
        cnt = jnp.minimum(pos + 1, w).astype(F32)
        d_ref[pl.ds(d0, n), ls] = win / cnt - cur


def _mix_out(a_bf, d_ref, wp_ref, ps_ref, wo_ref):
    out = jnp.dot(a_bf, wo_ref[0:ATTN_WIDTH, :], preferred_element_type=F32)
    ys = []
    for g in range(len(POOL_WINDOWS)):
        ls = slice(g * POOL_GROUP_W, (g + 1) * POOL_GROUP_W)
        y = jnp.dot(d_ref[:, ls].astype(BF16), wp_ref[g], preferred_element_type=F32)
        ys.append((y * ps_ref[:, ls]).astype(BF16))
    return out + jnp.dot(jnp.concatenate(ys, axis=-1), wo_ref[ATTN_WIDTH:, :], preferred_element_type=F32)


def _mid_prompt_kernel(a_ref, u_ref, uprev_ref, x_ref, wp_ref, ps_ref, wo_ref, g2_ref,
                       x1_ref, h2_ref, hist_ref, ext_ref, d_ref, *, tm, tiles_per_seq):
    t_in_seq = lax.rem(pl.program_id(0), tiles_per_seq)
    ext_ref[0:HIST_PAD, :] = jnp.where(t_in_seq == 0, 0.0, uprev_ref[...])
    ext_ref[HIST_PAD:HIST_PAD + tm, :] = u_ref[...]
    pos = t_in_seq * tm + lax.broadcasted_iota(jnp.int32, (tm, 1), 0)
    _pool_diff(ext_ref, HIST_PAD, tm, pos, d_ref, 0)
    a = jnp.concatenate([a_ref[hd] for hd in range(N_HEADS)], axis=-1)
    x1 = x_ref[...] + _mix_out(a, d_ref, wp_ref, ps_ref, wo_ref)
    x1_ref[...] = x1
    h2_ref[...] = _rmsnorm(x1, g2_ref[...]).astype(BF16)

    @pl.when(t_in_seq == tiles_per_seq - 1)
    def _():
        hist_ref[...] = ext_ref[pl.ds(HIST_PAD + tm - POOL_HIST, POOL_HIST), :]


def _mid_prompt(a, u, x, wp_bf, ps, wo_bf, g2, batch, seq, tm=256):
    m = x.shape[0]
    tiles_per_seq = seq // tm
    row = lambda i: (i, 0)
    const2 = lambda i: (0, 0)
    return pl.pallas_call(
        functools.partial(_mid_prompt_kernel, tm=tm, tiles_per_seq=tiles_per_seq),
        grid=(m // tm,),
        in_specs=[
            pl.BlockSpec((N_HEADS, tm, HEAD_DIM), lambda i: (0, i, 0)),
            pl.BlockSpec((tm, POOL_WIDTH), row),
            pl.BlockSpec((HIST_PAD, POOL_WIDTH), lambda i: (jnp.maximum(i * (tm // HIST_PAD) - 1, 0), 0)),
            pl.BlockSpec((tm, D_MODEL), row),
            pl.BlockSpec(wp_bf.shape, lambda i: (0, 0, 0), pipeline_mode=pl.Buffered(1)),
            pl.BlockSpec((1, POOL_WIDTH), const2),
            pl.BlockSpec((D_MODEL, D_MODEL), const2, pipeline_mode=pl.Buffered(1)),
            pl.BlockSpec((1, D_MODEL), const2),
        ],
        out_specs=[
            pl.BlockSpec((tm, D_MODEL), row),
            pl.BlockSpec((tm, D_MODEL), row),
            pl.BlockSpec((None, POOL_HIST, POOL_WIDTH), lambda i: (i // tiles_per_seq, 0, 0)),
        ],
        out_shape=[
            jax.ShapeDtypeStruct((m, D_MODEL), F32),
            jax.ShapeDtypeStruct((m, D_MODEL), BF16),
            jax.ShapeDtypeStruct((batch, POOL_HIST, POOL_WIDTH), F32),
        ],
        scratch_shapes=[pltpu.VMEM((HIST_PAD + tm, POOL_WIDTH), F32), pltpu.VMEM((tm, POOL_WIDTH), F32)],
        compiler_params=_params("arbitrary"),
        name="mid_prompt",
    )(a, u, u, x, wp_bf, ps, wo_bf, g2)


def _mid_sample_kernel(a_ref, u_ref, hist_ref, x_ref, wp_ref, ps_ref, wo_ref, g2_ref,
                       x1_ref, h2_ref, newhist_ref, ext_ref, d_ref, *, n_seq, t_new, start_pos):
    stride = POOL_HIST + t_new
    stride += -stride % 8
    pos = start_pos + lax.broadcasted_iota(jnp.int32, (t_new, 1), 0)
    for b in range(n_seq):
        base = b * stride
        ext_ref[base:base + POOL_HIST, :] = hist_ref[b]
        ext_ref[base + POOL_HIST:base + POOL_HIST + t_new, :] = u_ref[b * t_new:(b + 1) * t_new, :]
        _pool_diff(ext_ref, base + POOL_HIST, t_new, pos, d_ref, b * t_new)
        newhist_ref[b] = ext_ref[pl.ds(base + t_new, POOL_HIST), :]
    x1 = x_ref[...] + _mix_out(a_ref[...].astype(BF16), d_ref, wp_ref, ps_ref, wo_ref)
    x1_ref[...] = x1
    h2_ref[...] = _rmsnorm(x1, g2_ref[...]).astype(BF16)


def _mid_sample(a, u, hist, x, wp_bf, ps, wo_bf, g2, n_seq, t_new, start_pos):
    m = x.shape[0]
    stride = POOL_HIST + t_new
    stride += -stride % 8
    return pl.pallas_call(
        functools.partial(_mid_sample_kernel, n_seq=n_seq, t_new=t_new, start_pos=start_pos),
        out_shape=[
            jax.ShapeDtypeStruct((m, D_MODEL), F32),
            jax.ShapeDtypeStruct((m, D_MODEL), BF16),
            jax.ShapeDtypeStruct((n_seq, POOL_HIST, POOL_WIDTH), F32),
        ],
        scratch_shapes=[pltpu.VMEM((n_seq * stride, POOL_WIDTH), F32), pltpu.VMEM((m, POOL_WIDTH), F32)],
        compiler_params=pltpu.CompilerParams(vmem_limit_bytes=VMEM_LIMIT_BYTES),
        name="mid_sample",
    )(a, u, hist, x, wp_bf, ps, wo_bf, g2)


def _ffn_kernel(h2_ref, wg_ref, wu_ref, wd_ref, x1_ref, gf_ref, y_ref):
    j = pl.program_id(1)

    @pl.when(j == 0)
    def _():
        y_ref[...] = x1_ref[...]

    h2 = h2_ref[...]
    gate = jnp.dot(h2, wg_ref[...], preferred_element_type=F32)
    up = jnp.dot(h2, wu_ref[...], preferred_element_type=F32)
    act = (gate * jax.nn.sigmoid(gate) * up).astype(BF16)
    y_ref[...] += jnp.dot(act, wd_ref[...], preferred_element_type=F32)

    @pl.when(j == pl.num_programs(1) - 1)
    def _():
        y_ref[...] = _rmsnorm(y_ref[...], gf_ref[...])


def _ffn(h2, wg_bf, wu_bf, wd_bf, x1, gf, tm, tf=512):
    m = h2.shape[0]
    d_ff = wg_bf.shape[1]
    row = lambda i, j: (i, 0)
    return pl.pallas_call(
        _ffn_kernel,
        grid=(m // tm, d_ff // tf),
        in_specs=[
            pl.BlockSpec((tm, D_MODEL), row),
            pl.BlockSpec((D_MODEL, tf), lambda i, j: (0, j)),
            pl.BlockSpec((D_MODEL, tf), lambda i, j: (0, j)),
            pl.BlockSpec((tf, D_MODEL), lambda i, j: (j, 0)),
            pl.BlockSpec((tm, D_MODEL), row),
            pl.BlockSpec((1, D_MODEL), lambda i, j: (0, 0)),
        ],
        out_specs=pl.BlockSpec((tm, D_MODEL), row),
        out_shape=jax.ShapeDtypeStruct((m, D_MODEL), F32),
        compiler_params=_params("arbitrary", "arbitrary"),
        name="ffn",
    )(h2, wg_bf, wu_bf, wd_bf, x1, gf)


def _kmean_sample_kernel(pt_ref, *refs, pages_per_step):
    page_refs, o_ref = refs[:pages_per_step], refs[pages_per_step]
    for jb in range(pages_per_step // PAGES_PER_BLOCK):
        tot = jnp.sum(page_refs[PAGES_PER_BLOCK * jb][...], axis=0)
        for r in range(1, PAGES_PER_BLOCK):
            tot = tot + jnp.sum(page_refs[PAGES_PER_BLOCK * jb + r][...], axis=0)
        mean = tot * (1.0 / MOBA_BLOCK)
        for hd in range(N_HEADS):
            o_ref[jb:jb + 1, hd * HEAD_DIM:(hd + 1) * HEAD_DIM] = mean[hd:hd + 1, :]


def _kmean_sample(page_table_flat, cache_k, n_seq, n_pages, pages_per_step=16):
    steps = n_pages // pages_per_step
    blocks_per_step = pages_per_step // PAGES_PER_BLOCK

    def page_spec(r):
        return pl.BlockSpec(
            (None, None, PAGE_SIZE, N_HEADS, HEAD_DIM),
            lambda b, s, pt: (0, pt[b * n_pages + s * pages_per_step + r], 0, 0, 0))

    grid_spec = pltpu.PrefetchScalarGridSpec(
        num_scalar_prefetch=1,
        grid=(n_seq, steps),
        in_specs=[page_spec(r) for r in range(pages_per_step)],
        out_specs=pl.BlockSpec((None, blocks_per_step, ATTN_WIDTH), lambda b, s, pt: (b, s, 0)),
    )
    return pl.pallas_call(
        functools.partial(_kmean_sample_kernel, pages_per_step=pages_per_step),
        grid_spec=grid_spec,
        out_shape=jax.ShapeDtypeStruct((n_seq, n_pages // PAGES_PER_BLOCK, ATTN_WIDTH), F32),
        compiler_params=_params("arbitrary", "arbitrary"),
        name="kmean_sample",
    )(page_table_flat, *([cache_k] * pages_per_step))


def _select_sample_kernel(qh_ref, km_ref, idx_ref, sc_ref, *, n_seq, t_new, n_past_blocks):
    lanes = sc_ref.shape[1]
    lane = lax.broadcasted_iota(jnp.int32, sc_ref.shape, 1)
    sc_ref[...] = jnp.where(lane == n_past_blocks, NEG, -jnp.inf)
    for b in range(n_seq):
        for hd in range(N_HEADS):
            qbh = qh_ref[hd, b * t_new:(b + 1) * t_new, :]
            kmbh = km_ref[b, :, hd * HEAD_DIM:(hd + 1) * HEAD_DIM]
            r0 = (b * N_HEADS + hd) * t_new
            sc_ref[r0:r0 + t_new, 0:n_past_blocks] = lax.dot_general(
                qbh, kmbh, _NT, precision=lax.Precision.HIGHEST, preferred_element_type=F32)
    s = sc_ref[...]
    for r in range(MOBA_TOPK):
        mx = jnp.max(s, axis=-1, keepdims=True)
        am = jnp.min(jnp.where(s == mx, lane, lanes), axis=-1, keepdims=True)
        idx_ref[:, r:r + 1] = am
        s = jnp.where(lane == am, -jnp.inf, s)


def _select_sample(qh, km, n_seq, t_new):
    n_past_blocks = km.shape[1]
    rows = n_seq * N_HEADS * t_new
    return pl.pallas_call(
        functools.partial(_select_sample_kernel, n_seq=n_seq, t_new=t_new, n_past_blocks=n_past_blocks),
        out_shape=jax.ShapeDtypeStruct((rows, MOBA_TOPK), jnp.int32),
        scratch_shapes=[pltpu.VMEM((rows, 128), F32)],
        compiler_params=pltpu.CompilerParams(vmem_limit_bytes=VMEM_LIMIT_BYTES),
        name="select_sample",
    )(qh, km)


def _attn_sample_kernel(pt_ref, idx_ref, slopes_ref, qh_ref, kh_ref, vh_ref, ck_hbm, cv_hbm, o_ref,
                        kbuf, vbuf, sem, *, n_seq, t_new, n_pages):
    n_past_blocks = n_pages // PAGES_PER_BLOCK
    past_len = n_pages * PAGE_SIZE
    n_slots = t_new * MOBA_TOPK
    b = pl.program_id(0)
    hd = pl.program_id(1)
    step = b * N_HEADS + hd
    buf = lax.rem(step, 2)

    def block_of(bb, hh, qj):
        return idx_ref[(bb * N_HEADS + hh) * n_slots + qj]

    def gather(bb, hh, sl, start):
        for qj in range(n_slots):
            blk = jnp.minimum(block_of(bb, hh, qj), n_past_blocks - 1)
            for half in range(PAGES_PER_BLOCK):
                phys = pt_ref[bb * n_pages + blk * PAGES_PER_BLOCK + half]
                rows = pl.ds(half * PAGE_SIZE, PAGE_SIZE)
                for src, dst, s in ((ck_hbm, kbuf, 0), (cv_hbm, vbuf, 1)):
                    cp = pltpu.make_async_copy(src.at[0, phys, :, hh, :], dst.at[sl, qj, rows, :], sem.at[s, sl])
                    if start:
                        cp.start()
                    else:
                        cp.wait()

    @pl.when(step == 0)
    def _():
        gather(b, hd, buf, True)

    @pl.when(step + 1 < n_seq * N_HEADS)
    def _():
        nxt = step + 1
        gather(nxt // N_HEADS, lax.rem(nxt, N_HEADS), 1 - buf, True)

    gather(b, hd, buf, False)

    slope = slopes_ref[hd]
    rows = pl.ds(b * t_new, t_new)
    qs = (qh_ref[rows, :] * ATTN_SCALE).astype(BF16)
    row = lax.broadcasted_iota(jnp.int32, (t_new, MOBA_BLOCK), 0)
    col = lax.broadcasted_iota(jnp.int32, (t_new, MOBA_BLOCK), 1)

    pieces = []
    for qj in range(n_slots):
        blk = block_of(b, hd, qj)
        s = lax.dot_general(qs, kbuf[buf, qj].astype(BF16), _NT, preferred_element_type=F32)
        dist = (past_len + row - (blk * MOBA_BLOCK + col)).astype(F32)
        keep = jnp.logical_and(row == qj // MOBA_TOPK, blk < n_past_blocks)
        pieces.append(jnp.where(keep, s - slope * dist, NEG))
    own_rel = (lax.broadcasted_iota(jnp.int32, (t_new, t_new), 0)
               - lax.broadcasted_iota(jnp.int32, (t_new, t_new), 1))
    s_own = lax.dot_general(qs, kh_ref[rows, :].astype(BF16), _NT, preferred_element_type=F32)
    s_own = jnp.where(own_rel >= 0, s_own - slope * own_rel.astype(F32), NEG)

    m = jnp.max(s_own, axis=-1, keepdims=True)
    for s in pieces:
        m = jnp.maximum(m, jnp.max(s, axis=-1, keepdims=True))
    p_own = jnp.exp(s_own - m)
    l = jnp.sum(p_own, axis=-1, keepdims=True)
    acc = jnp.dot(p_own.astype(BF16), vh_ref[rows, :].astype(BF16), preferred_element_type=F32)
    for qj, s in enumerate(pieces):
        p = jnp.exp(s - m)
        l = l + jnp.sum(p, axis=-1, keepdims=True)
        acc = acc + jnp.dot(p.astype(BF16), vbuf[buf, qj].astype(BF16), preferred_element_type=F32)
    o_ref[...] = acc / l


def _attn_sample(page_table_flat, idx_flat, slopes, qh, kh, vh, cache_k, cache_v, n_seq, t_new, n_pages):
    n_slots = t_new * MOBA_TOPK
    head_blk = pl.BlockSpec((None, n_seq * t_new, HEAD_DIM), lambda b, h, pt, ix, sl: (h, 0, 0))
    grid_spec = pltpu.PrefetchScalarGridSpec(
        num_scalar_prefetch=3,
        grid=(n_seq, N_HEADS),
        in_specs=[head_blk, head_blk, head_blk,
                  pl.BlockSpec(memory_space=pl.ANY), pl.BlockSpec(memory_space=pl.ANY)],
        out_specs=pl.BlockSpec((None, t_new, HEAD_DIM), lambda b, h, pt, ix, sl: (b, 0, h)),
        scratch_shapes=[
            pltpu.VMEM((2, n_slots, MOBA_BLOCK, HEAD_DIM), F32),
            pltpu.VMEM((2, n_slots, MOBA_BLOCK, HEAD_DIM), F32),
            pltpu.SemaphoreType.DMA((2, 2)),
        ],
    )
    return pl.pallas_call(
        functools.partial(_attn_sample_kernel, n_seq=n_seq, t_new=t_new, n_pages=n_pages),
        grid_spec=grid_spec,
        out_shape=jax.ShapeDtypeStruct((n_seq, t_new, ATTN_WIDTH), F32),
        compiler_params=_params("arbitrary", "arbitrary"),
        name="attn_sample",
    )(page_table_flat, idx_flat, slopes, qh, kh, vh, cache_k, cache_v)


def kernel(x_prompt, x_sample, cache_k, cache_v, state_pool, page_table, norm1_g, w_in, w_pool, pool_scale,
           w_out, norm2_g, w_gate, w_up, w_down, norm_f_g):
    depth = w_in.shape[0]
    assert depth == 1, "single-layer step"
    batch, seq, _ = x_prompt.shape
    n_seq, t_new, _ = x_sample.shape
    n_pages = page_table.shape[1]
    past_len = n_pages * PAGE_SIZE
    assert seq % MOBA_BLOCK == 0 and past_len % MOBA_BLOCK == 0 and t_new <= MOBA_BLOCK

    slopes = jnp.exp2(-8.0 * jnp.arange(1, N_HEADS + 1, dtype=F32) / N_HEADS)
    g1 = norm1_g[0].reshape(1, D_MODEL)
    g2 = norm2_g[0].reshape(1, D_MODEL)
    gf = norm_f_g.reshape(1, D_MODEL)
    ps = pool_scale[0].reshape(1, POOL_WIDTH)
    w_in_bf = w_in[0].astype(BF16)
    w_pool_bf = w_pool[0].astype(BF16)
    w_out_bf = w_out[0].astype(BF16)
    w_gate_bf = w_gate[0].astype(BF16)
    w_up_bf = w_up[0].astype(BF16)
    w_down_bf = w_down[0].astype(BF16)

    xp = x_prompt.reshape(batch * seq, D_MODEL)
    k, v, u, qh_p, kbh, vbh, km = _proj_prompt(xp, g1, w_in_bf)
    km = km.reshape(batch, seq // MOBA_BLOCK, ATTN_WIDTH)
    a = _attn_prompt(qh_p, kbh, vbh, km, _alibi_key_table(slopes, seq), batch, seq)
    x1, h2, hist_p = _mid_prompt(a, u, xp, w_pool_bf, ps, w_out_bf, g2, batch, seq)
    y_p = _ffn(h2, w_gate_bf, w_up_bf, w_down_bf, x1, gf, tm=512)

    xs = x_sample.reshape(n_seq * t_new, D_MODEL)
    pt_flat = page_table.reshape(-1)
    k_s, v_s, u_s, qh, kh, vh = _proj_sample(xs, g1, w_in_bf)
    km_s = _kmean_sample(pt_flat, cache_k, n_seq, n_pages)
    idx = _select_sample(qh, km_s, n_seq, t_new)
    a_s = _attn_sample(pt_flat, idx.reshape(-1), slopes, qh, kh, vh, cache_k, cache_v, n_seq, t_new, n_pages)
    x1_s, h2_s, hist_s = _mid_sample(a_s.reshape(n_seq * t_new, ATTN_WIDTH), u_s, state_pool[0], xs,
                                     w_pool_bf, ps, w_out_bf, g2, n_seq, t_new, past_len)
    y_s = _ffn(h2_s, w_gate_bf, w_up_bf, w_down_bf, x1_s, gf, tm=n_seq * t_new)

    kv_p = (depth, batch, seq, N_HEADS, HEAD_DIM)
    kv_s = (depth, n_seq, t_new, N_HEADS, HEAD_DIM)
    return (y_p.reshape(batch, seq, D_MODEL), y_s.reshape(n_seq, t_new, D_MODEL),
            k.reshape(kv_p), v.reshape(kv_p), hist_p[None],
            k_s.reshape(kv_s), v_s.reshape(kv_s), hist_s[None])
```

```python
import functools

import jax
import jax.numpy as jnp
from jax import lax
from jax.experimental import pallas as pl
from jax.experimental.pallas import tpu as pltpu

D_MODEL = 2048
HEAD_DIM = 128
N_HEADS = 8
ATTN_WIDTH = N_HEADS * HEAD_DIM
POOL_WIDTH = D_MODEL - ATTN_WIDTH
IN_WIDTH = 3 * ATTN_WIDTH + POOL_WIDTH
MOBA_BLOCK = 256
MOBA_TOPK = 3
PAGE_SIZE = 128
PAGES_PER_BLOCK = MOBA_BLOCK // PAGE_SIZE
POOL_WINDOWS = (2, 4, 8, 16)
POOL_GROUP_W = POOL_WIDTH // len(POOL_WINDOWS)
POOL_HIST = max(POOL_WINDOWS) - 1
HIST_PAD = 16
RMS_EPS = 1e-6
NEG = -1e30
ATTN_SCALE = HEAD_DIM ** -0.5

V7X_VMEM_BYTES = 64 * 1024 * 1024
VMEM_LIMIT_BYTES = V7X_VMEM_BYTES - 8 * 1024 * 1024

F32 = jnp.float32
BF16 = jnp.bfloat16
_NT = (((1,), (1,)), ((), ()))


def _rmsnorm(x, g):
    y = x * lax.rsqrt(jnp.mean(x * x, axis=-1, keepdims=True) + RMS_EPS)
    return y * g


def _params(*semantics):
    return pltpu.CompilerParams(dimension_semantics=semantics, vmem_limit_bytes=VMEM_LIMIT_BYTES)


def _proj_prompt_kernel(x_ref, g_ref, w_ref, k_ref, v_ref, u_ref, qh_ref, kbh_ref, vbh_ref, km_ref, *, tm):
    h = _rmsnorm(x_ref[...], g_ref[...]).astype(BF16)

    def seg(s):
        return jnp.dot(h, w_ref[:, s * ATTN_WIDTH:(s + 1) * ATTN_WIDTH], preferred_element_type=F32)

    zq = seg(0)
    for hd in range(N_HEADS):
        qh_ref[hd] = zq[:, hd * HEAD_DIM:(hd + 1) * HEAD_DIM]
    zk = seg(1)
    k_ref[...] = zk
    for hd in range(N_HEADS):
        kbh_ref[hd] = zk[:, hd * HEAD_DIM:(hd + 1) * HEAD_DIM].astype(BF16)
    for blk in range(tm // MOBA_BLOCK):
        rows = zk[blk * MOBA_BLOCK:(blk + 1) * MOBA_BLOCK]
        km_ref[blk] = jnp.sum(rows, axis=0, keepdims=True) * (1.0 / MOBA_BLOCK)
    zv = seg(2)
    v_ref[...] = zv
    for hd in range(N_HEADS):
        vbh_ref[hd] = zv[:, hd * HEAD_DIM:(hd + 1) * HEAD_DIM].astype(BF16)
    u_ref[...] = seg(3)


def _proj_prompt(x, g, w_bf, tm=256):
    m = x.shape[0]
    row = lambda i: (i, 0)
    f32_out = jax.ShapeDtypeStruct((m, ATTN_WIDTH), F32)
    km_out = jax.ShapeDtypeStruct((m // MOBA_BLOCK, 1, ATTN_WIDTH), F32)
    blk = pl.BlockSpec((tm, ATTN_WIDTH), row)
    hm_blk = pl.BlockSpec((N_HEADS, tm, HEAD_DIM), lambda i: (0, i, 0))
    return pl.pallas_call(
        functools.partial(_proj_prompt_kernel, tm=tm),
        grid=(m // tm,),
        in_specs=[
            pl.BlockSpec((tm, D_MODEL), row),
            pl.BlockSpec((1, D_MODEL), lambda i: (0, 0)),
            pl.BlockSpec((D_MODEL, IN_WIDTH), lambda i: (0, 0), pipeline_mode=pl.Buffered(1)),
        ],
        out_specs=[blk, blk, blk, hm_blk, hm_blk, hm_blk,
                   pl.BlockSpec((tm // MOBA_BLOCK, 1, ATTN_WIDTH), lambda i: (i, 0, 0))],
        out_shape=[f32_out, f32_out, f32_out,
                   jax.ShapeDtypeStruct((N_HEADS, m, HEAD_DIM), F32),
                   jax.ShapeDtypeStruct((N_HEADS, m, HEAD_DIM), BF16),
                   jax.ShapeDtypeStruct((N_HEADS, m, HEAD_DIM), BF16), km_out],
        compiler_params=_params("arbitrary"),
        name="proj_prompt",
    )(x, g, w_bf)


def _proj_sample_kernel(x_ref, g_ref, w_ref, k_ref, v_ref, u_ref, qh_ref, kh_ref, vh_ref):
    h = _rmsnorm(x_ref[...], g_ref[...]).astype(BF16)
    zs = [jnp.dot(h, w_ref[:, s * ATTN_WIDTH:(s + 1) * ATTN_WIDTH], preferred_element_type=F32)
          for s in range(4)]
    k_ref[...] = zs[1]
    v_ref[...] = zs[2]
    u_ref[...] = zs[3]
    for hd in range(N_HEADS):
        ls = slice(hd * HEAD_DIM, (hd + 1) * HEAD_DIM)
        qh_ref[hd] = zs[0][:, ls]
        kh_ref[hd] = zs[1][:, ls]
        vh_ref[hd] = zs[2][:, ls]


def _proj_sample(x, g, w_bf):
    m = x.shape[0]
    f32_out = jax.ShapeDtypeStruct((m, ATTN_WIDTH), F32)
    hm_out = jax.ShapeDtypeStruct((N_HEADS, m, HEAD_DIM), F32)
    return pl.pallas_call(
        _proj_sample_kernel,
        out_shape=[f32_out, f32_out, f32_out, hm_out, hm_out, hm_out],
        compiler_params=pltpu.CompilerParams(vmem_limit_bytes=VMEM_LIMIT_BYTES),
        name="proj_sample",
    )(x, g, w_bf)


def _alibi_key_table(slopes, seq):
    nb = seq // MOBA_BLOCK
    t = jnp.arange(seq, dtype=jnp.int32)
    blk, local = t // MOBA_BLOCK, t % MOBA_BLOCK
    onehot = (blk[:, None] == jnp.arange(nb, dtype=jnp.int32)[None, :]).astype(F32)
    onehot = jnp.broadcast_to(onehot[None], (N_HEADS, seq, nb))
    pos = jnp.stack([local, blk * MOBA_BLOCK], axis=-1).astype(F32)[None] * slopes[:, None, None]
    pad = jnp.zeros((N_HEADS, seq, HEAD_DIM - nb - 2), F32)
    return jnp.concatenate([onehot, pos, pad], axis=-1).astype(BF16)


def _topk_block_bias_t(scores_t, n_past):
    nb = scores_t.shape[0]
    n_iota = lax.broadcasted_iota(jnp.int32, scores_t.shape, 0)
    past = n_iota < n_past
    s = jnp.where(past, scores_t, NEG)
    rank = jnp.zeros(scores_t.shape, jnp.int32)
    for m in range(nb):
        sm = s[m:m + 1, :]
        tie_lower = jnp.logical_and(sm == s, n_iota > m)
        rank = rank + jnp.where(sm > s, 1, jnp.where(tie_lower, 1, 0))
    keep = jnp.logical_or(jnp.logical_and(past, rank < MOBA_TOPK), n_iota == n_past)
    return jnp.where(keep, 0.0, NEG)


def _attn_prompt_kernel(q_ref, kb_ref, vb_ref, km_ref, kx_ref, *refs, nb, n_cast):
    w_refs, o_ref, wbf_refs = refs[:n_cast], refs[n_cast], refs[n_cast + 1:2 * n_cast + 1]
    qa_ref, ka_ref, s_ref, p_ref, acc_ref = refs[2 * n_cast + 1:]
    for w_ref, wbf_ref in zip(w_refs, wbf_refs):
        wbf_ref[...] = w_ref[...].astype(BF16)
    seq = nb * MOBA_BLOCK
    km = km_ref[...]
    shape = (MOBA_BLOCK, MOBA_BLOCK)
    causal = lax.broadcasted_iota(jnp.int32, shape, 0) >= lax.broadcasted_iota(jnp.int32, shape, 1)

    ka_ref[:, 0:HEAD_DIM] = kb_ref[...]
    ka_ref[:, HEAD_DIM:] = kx_ref[...]
    ones_rows = jnp.where(lax.broadcasted_iota(jnp.int32, (8, MOBA_BLOCK), 0) < 2, 1.0, 0.0)
    zero_rows = jnp.zeros((HEAD_DIM - nb - 8, MOBA_BLOCK), F32)
    for qi in range(nb):
        rows = slice(qi * MOBA_BLOCK, (qi + 1) * MOBA_BLOCK)
        q = q_ref[rows, :]
        sel_t = lax.dot_general(km, q, _NT, precision=lax.Precision.HIGHEST, preferred_element_type=F32)
        bias_t = _topk_block_bias_t(sel_t, qi)
        qa_ref[rows, 0:HEAD_DIM] = (q * ATTN_SCALE).astype(BF16)
        qa_ref[rows, HEAD_DIM:] = jnp.concatenate([bias_t, ones_rows, zero_rows], axis=0).T.astype(BF16)

    for n in range(nb):
        r0 = n * MOBA_BLOCK
        s = lax.dot_general(qa_ref[r0:seq, :], ka_ref[r0:r0 + MOBA_BLOCK, :], _NT, preferred_element_type=F32)
        s_ref[n, r0:r0 + MOBA_BLOCK, :] = jnp.where(causal, s[0:MOBA_BLOCK], NEG)
        if r0 + MOBA_BLOCK < seq:
            s_ref[n, r0 + MOBA_BLOCK:seq, :] = s[MOBA_BLOCK:]

    inv_l = []
    for qi in range(nb):
        rows = slice(qi * MOBA_BLOCK, (qi + 1) * MOBA_BLOCK)
        m = jnp.max(s_ref[0, rows, :], axis=-1, keepdims=True)
        for n in range(1, qi + 1):
            m = jnp.maximum(m, jnp.max(s_ref[n, rows, :], axis=-1, keepdims=True))
        l = jnp.zeros((MOBA_BLOCK, 1), F32)
        for n in range(qi + 1):
            p = jnp.exp(s_ref[n, rows, :] - m)
            l = l + jnp.sum(p, axis=-1, keepdims=True)
            p_ref[n, rows, :] = p.astype(BF16)
        inv_l.append(1.0 / l)

    for n in range(nb):
        r0 = n * MOBA_BLOCK
        pv = jnp.dot(p_ref[n, r0:seq, :], vb_ref[r0:r0 + MOBA_BLOCK, :], preferred_element_type=F32)
        if n == 0:
            acc_ref[...] = pv
        else:
            acc_ref[r0:seq, :] += pv
    for qi in range(nb):
        rows = slice(qi * MOBA_BLOCK, (qi + 1) * MOBA_BLOCK)
        o_ref[rows, :] = (acc_ref[rows, :] * inv_l[qi]).astype(o_ref.dtype)


def _attn_prompt(qh, kbh, vbh, km, kx, batch, seq, cast_weights):
    nb = seq // MOBA_BLOCK
    assert nb == 8, "bias rows fill one sublane tile"
    steps = batch * N_HEADS
    slab = pl.BlockSpec((None, seq, HEAD_DIM), lambda b, h: (h, b, 0))
    chunk_specs = []
    for w in cast_weights:
        assert w.shape[0] % (16 * steps) == 0, "bf16 row chunks must be whole (16, 128) tiles"
        chunk_specs.append(pl.BlockSpec((w.shape[0] // steps, w.shape[1]), lambda b, h: (b * N_HEADS + h, 0)))
    res = pl.pallas_call(
        functools.partial(_attn_prompt_kernel, nb=nb, n_cast=len(cast_weights)),
        grid=(batch, N_HEADS),
        in_specs=[
            slab, slab, slab,
            pl.BlockSpec((None, nb, HEAD_DIM), lambda b, h: (b, 0, h)),
            pl.BlockSpec((None, seq, HEAD_DIM), lambda b, h: (h, 0, 0)),
        ] + chunk_specs,
        out_specs=[slab] + chunk_specs,
        out_shape=[jax.ShapeDtypeStruct((N_HEADS, batch * seq, HEAD_DIM), BF16)]
        + [jax.ShapeDtypeStruct(w.shape, BF16) for w in cast_weights],
        scratch_shapes=[
            pltpu.VMEM((seq, 2 * HEAD_DIM), BF16),
            pltpu.VMEM((seq, 2 * HEAD_DIM), BF16),
            pltpu.VMEM((nb, seq, MOBA_BLOCK), F32),
            pltpu.VMEM((nb, seq, MOBA_BLOCK), BF16),
            pltpu.VMEM((seq, HEAD_DIM), F32),
        ],
        compiler_params=_params("arbitrary", "arbitrary"),
        name="attn_prompt",
    )(qh, kbh, vbh, km, kx, *cast_weights)
    return res[0], res[1:]


def _pool_diff(ext_ref, r0, n, pos, d_ref, d0):
    for g, w in enumerate(POOL_WINDOWS):
        ls = slice(g * POOL_GROUP_W, (g + 1) * POOL_GROUP_W)
        cur = ext_ref[pl.ds(r0, n), ls]
        win = cur
        for j in range(1, w):
            win = win + ext_ref[pl.ds(r0 - j, n), ls]
        cnt = jnp.minimum(pos + 1, w).astype(F32)
        d_ref[pl.ds(d0, n), ls] = win / cnt - cur


def _mix_out(a_bf, d_ref, wp_ref, ps_ref, wo_ref):
    out = jnp.dot(a_bf, wo_ref[0:ATTN_WIDTH, :], preferred_element_type=F32)
    ys = []
    for g in range(len(POOL_WINDOWS)):
        ls = slice(g * POOL_GROUP_W, (g + 1) * POOL_GROUP_W)
        y = jnp.dot(d_ref[:, ls].astype(BF16), wp_ref[g], preferred_element_type=F32)
        ys.append((y * ps_ref[:, ls]).astype(BF16))
    return out + jnp.dot(jnp.concatenate(ys, axis=-1), wo_ref[ATTN_WIDTH:, :], preferred_element_type=F32)


def _mid_prompt_kernel(a_ref, u_ref, uprev_ref, x_ref, wp_ref, ps_ref, wo_ref, g2_ref,
                       x1_ref, h2_ref, hist_ref, ext_ref, d_ref, *, tm, tiles_per_seq):
    t_in_seq = lax.rem(pl.program_id(0), tiles_per_seq)
    ext_ref[0:HIST_PAD, :] = jnp.where(t_in_seq == 0, 0.0, uprev_ref[...])
    ext_ref[HIST_PAD:HIST_PAD + tm, :] = u_ref[...]
    pos = t_in_seq * tm + lax.broadcasted_iota(jnp.int32, (tm, 1), 0)
    _pool_diff(ext_ref, HIST_PAD, tm, pos, d_ref, 0)
    a = jnp.concatenate([a_ref[hd] for hd in range(N_HEADS)], axis=-1)
    x1 = x_ref[...] + _mix_out(a, d_ref, wp_ref, ps_ref, wo_ref)
    x1_ref[...] = x1
    h2_ref[...] = _rmsnorm(x1, g2_ref[...]).astype(BF16)

    @pl.when(t_in_seq == tiles_per_seq - 1)
    def _():
        hist_ref[...] = ext_ref[pl.ds(HIST_PAD + tm - POOL_HIST, POOL_HIST), :]


def _mid_prompt(a, u, x, wp_bf, ps, wo_bf, g2, batch, seq, tm=256):
    m = x.shape[0]
    tiles_per_seq = seq // tm
    row = lambda i: (i, 0)
    const2 = lambda i: (0, 0)
    return pl.pallas_call(
        functools.partial(_mid_prompt_kernel, tm=tm, tiles_per_seq=tiles_per_seq),
        grid=(m // tm,),
        in_specs=[
            pl.BlockSpec((N_HEADS, tm, HEAD_DIM), lambda i: (0, i, 0)),
            pl.BlockSpec((tm, POOL_WIDTH), row),
            pl.BlockSpec((HIST_PAD, POOL_WIDTH), lambda i: (jnp.maximum(i * (tm // HIST_PAD) - 1, 0), 0)),
            pl.BlockSpec((tm, D_MODEL), row),
            pl.BlockSpec(wp_bf.shape, lambda i: (0, 0, 0), pipeline_mode=pl.Buffered(1)),
            pl.BlockSpec((1, POOL_WIDTH), const2),
            pl.BlockSpec((D_MODEL, D_MODEL), const2, pipeline_mode=pl.Buffered(1)),
            pl.BlockSpec((1, D_MODEL), const2),
        ],
        out_specs=[
            pl.BlockSpec((tm, D_MODEL), row),
            pl.BlockSpec((tm, D_MODEL), row),
            pl.BlockSpec((None, POOL_HIST, POOL_WIDTH), lambda i: (i // tiles_per_seq, 0, 0)),
        ],
        out_shape=[
            jax.ShapeDtypeStruct((m, D_MODEL), F32),
            jax.ShapeDtypeStruct((m, D_MODEL), BF16),
            jax.ShapeDtypeStruct((batch, POOL_HIST, POOL_WIDTH), F32),
        ],
        scratch_shapes=[pltpu.VMEM((HIST_PAD + tm, POOL_WIDTH), F32), pltpu.VMEM((tm, POOL_WIDTH), F32)],
        compiler_params=_params("arbitrary"),
        name="mid_prompt",
    )(a, u, u, x, wp_bf, ps, wo_bf, g2)


def _mid_sample_kernel(a_ref, u_ref, hist_ref, x_ref, wp_ref, ps_ref, wo_ref, g2_ref,
                       x1_ref, h2_ref, newhist_ref, ext_ref, d_ref, *, n_seq, t_new, start_pos):
    stride = POOL_HIST + t_new
    stride += -stride % 8
    pos = start_pos + lax.broadcasted_iota(jnp.int32, (t_new, 1), 0)
    for b in range(n_seq):
        base = b * stride
        ext_ref[base:base + POOL_HIST, :] = hist_ref[b]
        ext_ref[base + POOL_HIST:base + POOL_HIST + t_new, :] = u_ref[b * t_new:(b + 1) * t_new, :]
        _pool_diff(ext_ref, base + POOL_HIST, t_new, pos, d_ref, b * t_new)
        newhist_ref[b] = ext_ref[pl.ds(base + t_new, POOL_HIST), :]
    x1 = x_ref[...] + _mix_out(a_ref[...].astype(BF16), d_ref, wp_ref, ps_ref, wo_ref)
    x1_ref[...] = x1
    h2_ref[...] = _rmsnorm(x1, g2_ref[...]).astype(BF16)


def _mid_sample(a, u, hist, x, wp_bf, ps, wo_bf, g2, n_seq, t_new, start_pos):
    m = x.shape[0]
    stride = POOL_HIST + t_new
    stride += -stride % 8
    return pl.pallas_call(
        functools.partial(_mid_sample_kernel, n_seq=n_seq, t_new=t_new, start_pos=start_pos),
        out_shape=[
            jax.ShapeDtypeStruct((m, D_MODEL), F32),
            jax.ShapeDtypeStruct((m, D_MODEL), BF16),
            jax.ShapeDtypeStruct((n_seq, POOL_HIST, POOL_WIDTH), F32),
        ],
        scratch_shapes=[pltpu.VMEM((n_seq * stride, POOL_WIDTH), F32), pltpu.VMEM((m, POOL_WIDTH), F32)],
        compiler_params=pltpu.CompilerParams(vmem_limit_bytes=VMEM_LIMIT_BYTES),
        name="mid_sample",
    )(a, u, hist, x, wp_bf, ps, wo_bf, g2)


def _store_block_mean(o_ref, jb, tot):
    mean = tot * (1.0 / MOBA_BLOCK)
    for hd in range(N_HEADS):
        o_ref[jb:jb + 1, hd * HEAD_DIM:(hd + 1) * HEAD_DIM] = mean[hd:hd + 1, :]


def _ffn_kernel(*refs, n_pages):
    if n_pages:
        refs = refs[1:]
    h2_ref, wg_ref, wu_ref, wd_ref, x1_ref, gf_ref = refs[:6]
    page_refs = refs[6:6 + n_pages]
    y_ref = refs[6 + n_pages]
    km_ref = refs[7 + n_pages] if n_pages else None
    j = pl.program_id(1)

    @pl.when(j == 0)
    def _():
        y_ref[...] = x1_ref[...]

    n_slots = 6
    sums = {}

    def side_job(slot):
        for r in range(slot, n_pages, n_slots):
            sums[r] = jnp.sum(page_refs[r][...], axis=0)
            jb, pos = divmod(r, PAGES_PER_BLOCK)
            if pos == PAGES_PER_BLOCK - 1:
                tot = sums[jb * PAGES_PER_BLOCK]
                for rr in range(jb * PAGES_PER_BLOCK + 1, r + 1):
                    tot = tot + sums[rr]
                _store_block_mean(km_ref, jb, tot)

    h2 = h2_ref[...]
    tf = wg_ref.shape[1]
    gate, up = [], []
    for c, cols in enumerate((slice(0, tf // 2), slice(tf // 2, tf))):
        gate.append(jnp.dot(h2, wg_ref[:, cols], preferred_element_type=F32))
        side_job(c)
    for c, cols in enumerate((slice(0, tf // 2), slice(tf // 2, tf))):
        up.append(jnp.dot(h2, wu_ref[:, cols], preferred_element_type=F32))
        side_job(2 + c)
    gate = jnp.concatenate(gate, axis=-1)
    act = (gate * jax.nn.sigmoid(gate) * jnp.concatenate(up, axis=-1)).astype(BF16)
    for c, cols in enumerate((slice(0, D_MODEL // 2), slice(D_MODEL // 2, D_MODEL))):
        y_ref[:, cols] += jnp.dot(act, wd_ref[:, cols], preferred_element_type=F32)
        side_job(4 + c)

    @pl.when(j == pl.num_programs(1) - 1)
    def _():
        y_ref[...] = _rmsnorm(y_ref[...], gf_ref[...])


def _ffn(h2, wg_bf, wu_bf, wd_bf, x1, gf, tm, tf=512, paged_k=None):
    m = h2.shape[0]
    d_ff = wg_bf.shape[1]
    ni, nj = m // tm, d_ff // tf
    in_specs = [
        pl.BlockSpec((tm, D_MODEL), lambda i, j, *_: (i, 0)),
        pl.BlockSpec((D_MODEL, tf), lambda i, j, *_: (0, j)),
        pl.BlockSpec((D_MODEL, tf), lambda i, j, *_: (0, j)),
        pl.BlockSpec((tf, D_MODEL), lambda i, j, *_: (j, 0)),
        pl.BlockSpec((tm, D_MODEL), lambda i, j, *_: (i, 0)),
        pl.BlockSpec((1, D_MODEL), lambda i, j, *_: (0, 0)),
    ]
    y_spec = pl.BlockSpec((tm, D_MODEL), lambda i, j, *_: (i, 0))
    y_shape = jax.ShapeDtypeStruct((m, D_MODEL), F32)
    if paged_k is None:
        return pl.pallas_call(
            functools.partial(_ffn_kernel, n_pages=0),
            grid=(ni, nj), in_specs=in_specs, out_specs=y_spec, out_shape=y_shape,
            compiler_params=_params("arbitrary", "arbitrary"), name="ffn",
        )(h2, wg_bf, wu_bf, wd_bf, x1, gf)

    page_table_flat, cache_k = paged_k
    total = page_table_flat.shape[0]
    pps = PAGES_PER_BLOCK * pl.cdiv(total, PAGES_PER_BLOCK * ni * nj)
    n_events = pl.cdiv(total, pps)

    def page_spec(r):
        def index_map(i, j, pt):
            event = jnp.minimum(i * nj + j, n_events - 1)
            return (0, pt[jnp.minimum(event * pps + r, total - 1)], 0, 0, 0)
        return pl.BlockSpec((None, None, PAGE_SIZE, N_HEADS, HEAD_DIM), index_map)

    bps = pps // PAGES_PER_BLOCK
    km_spec = pl.BlockSpec((None, bps, ATTN_WIDTH), lambda i, j, pt: (jnp.minimum(i * nj + j, n_events - 1), 0, 0))
    y, km = pl.pallas_call(
        functools.partial(_ffn_kernel, n_pages=pps),
        grid_spec=pltpu.PrefetchScalarGridSpec(
            num_scalar_prefetch=1, grid=(ni, nj),
            in_specs=in_specs + [page_spec(r) for r in range(pps)],
            out_specs=[y_spec, km_spec]),
        out_shape=[y_shape, jax.ShapeDtypeStruct((n_events, bps, ATTN_WIDTH), F32)],
        compiler_params=_params("arbitrary", "arbitrary"), name="ffn_kmean",
    )(page_table_flat, h2, wg_bf, wu_bf, wd_bf, x1, gf, *([cache_k] * pps))
    return y, km.reshape(n_events * bps, ATTN_WIDTH)[:total // PAGES_PER_BLOCK]


def _select_sample_kernel(qh_ref, km_ref, idx_ref, sc_ref, *, n_seq, t_new, n_past_blocks):
    lanes = sc_ref.shape[1]
    lane = lax.broadcasted_iota(jnp.int32, sc_ref.shape, 1)
    sc_ref[...] = jnp.where(lane == n_past_blocks, NEG, -jnp.inf)
    for b in range(n_seq):
        for hd in range(N_HEADS):
            qbh = qh_ref[hd, b * t_new:(b + 1) * t_new, :]
            kmbh = km_ref[b, :, hd * HEAD_DIM:(hd + 1) * HEAD_DIM]
            r0 = (b * N_HEADS + hd) * t_new
            sc_ref[r0:r0 + t_new, 0:n_past_blocks] = lax.dot_general(
                qbh, kmbh, _NT, precision=lax.Precision.HIGHEST, preferred_element_type=F32)
    s = sc_ref[...]
    for r in range(MOBA_TOPK):
        mx = jnp.max(s, axis=-1, keepdims=True)
        am = jnp.min(jnp.where(s == mx, lane, lanes), axis=-1, keepdims=True)
        idx_ref[:, r:r + 1] = am
        s = jnp.where(lane == am, -jnp.inf, s)


def _select_sample(qh, km, n_seq, t_new):
    n_past_blocks = km.shape[1]
    rows = n_seq * N_HEADS * t_new
    return pl.pallas_call(
        functools.partial(_select_sample_kernel, n_seq=n_seq, t_new=t_new, n_past_blocks=n_past_blocks),
        out_shape=jax.ShapeDtypeStruct((rows, MOBA_TOPK), jnp.int32),
        scratch_shapes=[pltpu.VMEM((rows, 128), F32)],
        compiler_params=pltpu.CompilerParams(vmem_limit_bytes=VMEM_LIMIT_BYTES),
        name="select_sample",
    )(qh, km)


def _attn_sample_kernel(pt_ref, idx_ref, slopes_ref, qh_ref, kh_ref, vh_ref, ck_hbm, cv_hbm, o_ref,
                        kbuf, vbuf, sem, *, n_seq, t_new, n_pages):
    n_past_blocks = n_pages // PAGES_PER_BLOCK
    past_len = n_pages * PAGE_SIZE
    n_slots = t_new * MOBA_TOPK
    b = pl.program_id(0)
    hd = pl.program_id(1)
    step = b * N_HEADS + hd
    buf = lax.rem(step, 2)

    def block_of(bb, hh, qj):
        return idx_ref[(bb * N_HEADS + hh) * n_slots + qj]

    def gather(bb, hh, sl, start):
        for qj in range(n_slots):
            blk = jnp.minimum(block_of(bb, hh, qj), n_past_blocks - 1)
            for half in range(PAGES_PER_BLOCK):
                phys = pt_ref[bb * n_pages + blk * PAGES_PER_BLOCK + half]
                rows = pl.ds(half * PAGE_SIZE, PAGE_SIZE)
                for src, dst, s in ((ck_hbm, kbuf, 0), (cv_hbm, vbuf, 1)):
                    cp = pltpu.make_async_copy(src.at[0, phys, :, hh, :], dst.at[sl, qj, rows, :], sem.at[s, sl])
                    if start:
                        cp.start()
                    else:
                        cp.wait()

    @pl.when(step == 0)
    def _():
        gather(b, hd, buf, True)

    @pl.when(step + 1 < n_seq * N_HEADS)
    def _():
        nxt = step + 1
        gather(nxt // N_HEADS, lax.rem(nxt, N_HEADS), 1 - buf, True)

    gather(b, hd, buf, False)

    slope = slopes_ref[hd]
    rows = pl.ds(b * t_new, t_new)
    qs = (qh_ref[rows, :] * ATTN_SCALE).astype(BF16)
    row = lax.broadcasted_iota(jnp.int32, (t_new, MOBA_BLOCK), 0)
    col = lax.broadcasted_iota(jnp.int32, (t_new, MOBA_BLOCK), 1)

    pieces = []
    for qj in range(n_slots):
        blk = block_of(b, hd, qj)
        s = lax.dot_general(qs, kbuf[buf, qj].astype(BF16), _NT, preferred_element_type=F32)
        dist = (past_len + row - (blk * MOBA_BLOCK + col)).astype(F32)
        keep = jnp.logical_and(row == qj // MOBA_TOPK, blk < n_past_blocks)
        pieces.append(jnp.where(keep, s - slope * dist, NEG))
    own_rel = (lax.broadcasted_iota(jnp.int32, (t_new, t_new), 0)
               - lax.broadcasted_iota(jnp.int32, (t_new, t_new), 1))
    s_own = lax.dot_general(qs, kh_ref[rows, :].astype(BF16), _NT, preferred_element_type=F32)
    s_own = jnp.where(own_rel >= 0, s_own - slope * own_rel.astype(F32), NEG)

    m = jnp.max(s_own, axis=-1, keepdims=True)
    for s in pieces:
        m = jnp.maximum(m, jnp.max(s, axis=-1, keepdims=True))
    p_own = jnp.exp(s_own - m)
    l = jnp.sum(p_own, axis=-1, keepdims=True)
    acc = jnp.dot(p_own.astype(BF16), vh_ref[rows, :].astype(BF16), preferred_element_type=F32)
    for qj, s in enumerate(pieces):
        p = jnp.exp(s - m)
        l = l + jnp.sum(p, axis=-1, keepdims=True)
        acc = acc + jnp.dot(p.astype(BF16), vbuf[buf, qj].astype(BF16), preferred_element_type=F32)
    o_ref[...] = acc / l


def _attn_sample(page_table_flat, idx_flat, slopes, qh, kh, vh, cache_k, cache_v, n_seq, t_new, n_pages):
    n_slots = t_new * MOBA_TOPK
    head_blk = pl.BlockSpec((None, n_seq * t_new, HEAD_DIM), lambda b, h, pt, ix, sl: (h, 0, 0))
    grid_spec = pltpu.PrefetchScalarGridSpec(
        num_scalar_prefetch=3,
        grid=(n_seq, N_HEADS),
        in_specs=[head_blk, head_blk, head_blk,
                  pl.BlockSpec(memory_space=pl.ANY), pl.BlockSpec(memory_space=pl.ANY)],
        out_specs=pl.BlockSpec((None, t_new, HEAD_DIM), lambda b, h, pt, ix, sl: (b, 0, h)),
        scratch_shapes=[
            pltpu.VMEM((2, n_slots, MOBA_BLOCK, HEAD_DIM), F32),
            pltpu.VMEM((2, n_slots, MOBA_BLOCK, HEAD_DIM), F32),
            pltpu.SemaphoreType.DMA((2, 2)),
        ],
    )
    return pl.pallas_call(
        functools.partial(_attn_sample_kernel, n_seq=n_seq, t_new=t_new, n_pages=n_pages),
        grid_spec=grid_spec,
        out_shape=jax.ShapeDtypeStruct((n_seq, t_new, ATTN_WIDTH), F32),
        compiler_params=_params("arbitrary", "arbitrary"),
        name="attn_sample",
    )(page_table_flat, idx_flat, slopes, qh, kh, vh, cache_k, cache_v)


def kernel(x_prompt, x_sample, cache_k, cache_v, state_pool, page_table, norm1_g, w_in, w_pool, pool_scale,
           w_out, norm2_g, w_gate, w_up, w_down, norm_f_g):
    depth = w_in.shape[0]
    assert depth == 1, "single-layer step"
    batch, seq, _ = x_prompt.shape
    n_seq, t_new, _ = x_sample.shape
    n_pages = page_table.shape[1]
    past_len = n_pages * PAGE_SIZE
    assert seq % MOBA_BLOCK == 0 and past_len % MOBA_BLOCK == 0 and t_new <= MOBA_BLOCK

    slopes = jnp.exp2(-8.0 * jnp.arange(1, N_HEADS + 1, dtype=F32) / N_HEADS)
    g1 = norm1_g[0].reshape(1, D_MODEL)
    g2 = norm2_g[0].reshape(1, D_MODEL)
    gf = norm_f_g.reshape(1, D_MODEL)
    ps = pool_scale[0].reshape(1, POOL_WIDTH)
    w_in_bf = w_in[0].astype(BF16)
    w_pool_bf = w_pool[0].astype(BF16)

    xp = x_prompt.reshape(batch * seq, D_MODEL)
    pt_flat = page_table.reshape(-1)
    k, v, u, qh_p, kbh, vbh, km = _proj_prompt(xp, g1, w_in_bf)
    km = km.reshape(batch, seq // MOBA_BLOCK, ATTN_WIDTH)
    a, (w_out_bf, w_gate_bf, w_up_bf, w_down_bf) = _attn_prompt(
        qh_p, kbh, vbh, km, _alibi_key_table(slopes, seq), batch, seq,
        (w_out[0], w_gate[0], w_up[0], w_down[0]))
    x1, h2, hist_p = _mid_prompt(a, u, xp, w_pool_bf, ps, w_out_bf, g2, batch, seq)
    y_p, km_s = _ffn(h2, w_gate_bf, w_up_bf, w_down_bf, x1, gf, tm=512, paged_k=(pt_flat, cache_k))

    xs = x_sample.reshape(n_seq * t_new, D_MODEL)
    k_s, v_s, u_s, qh, kh, vh = _proj_sample(xs, g1, w_in_bf)
    km_s = km_s.reshape(n_seq, n_pages // PAGES_PER_BLOCK, ATTN_WIDTH)
    idx = _select_sample(qh, km_s, n_seq, t_new)
    a_s = _attn_sample(pt_flat, idx.reshape(-1), slopes, qh, kh, vh, cache_k, cache_v, n_seq, t_new, n_pages)
    x1_s, h2_s, hist_s = _mid_sample(a_s.reshape(n_seq * t_new, ATTN_WIDTH), u_s, state_pool[0], xs,
                                     w_pool_bf, ps, w_out_bf, g2, n_seq, t_new, past_len)
    y_s = _ffn(h2_s, w_gate_bf, w_up_bf, w_down_bf, x1_s, gf, tm=n_seq * t_new)

    kv_p = (depth, batch, seq, N_HEADS, HEAD_DIM)
    kv_s = (depth, n_seq, t_new, N_HEADS, HEAD_DIM)
    return (y_p.reshape(batch, seq, D_MODEL), y_s.reshape(n_seq, t_new, D_MODEL),
            k.reshape(kv_p), v.reshape(kv_p), hist_p[None],
            k_s.reshape(kv_s), v_s.reshape(kv_s), hist_s[None])
```

```python
import functools

import jax
import jax.numpy as jnp
from jax import lax
from jax.experimental import pallas as pl
from jax.experimental.pallas import tpu as pltpu

D_MODEL = 2048
HEAD_DIM = 128
N_HEADS = 8
ATTN_WIDTH = N_HEADS * HEAD_DIM
POOL_WIDTH = D_MODEL - ATTN_WIDTH
IN_WIDTH = 3 * ATTN_WIDTH + POOL_WIDTH
MOBA_BLOCK = 256
MOBA_TOPK = 3
PAGE_SIZE = 128
PAGES_PER_BLOCK = MOBA_BLOCK // PAGE_SIZE
POOL_WINDOWS = (2, 4, 8, 16)
POOL_GROUP_W = POOL_WIDTH // len(POOL_WINDOWS)
POOL_HIST = max(POOL_WINDOWS) - 1
HIST_PAD = 16
RMS_EPS = 1e-6
NEG = -1e30
ATTN_SCALE = HEAD_DIM ** -0.5

V7X_VMEM_BYTES = 64 * 1024 * 1024
VMEM_LIMIT_BYTES = V7X_VMEM_BYTES - 8 * 1024 * 1024

F32 = jnp.float32
BF16 = jnp.bfloat16
_NT = (((1,), (1,)), ((), ()))


def _rmsnorm(x, g):
    y = x * lax.rsqrt(jnp.mean(x * x, axis=-1, keepdims=True) + RMS_EPS)
    return y * g


def _params(*semantics):
    return pltpu.CompilerParams(dimension_semantics=semantics, vmem_limit_bytes=VMEM_LIMIT_BYTES)


def _proj_prompt_kernel(x_ref, g_ref, w_ref, k_ref, v_ref, u_ref, qh_ref, kbh_ref, vbh_ref, km_ref, *, tm):
    h = _rmsnorm(x_ref[...], g_ref[...]).astype(BF16)

    def seg(s):
        return jnp.dot(h, w_ref[:, s * ATTN_WIDTH:(s + 1) * ATTN_WIDTH], preferred_element_type=F32)

    zq = seg(0)
    for hd in range(N_HEADS):
        qh_ref[hd] = zq[:, hd * HEAD_DIM:(hd + 1) * HEAD_DIM]
    zk = seg(1)
    k_ref[...] = zk
    for hd in range(N_HEADS):
        kbh_ref[hd] = zk[:, hd * HEAD_DIM:(hd + 1) * HEAD_DIM].astype(BF16)
    for blk in range(tm // MOBA_BLOCK):
        rows = zk[blk * MOBA_BLOCK:(blk + 1) * MOBA_BLOCK]
        km_ref[blk] = jnp.sum(rows, axis=0, keepdims=True) * (1.0 / MOBA_BLOCK)
    zv = seg(2)
    v_ref[...] = zv
    for hd in range(N_HEADS):
        vbh_ref[hd] = zv[:, hd * HEAD_DIM:(hd + 1) * HEAD_DIM].astype(BF16)
    u_ref[...] = seg(3)


def _proj_prompt(x, g, w_bf, tm=256):
    m = x.shape[0]
    row = lambda i: (i, 0)
    f32_out = jax.ShapeDtypeStruct((m, ATTN_WIDTH), F32)
    km_out = jax.ShapeDtypeStruct((m // MOBA_BLOCK, 1, ATTN_WIDTH), F32)
    blk = pl.BlockSpec((tm, ATTN_WIDTH), row)
    hm_blk = pl.BlockSpec((N_HEADS, tm, HEAD_DIM), lambda i: (0, i, 0))
    return pl.pallas_call(
        functools.partial(_proj_prompt_kernel, tm=tm),
        grid=(m // tm,),
        in_specs=[
            pl.BlockSpec((tm, D_MODEL), row),
            pl.BlockSpec((1, D_MODEL), lambda i: (0, 0)),
            pl.BlockSpec((D_MODEL, IN_WIDTH), lambda i: (0, 0), pipeline_mode=pl.Buffered(1)),
        ],
        out_specs=[blk, blk, blk, hm_blk, hm_blk, hm_blk,
                   pl.BlockSpec((tm // MOBA_BLOCK, 1, ATTN_WIDTH), lambda i: (i, 0, 0))],
        out_shape=[f32_out, f32_out, f32_out,
                   jax.ShapeDtypeStruct((N_HEADS, m, HEAD_DIM), F32),
                   jax.ShapeDtypeStruct((N_HEADS, m, HEAD_DIM), BF16),
                   jax.ShapeDtypeStruct((N_HEADS, m, HEAD_DIM), BF16), km_out],
        compiler_params=_params("arbitrary"),
        name="proj_prompt",
    )(x, g, w_bf)


def _proj_sample_kernel(x_ref, g_ref, w_ref, k_ref, v_ref, u_ref, qh_ref, kh_ref, vh_ref):
    h = _rmsnorm(x_ref[...], g_ref[...]).astype(BF16)
    zs = [jnp.dot(h, w_ref[:, s * ATTN_WIDTH:(s + 1) * ATTN_WIDTH], preferred_element_type=F32)
          for s in range(4)]
    k_ref[...] = zs[1]
    v_ref[...] = zs[2]
    u_ref[...] = zs[3]
    for hd in range(N_HEADS):
        ls = slice(hd * HEAD_DIM, (hd + 1) * HEAD_DIM)
        qh_ref[hd] = zs[0][:, ls]
        kh_ref[hd] = zs[1][:, ls]
        vh_ref[hd] = zs[2][:, ls]


def _proj_sample(x, g, w_bf):
    m = x.shape[0]
    f32_out = jax.ShapeDtypeStruct((m, ATTN_WIDTH), F32)
    hm_out = jax.ShapeDtypeStruct((N_HEADS, m, HEAD_DIM), F32)
    return pl.pallas_call(
        _proj_sample_kernel,
        out_shape=[f32_out, f32_out, f32_out, hm_out, hm_out, hm_out],
        compiler_params=pltpu.CompilerParams(vmem_limit_bytes=VMEM_LIMIT_BYTES),
        name="proj_sample",
    )(x, g, w_bf)


def _alibi_key_table(slopes, seq):
    nb = seq // MOBA_BLOCK
    t = jnp.arange(seq, dtype=jnp.int32)
    blk, local = t // MOBA_BLOCK, t % MOBA_BLOCK
    onehot = (blk[:, None] == jnp.arange(nb, dtype=jnp.int32)[None, :]).astype(F32)
    onehot = jnp.broadcast_to(onehot[None], (N_HEADS, seq, nb))
    pos = jnp.stack([local, blk * MOBA_BLOCK], axis=-1).astype(F32)[None] * slopes[:, None, None]
    pad = jnp.zeros((N_HEADS, seq, HEAD_DIM - nb - 2), F32)
    return jnp.concatenate([onehot, pos, pad], axis=-1).astype(BF16)


def _split_bf16(x):
    hi = x.astype(BF16)
    return hi, (x - hi.astype(F32)).astype(BF16)


def _topk_block_bias_t(scores_t, n_past):
    nb = scores_t.shape[0]
    n_iota = lax.broadcasted_iota(jnp.int32, scores_t.shape, 0)
    past = n_iota < n_past
    s = jnp.where(past, scores_t, NEG)
    rank = jnp.zeros(scores_t.shape, jnp.int32)
    for m in range(nb):
        sm = s[m:m + 1, :]
        tie_lower = jnp.logical_and(sm == s, n_iota > m)
        rank = rank + jnp.where(sm > s, 1, jnp.where(tie_lower, 1, 0))
    keep = jnp.logical_or(jnp.logical_and(past, rank < MOBA_TOPK), n_iota == n_past)
    return jnp.where(keep, 0.0, NEG)


def _attn_prompt_kernel(q_ref, kb_ref, vb_ref, km_ref, kx_ref, *refs, nb, n_cast):
    w_refs, o_ref, wbf_refs = refs[:n_cast], refs[n_cast], refs[n_cast + 1:2 * n_cast + 1]
    qa_ref, ka_ref, s_ref, p_ref, acc_ref = refs[2 * n_cast + 1:]
    for w_ref, wbf_ref in zip(w_refs, wbf_refs):
        wbf_ref[...] = w_ref[...].astype(BF16)
    seq = nb * MOBA_BLOCK
    km_hl = jnp.concatenate(_split_bf16(km_ref[...]), axis=0)
    shape = (MOBA_BLOCK, MOBA_BLOCK)
    causal = lax.broadcasted_iota(jnp.int32, shape, 0) >= lax.broadcasted_iota(jnp.int32, shape, 1)

    ka_ref[:, 0:HEAD_DIM] = kb_ref[...]
    ka_ref[:, HEAD_DIM:] = kx_ref[...]
    ones_rows = jnp.where(lax.broadcasted_iota(jnp.int32, (8, MOBA_BLOCK), 0) < 2, 1.0, 0.0)
    zero_rows = jnp.zeros((HEAD_DIM - nb - 8, MOBA_BLOCK), F32)
    for qi in range(nb):
        rows = slice(qi * MOBA_BLOCK, (qi + 1) * MOBA_BLOCK)
        q = q_ref[rows, :]
        q_hi, q_lo = _split_bf16(q)
        sel_hl = lax.dot_general(km_hl, q_hi, _NT, preferred_element_type=F32)
        sel_t = (sel_hl[0:nb] + sel_hl[nb:2 * nb]
                 + lax.dot_general(km_hl[0:nb], q_lo, _NT, preferred_element_type=F32))
        bias_t = _topk_block_bias_t(sel_t, qi)
        qa_ref[rows, 0:HEAD_DIM] = (q * ATTN_SCALE).astype(BF16)
        qa_ref[rows, HEAD_DIM:] = jnp.concatenate([bias_t, ones_rows, zero_rows], axis=0).T.astype(BF16)

    for n in range(nb):
        r0 = n * MOBA_BLOCK
        s = lax.dot_general(qa_ref[r0:seq, :], ka_ref[r0:r0 + MOBA_BLOCK, :], _NT, preferred_element_type=F32)
        s_ref[n, r0:r0 + MOBA_BLOCK, :] = jnp.where(causal, s[0:MOBA_BLOCK], NEG)
        if r0 + MOBA_BLOCK < seq:
            s_ref[n, r0 + MOBA_BLOCK:seq, :] = s[MOBA_BLOCK:]

    inv_l = []
    for qi in range(nb):
        rows = slice(qi * MOBA_BLOCK, (qi + 1) * MOBA_BLOCK)
        m = jnp.max(s_ref[0, rows, :], axis=-1, keepdims=True)
        for n in range(1, qi + 1):
            m = jnp.maximum(m, jnp.max(s_ref[n, rows, :], axis=-1, keepdims=True))
        l = jnp.zeros((MOBA_BLOCK, 1), F32)
        for n in range(qi + 1):
            p = jnp.exp(s_ref[n, rows, :] - m)
            l = l + jnp.sum(p, axis=-1, keepdims=True)
            p_ref[n, rows, :] = p.astype(BF16)
        inv_l.append(1.0 / l)

    for n in range(nb):
        r0 = n * MOBA_BLOCK
        pv = jnp.dot(p_ref[n, r0:seq, :], vb_ref[r0:r0 + MOBA_BLOCK, :], preferred_element_type=F32)
        if n == 0:
            acc_ref[...] = pv
        else:
            acc_ref[r0:seq, :] += pv
    for qi in range(nb):
        rows = slice(qi * MOBA_BLOCK, (qi + 1) * MOBA_BLOCK)
        o_ref[rows, :] = (acc_ref[rows, :] * inv_l[qi]).astype(o_ref.dtype)


def _attn_prompt(qh, kbh, vbh, km, kx, batch, seq, cast_weights):
    nb = seq // MOBA_BLOCK
    assert nb == 8, "bias rows fill one sublane tile"
    steps = batch * N_HEADS
    slab = pl.BlockSpec((None, seq, HEAD_DIM), lambda b, h: (h, b, 0))
    chunk_specs = []
    for w in cast_weights:
        assert w.shape[0] % (16 * steps) == 0, "bf16 row chunks must be whole (16, 128) tiles"
        chunk_specs.append(pl.BlockSpec((w.shape[0] // steps, w.shape[1]), lambda b, h: (b * N_HEADS + h, 0)))
    res = pl.pallas_call(
        functools.partial(_attn_prompt_kernel, nb=nb, n_cast=len(cast_weights)),
        grid=(batch, N_HEADS),
        in_specs=[
            slab, slab, slab,
            pl.BlockSpec((None, nb, HEAD_DIM), lambda b, h: (b, 0, h)),
            pl.BlockSpec((None, seq, HEAD_DIM), lambda b, h: (h, 0, 0)),
        ] + chunk_specs,
        out_specs=[slab] + chunk_specs,
        out_shape=[jax.ShapeDtypeStruct((N_HEADS, batch * seq, HEAD_DIM), BF16)]
        + [jax.ShapeDtypeStruct(w.shape, BF16) for w in cast_weights],
        scratch_shapes=[
            pltpu.VMEM((seq, 2 * HEAD_DIM), BF16),
            pltpu.VMEM((seq, 2 * HEAD_DIM), BF16),
            pltpu.VMEM((nb, seq, MOBA_BLOCK), F32),
            pltpu.VMEM((nb, seq, MOBA_BLOCK), BF16),
            pltpu.VMEM((seq, HEAD_DIM), F32),
        ],
        compiler_params=_params("arbitrary", "arbitrary"),
        name="attn_prompt",
    )(qh, kbh, vbh, km, kx, *cast_weights)
    return res[0], res[1:]


def _pool_diff(ext_ref, r0, n, pos, d_ref, d0):
    for g, w in enumerate(POOL_WINDOWS):
        ls = slice(g * POOL_GROUP_W, (g + 1) * POOL_GROUP_W)
        cur = ext_ref[pl.ds(r0, n), ls]
        win = cur
        for j in range(1, w):
            win = win + ext_ref[pl.ds(r0 - j, n), ls]
        cnt = jnp.minimum(pos + 1, w).astype(F32)
        d_ref[pl.ds(d0, n), ls] = win / cnt - cur


def _mix_out(a_bf, d_ref, wp_ref, ps_ref, wo_ref):
    out = jnp.dot(a_bf, wo_ref[0:ATTN_WIDTH, :], preferred_element_type=F32)
    ys = []
    for g in range(len(POOL_WINDOWS)):
        ls = slice(g * POOL_GROUP_W, (g + 1) * POOL_GROUP_W)
        y = jnp.dot(d_ref[:, ls].astype(BF16), wp_ref[g], preferred_element_type=F32)
        ys.append((y * ps_ref[:, ls]).astype(BF16))
    return out + jnp.dot(jnp.concatenate(ys, axis=-1), wo_ref[ATTN_WIDTH:, :], preferred_element_type=F32)


def _mid_prompt_kernel(a_ref, u_ref, uprev_ref, x_ref, wp_ref, ps_ref, wo_ref, g2_ref,
                       x1_ref, h2_ref, hist_ref, ext_ref, d_ref, *, tm, tiles_per_seq):
    t_in_seq = lax.rem(pl.program_id(0), tiles_per_seq)
    ext_ref[0:HIST_PAD, :] = jnp.where(t_in_seq == 0, 0.0, uprev_ref[...])
    ext_ref[HIST_PAD:HIST_PAD + tm, :] = u_ref[...]
    pos = t_in_seq * tm + lax.broadcasted_iota(jnp.int32, (tm, 1), 0)
    _pool_diff(ext_ref, HIST_PAD, tm, pos, d_ref, 0)
    a = jnp.concatenate([a_ref[hd] for hd in range(N_HEADS)], axis=-1)
    x1 = x_ref[...] + _mix_out(a, d_ref, wp_ref, ps_ref, wo_ref)
    x1_ref[...] = x1
    h2_ref[...] = _rmsnorm(x1, g2_ref[...]).astype(BF16)

    @pl.when(t_in_seq == tiles_per_seq - 1)
    def _():
        hist_ref[...] = ext_ref[pl.ds(HIST_PAD + tm - POOL_HIST, POOL_HIST), :]


def _mid_prompt(a, u, x, wp_bf, ps, wo_bf, g2, batch, seq, tm=256):
    m = x.shape[0]
    tiles_per_seq = seq // tm
    row = lambda i: (i, 0)
    const2 = lambda i: (0, 0)
    return pl.pallas_call(
        functools.partial(_mid_prompt_kernel, tm=tm, tiles_per_seq=tiles_per_seq),
        grid=(m // tm,),
        in_specs=[
            pl.BlockSpec((N_HEADS, tm, HEAD_DIM), lambda i: (0, i, 0)),
            pl.BlockSpec((tm, POOL_WIDTH), row),
            pl.BlockSpec((HIST_PAD, POOL_WIDTH), lambda i: (jnp.maximum(i * (tm // HIST_PAD) - 1, 0), 0)),
            pl.BlockSpec((tm, D_MODEL), row),
            pl.BlockSpec(wp_bf.shape, lambda i: (0, 0, 0), pipeline_mode=pl.Buffered(1)),
            pl.BlockSpec((1, POOL_WIDTH), const2),
            pl.BlockSpec((D_MODEL, D_MODEL), const2, pipeline_mode=pl.Buffered(1)),
            pl.BlockSpec((1, D_MODEL), const2),
        ],
        out_specs=[
            pl.BlockSpec((tm, D_MODEL), row),
            pl.BlockSpec((tm, D_MODEL), row),
            pl.BlockSpec((None, POOL_HIST, POOL_WIDTH), lambda i: (i // tiles_per_seq, 0, 0)),
        ],
        out_shape=[
            jax.ShapeDtypeStruct((m, D_MODEL), F32),
            jax.ShapeDtypeStruct((m, D_MODEL), BF16),
            jax.ShapeDtypeStruct((batch, POOL_HIST, POOL_WIDTH), F32),
        ],
        scratch_shapes=[pltpu.VMEM((HIST_PAD + tm, POOL_WIDTH), F32), pltpu.VMEM((tm, POOL_WIDTH), F32)],
        compiler_params=_params("arbitrary"),
        name="mid_prompt",
    )(a, u, u, x, wp_bf, ps, wo_bf, g2)


def _mid_sample_kernel(a_ref, u_ref, hist_ref, x_ref, wp_ref, ps_ref, wo_ref, g2_ref,
                       x1_ref, h2_ref, newhist_ref, ext_ref, d_ref, *, n_seq, t_new, start_pos):
    stride = POOL_HIST + t_new
    stride += -stride % 8
    pos = start_pos + lax.broadcasted_iota(jnp.int32, (t_new, 1), 0)
    for b in range(n_seq):
        base = b * stride
        ext_ref[base:base + POOL_HIST, :] = hist_ref[b]
        ext_ref[base + POOL_HIST:base + POOL_HIST + t_new, :] = u_ref[b * t_new:(b + 1) * t_new, :]
        _pool_diff(ext_ref, base + POOL_HIST, t_new, pos, d_ref, b * t_new)
        newhist_ref[b] = ext_ref[pl.ds(base + t_new, POOL_HIST), :]
    x1 = x_ref[...] + _mix_out(a_ref[...].astype(BF16), d_ref, wp_ref, ps_ref, wo_ref)
    x1_ref[...] = x1
    h2_ref[...] = _rmsnorm(x1, g2_ref[...]).astype(BF16)


def _mid_sample(a, u, hist, x, wp_bf, ps, wo_bf, g2, n_seq, t_new, start_pos):
    m = x.shape[0]
    stride = POOL_HIST + t_new
    stride += -stride % 8
    return pl.pallas_call(
        functools.partial(_mid_sample_kernel, n_seq=n_seq, t_new=t_new, start_pos=start_pos),
        out_shape=[
            jax.ShapeDtypeStruct((m, D_MODEL), F32),
            jax.ShapeDtypeStruct((m, D_MODEL), BF16),
            jax.ShapeDtypeStruct((n_seq, POOL_HIST, POOL_WIDTH), F32),
        ],
        scratch_shapes=[pltpu.VMEM((n_seq * stride, POOL_WIDTH), F32), pltpu.VMEM((m, POOL_WIDTH), F32)],
        compiler_params=pltpu.CompilerParams(vmem_limit_bytes=VMEM_LIMIT_BYTES),
        name="mid_sample",
    )(a, u, hist, x, wp_bf, ps, wo_bf, g2)


def _store_block_mean(o_ref, jb, tot):
    mean = tot * (1.0 / MOBA_BLOCK)
    for hd in range(N_HEADS):
        o_ref[jb:jb + 1, hd * HEAD_DIM:(hd + 1) * HEAD_DIM] = mean[hd:hd + 1, :]


def _ffn_kernel(*refs, n_pages):
    if n_pages:
        refs = refs[1:]
    h2_ref, wg_ref, wu_ref, wd_ref, x1_ref, gf_ref = refs[:6]
    page_refs = refs[6:6 + n_pages]
    y_ref = refs[6 + n_pages]
    km_ref = refs[7 + n_pages] if n_pages else None
    j = pl.program_id(1)

    @pl.when(j == 0)
    def _():
        y_ref[...] = x1_ref[...]

    n_slots = 6
    sums = {}

    def side_job(slot):
        for r in range(slot, n_pages, n_slots):
            sums[r] = jnp.sum(page_refs[r][...], axis=0)
            jb, pos = divmod(r, PAGES_PER_BLOCK)
            if pos == PAGES_PER_BLOCK - 1:
                tot = sums[jb * PAGES_PER_BLOCK]
                for rr in range(jb * PAGES_PER_BLOCK + 1, r + 1):
                    tot = tot + sums[rr]
                _store_block_mean(km_ref, jb, tot)

    h2 = h2_ref[...]
    tf = wg_ref.shape[1]
    gate, up = [], []
    for c, cols in enumerate((slice(0, tf // 2), slice(tf // 2, tf))):
        gate.append(jnp.dot(h2, wg_ref[:, cols], preferred_element_type=F32))
        side_job(c)
    for c, cols in enumerate((slice(0, tf // 2), slice(tf // 2, tf))):
        up.append(jnp.dot(h2, wu_ref[:, cols], preferred_element_type=F32))
        side_job(2 + c)
    gate = jnp.concatenate(gate, axis=-1)
    act = (gate * jax.nn.sigmoid(gate) * jnp.concatenate(up, axis=-1)).astype(BF16)
    for c, cols in enumerate((slice(0, D_MODEL // 2), slice(D_MODEL // 2, D_MODEL))):
        y_ref[:, cols] += jnp.dot(act, wd_ref[:, cols], preferred_element_type=F32)
        side_job(4 + c)

    @pl.when(j == pl.num_programs(1) - 1)
    def _():
        y_ref[...] = _rmsnorm(y_ref[...], gf_ref[...])


def _ffn(h2, wg_bf, wu_bf, wd_bf, x1, gf, tm, tf=512, paged_k=None):
    m = h2.shape[0]
    d_ff = wg_bf.shape[1]
    ni, nj = m // tm, d_ff // tf
    in_specs = [
        pl.BlockSpec((tm, D_MODEL), lambda i, j, *_: (i, 0)),
        pl.BlockSpec((D_MODEL, tf), lambda i, j, *_: (0, j)),
        pl.BlockSpec((D_MODEL, tf), lambda i, j, *_: (0, j)),
        pl.BlockSpec((tf, D_MODEL), lambda i, j, *_: (j, 0)),
        pl.BlockSpec((tm, D_MODEL), lambda i, j, *_: (i, 0)),
        pl.BlockSpec((1, D_MODEL), lambda i, j, *_: (0, 0)),
    ]
    y_spec = pl.BlockSpec((tm, D_MODEL), lambda i, j, *_: (i, 0))
    y_shape = jax.ShapeDtypeStruct((m, D_MODEL), F32)
    if paged_k is None:
        return pl.pallas_call(
            functools.partial(_ffn_kernel, n_pages=0),
            grid=(ni, nj), in_specs=in_specs, out_specs=y_spec, out_shape=y_shape,
            compiler_params=_params("arbitrary", "arbitrary"), name="ffn",
        )(h2, wg_bf, wu_bf, wd_bf, x1, gf)

    page_table_flat, cache_k = paged_k
    total = page_table_flat.shape[0]
    pps = PAGES_PER_BLOCK * pl.cdiv(total, PAGES_PER_BLOCK * ni * nj)
    n_events = pl.cdiv(total, pps)

    def page_spec(r):
        def index_map(i, j, pt):
            event = jnp.minimum(i * nj + j, n_events - 1)
            return (0, pt[jnp.minimum(event * pps + r, total - 1)], 0, 0, 0)
        return pl.BlockSpec((None, None, PAGE_SIZE, N_HEADS, HEAD_DIM), index_map)

    bps = pps // PAGES_PER_BLOCK
    km_spec = pl.BlockSpec((None, bps, ATTN_WIDTH), lambda i, j, pt: (jnp.minimum(i * nj + j, n_events - 1), 0, 0))
    y, km = pl.pallas_call(
        functools.partial(_ffn_kernel, n_pages=pps),
        grid_spec=pltpu.PrefetchScalarGridSpec(
            num_scalar_prefetch=1, grid=(ni, nj),
            in_specs=in_specs + [page_spec(r) for r in range(pps)],
            out_specs=[y_spec, km_spec]),
        out_shape=[y_shape, jax.ShapeDtypeStruct((n_events, bps, ATTN_WIDTH), F32)],
        compiler_params=_params("arbitrary", "arbitrary"), name="ffn_kmean",
    )(page_table_flat, h2, wg_bf, wu_bf, wd_bf, x1, gf, *([cache_k] * pps))
    return y, km.reshape(n_events * bps, ATTN_WIDTH)[:total // PAGES_PER_BLOCK]


def _select_sample_kernel(qh_ref, km_ref, idx_ref, sc_ref, *, n_seq, t_new, n_past_blocks):
    lanes = sc_ref.shape[1]
    lane = lax.broadcasted_iota(jnp.int32, sc_ref.shape, 1)
    sc_ref[...] = jnp.where(lane == n_past_blocks, NEG, -jnp.inf)
    for b in range(n_seq):
        for hd in range(N_HEADS):
            qbh = qh_ref[hd, b * t_new:(b + 1) * t_new, :]
            kmbh = km_ref[b, :, hd * HEAD_DIM:(hd + 1) * HEAD_DIM]
            r0 = (b * N_HEADS + hd) * t_new
            sc_ref[r0:r0 + t_new, 0:n_past_blocks] = lax.dot_general(
                qbh, kmbh, _NT, precision=lax.Precision.HIGHEST, preferred_element_type=F32)
    s = sc_ref[...]
    for r in range(MOBA_TOPK):
        mx = jnp.max(s, axis=-1, keepdims=True)
        am = jnp.min(jnp.where(s == mx, lane, lanes), axis=-1, keepdims=True)
        idx_ref[:, r:r + 1] = am
        s = jnp.where(lane == am, -jnp.inf, s)


def _select_sample(qh, km, n_seq, t_new):
    n_past_blocks = km.shape[1]
    rows = n_seq * N_HEADS * t_new
    return pl.pallas_call(
        functools.partial(_select_sample_kernel, n_seq=n_seq, t_new=t_new, n_past_blocks=n_past_blocks),
        out_shape=jax.ShapeDtypeStruct((rows, MOBA_TOPK), jnp.int32),
        scratch_shapes=[pltpu.VMEM((rows, 128), F32)],
        compiler_params=pltpu.CompilerParams(vmem_limit_bytes=VMEM_LIMIT_BYTES),
        name="select_sample",
    )(qh, km)


GATHER_DEPTH = 2


def _attn_sample_kernel(pt_ref, idx_ref, slopes_ref, qh_ref, kh_ref, vh_ref, ck_hbm, cv_hbm, o_ref,
                        kbuf, vbuf, sem, *, n_seq, t_new, n_pages):
    n_past_blocks = n_pages // PAGES_PER_BLOCK
    past_len = n_pages * PAGE_SIZE
    n_slots = t_new * MOBA_TOPK
    n_keys = n_slots * MOBA_BLOCK
    n_steps = n_seq * N_HEADS
    n_bufs = GATHER_DEPTH + 1
    b = pl.program_id(0)
    hd = pl.program_id(1)
    step = b * N_HEADS + hd

    def block_of(st, qj):
        return idx_ref[st * n_slots + qj]

    def gather(st, start):
        bb, hh, sl = st // N_HEADS, lax.rem(st, N_HEADS), lax.rem(st, n_bufs)
        for qj in range(n_slots):
            blk = jnp.minimum(block_of(st, qj), n_past_blocks - 1)
            for half in range(PAGES_PER_BLOCK):
                phys = pt_ref[bb * n_pages + blk * PAGES_PER_BLOCK + half]
                rows = pl.ds(qj * MOBA_BLOCK + half * PAGE_SIZE, PAGE_SIZE)
                for src, dst, s in ((ck_hbm, kbuf, 0), (cv_hbm, vbuf, 1)):
                    cp = pltpu.make_async_copy(src.at[0, phys, :, hh, :], dst.at[sl, rows, :], sem.at[s, sl])
                    if start:
                        cp.start()
                    else:
                        cp.wait()

    @pl.when(step == 0)
    def _():
        for d in range(min(GATHER_DEPTH, n_steps)):
            gather(step + d, True)

    @pl.when(step + GATHER_DEPTH < n_steps)
    def _():
        gather(step + GATHER_DEPTH, True)

    gather(step, False)

    buf = lax.rem(step, n_bufs)
    slope = slopes_ref[hd]
    rows = pl.ds(b * t_new, t_new)
    qs = (qh_ref[rows, :] * ATTN_SCALE).astype(BF16)

    s = lax.dot_general(qs, kbuf[buf].astype(BF16), _NT, preferred_element_type=F32)
    row = lax.broadcasted_iota(jnp.int32, (t_new, n_keys), 0)
    col = lax.broadcasted_iota(jnp.int32, (t_new, n_keys), 1)
    shift = jnp.zeros((t_new, n_keys), jnp.int32)
    valid = jnp.zeros((t_new, n_keys), jnp.int32)
    for qj in range(n_slots):
        blk = block_of(step, qj)
        in_slot = jnp.logical_and(col >= qj * MOBA_BLOCK, col < (qj + 1) * MOBA_BLOCK)
        shift = jnp.where(in_slot, blk * MOBA_BLOCK - qj * MOBA_BLOCK, shift)
        valid = jnp.where(in_slot, (blk < n_past_blocks).astype(jnp.int32), valid)
    dist = (past_len + row - (col + shift)).astype(F32)
    q_lo = row * (MOBA_TOPK * MOBA_BLOCK)
    mine = jnp.logical_and(col >= q_lo, col < q_lo + MOBA_TOPK * MOBA_BLOCK)
    s = jnp.where(jnp.logical_and(mine, valid > 0), s - slope * dist, NEG)
    own_rel = (lax.broadcasted_iota(jnp.int32, (t_new, t_new), 0)
               - lax.broadcasted_iota(jnp.int32, (t_new, t_new), 1))
    s_own = lax.dot_general(qs, kh_ref[rows, :].astype(BF16), _NT, preferred_element_type=F32)
    s_own = jnp.where(own_rel >= 0, s_own - slope * own_rel.astype(F32), NEG)

    m = jnp.maximum(jnp.max(s_own, axis=-1, keepdims=True), jnp.max(s, axis=-1, keepdims=True))
    p_own = jnp.exp(s_own - m)
    p = jnp.exp(s - m)
    l = jnp.sum(p_own, axis=-1, keepdims=True) + jnp.sum(p, axis=-1, keepdims=True)
    acc = (jnp.dot(p_own.astype(BF16), vh_ref[rows, :].astype(BF16), preferred_element_type=F32)
           + jnp.dot(p.astype(BF16), vbuf[buf].astype(BF16), preferred_element_type=F32))
    o_ref[...] = acc / l


def _attn_sample(page_table_flat, idx_flat, slopes, qh, kh, vh, cache_k, cache_v, n_seq, t_new, n_pages):
    n_slots = t_new * MOBA_TOPK
    head_blk = pl.BlockSpec((None, n_seq * t_new, HEAD_DIM), lambda b, h, pt, ix, sl: (h, 0, 0))
    grid_spec = pltpu.PrefetchScalarGridSpec(
        num_scalar_prefetch=3,
        grid=(n_seq, N_HEADS),
        in_specs=[head_blk, head_blk, head_blk,
                  pl.BlockSpec(memory_space=pl.ANY), pl.BlockSpec(memory_space=pl.ANY)],
        out_specs=pl.BlockSpec((None, t_new, HEAD_DIM), lambda b, h, pt, ix, sl: (b, 0, h)),
        scratch_shapes=[
            pltpu.VMEM((GATHER_DEPTH + 1, n_slots * MOBA_BLOCK, HEAD_DIM), F32),
            pltpu.VMEM((GATHER_DEPTH + 1, n_slots * MOBA_BLOCK, HEAD_DIM), F32),
            pltpu.SemaphoreType.DMA((2, GATHER_DEPTH + 1)),
        ],
    )
    return pl.pallas_call(
        functools.partial(_attn_sample_kernel, n_seq=n_seq, t_new=t_new, n_pages=n_pages),
        grid_spec=grid_spec,
        out_shape=jax.ShapeDtypeStruct((n_seq, t_new, ATTN_WIDTH), F32),
        compiler_params=_params("arbitrary", "arbitrary"),
        name="attn_sample",
    )(page_table_flat, idx_flat, slopes, qh, kh, vh, cache_k, cache_v)


def kernel(x_prompt, x_sample, cache_k, cache_v, state_pool, page_table, norm1_g, w_in, w_pool, pool_scale,
           w_out, norm2_g, w_gate, w_up, w_down, norm_f_g):
    depth = w_in.shape[0]
    assert depth == 1, "single-layer step"
    batch, seq, _ = x_prompt.shape
    n_seq, t_new, _ = x_sample.shape
    n_pages = page_table.shape[1]
    past_len = n_pages * PAGE_SIZE
    assert seq % MOBA_BLOCK == 0 and past_len % MOBA_BLOCK == 0 and t_new <= MOBA_BLOCK

    slopes = jnp.exp2(-8.0 * jnp.arange(1, N_HEADS + 1, dtype=F32) / N_HEADS)
    g1 = norm1_g[0].reshape(1, D_MODEL)
    g2 = norm2_g[0].reshape(1, D_MODEL)
    gf = norm_f_g.reshape(1, D_MODEL)
    ps = pool_scale[0].reshape(1, POOL_WIDTH)
    w_in_bf = w_in[0].astype(BF16)
    w_pool_bf = w_pool[0].astype(BF16)

    xp = x_prompt.reshape(batch * seq, D_MODEL)
    pt_flat = page_table.reshape(-1)
    k, v, u, qh_p, kbh, vbh, km = _proj_prompt(xp, g1, w_in_bf)
    km = km.reshape(batch, seq // MOBA_BLOCK, ATTN_WIDTH)
    a, (w_out_bf, w_gate_bf, w_up_bf, w_down_bf) = _attn_prompt(
        qh_p, kbh, vbh, km, _alibi_key_table(slopes, seq), batch, seq,
        (w_out[0], w_gate[0], w_up[0], w_down[0]))
    x1, h2, hist_p = _mid_prompt(a, u, xp, w_pool_bf, ps, w_out_bf, g2, batch, seq)
    y_p, km_s = _ffn(h2, w_gate_bf, w_up_bf, w_down_bf, x1, gf, tm=512, paged_k=(pt_flat, cache_k))

    xs = x_sample.reshape(n_seq * t_new, D_MODEL)
    k_s, v_s, u_s, qh, kh, vh = _proj_sample(xs, g1, w_in_bf)
    km_s = km_s.reshape(n_seq, n_pages // PAGES_PER_BLOCK, ATTN_WIDTH)
    idx = _select_sample(qh, km_s, n_seq, t_new)
    a_s = _attn_sample(pt_flat, idx.reshape(-1), slopes, qh, kh, vh, cache_k, cache_v, n_seq, t_new, n_pages)
    x1_s, h2_s, hist_s = _mid_sample(a_s.reshape(n_seq * t_new, ATTN_WIDTH), u_s, state_pool[0], xs,
                                     w_pool_bf, ps, w_out_bf, g2, n_seq, t_new, past_len)
    y_s = _ffn(h2_s, w_gate_bf, w_up_bf, w_down_bf, x1_s, gf, tm=n_seq * t_new)

    kv_p = (depth, batch, seq, N_HEADS, HEAD_DIM)
    kv_s = (depth, n_seq, t_new, N_HEADS, HEAD_DIM)
    return (y_p.reshape(batch, seq, D_MODEL), y_s.reshape(n_seq, t_new, D_MODEL),
            k.reshape(kv_p), v.reshape(kv_p), hist_p[None],
            k_s.reshape(kv_s), v_s.reshape(kv_s), hist_s[None])
```

```python
import functools

import jax
import jax.numpy as jnp
import numpy as np
from jax import lax
from jax.experimental import pallas as pl
from jax.experimental.pallas import tpu as pltpu

D_MODEL = 2048
HEAD_DIM = 128
N_HEADS = 8
ATTN_WIDTH = N_HEADS * HEAD_DIM
POOL_WIDTH = D_MODEL - ATTN_WIDTH
IN_WIDTH = 3 * ATTN_WIDTH + POOL_WIDTH
MOBA_BLOCK = 256
MOBA_TOPK = 3
PAGE_SIZE = 128
PAGES_PER_BLOCK = MOBA_BLOCK // PAGE_SIZE
POOL_WINDOWS = (2, 4, 8, 16)
POOL_GROUP_W = POOL_WIDTH // len(POOL_WINDOWS)
POOL_HIST = max(POOL_WINDOWS) - 1
HIST_PAD = 16
RMS_EPS = 1e-6
NEG = -1e30
ATTN_SCALE = HEAD_DIM ** -0.5

V7X_VMEM_BYTES = 64 * 1024 * 1024
VMEM_LIMIT_BYTES = V7X_VMEM_BYTES - 8 * 1024 * 1024

F32 = jnp.float32
BF16 = jnp.bfloat16
_NT = (((1,), (1,)), ((), ()))


def _rmsnorm(x, g):
    y = x * lax.rsqrt(jnp.mean(x * x, axis=-1, keepdims=True) + RMS_EPS)
    return y * g


def _params(*semantics):
    return pltpu.CompilerParams(dimension_semantics=semantics, vmem_limit_bytes=VMEM_LIMIT_BYTES)


def _proj_prompt_kernel(x_ref, g_ref, w_ref, k_ref, v_ref, u_ref, qh_ref, kbh_ref, vbh_ref, km_ref, *, tm):
    h = _rmsnorm(x_ref[...], g_ref[...]).astype(BF16)

    def seg(s):
        return jnp.dot(h, w_ref[:, s * ATTN_WIDTH:(s + 1) * ATTN_WIDTH], preferred_element_type=F32)

    zq = seg(0)
    for hd in range(N_HEADS):
        qh_ref[hd] = zq[:, hd * HEAD_DIM:(hd + 1) * HEAD_DIM]
    zk = seg(1)
    k_ref[...] = zk
    for hd in range(N_HEADS):
        kbh_ref[hd] = zk[:, hd * HEAD_DIM:(hd + 1) * HEAD_DIM].astype(BF16)
    for blk in range(tm // MOBA_BLOCK):
        rows = zk[blk * MOBA_BLOCK:(blk + 1) * MOBA_BLOCK]
        km_ref[blk] = jnp.sum(rows, axis=0, keepdims=True) * (1.0 / MOBA_BLOCK)
    zv = seg(2)
    v_ref[...] = zv
    for hd in range(N_HEADS):
        vbh_ref[hd] = zv[:, hd * HEAD_DIM:(hd + 1) * HEAD_DIM].astype(BF16)
    u_ref[...] = seg(3)


def _proj_prompt(x, g, w_bf, tm=256):
    m = x.shape[0]
    row = lambda i: (i, 0)
    f32_out = jax.ShapeDtypeStruct((m, ATTN_WIDTH), F32)
    km_out = jax.ShapeDtypeStruct((m // MOBA_BLOCK, 1, ATTN_WIDTH), F32)
    blk = pl.BlockSpec((tm, ATTN_WIDTH), row)
    hm_blk = pl.BlockSpec((N_HEADS, tm, HEAD_DIM), lambda i: (0, i, 0))
    return pl.pallas_call(
        functools.partial(_proj_prompt_kernel, tm=tm),
        grid=(m // tm,),
        in_specs=[
            pl.BlockSpec((tm, D_MODEL), row),
            pl.BlockSpec((1, D_MODEL), lambda i: (0, 0)),
            pl.BlockSpec((D_MODEL, IN_WIDTH), lambda i: (0, 0), pipeline_mode=pl.Buffered(1)),
        ],
        out_specs=[blk, blk, blk, hm_blk, hm_blk, hm_blk,
                   pl.BlockSpec((tm // MOBA_BLOCK, 1, ATTN_WIDTH), lambda i: (i, 0, 0))],
        out_shape=[f32_out, f32_out, f32_out,
                   jax.ShapeDtypeStruct((N_HEADS, m, HEAD_DIM), F32),
                   jax.ShapeDtypeStruct((N_HEADS, m, HEAD_DIM), BF16),
                   jax.ShapeDtypeStruct((N_HEADS, m, HEAD_DIM), BF16), km_out],
        compiler_params=_params("arbitrary"),
        name="proj_prompt",
    )(x, g, w_bf)


def _proj_sample_kernel(x_ref, g_ref, w_ref, k_ref, v_ref, u_ref, qh_ref, kh_ref, vh_ref):
    seg = pl.program_id(0)
    h = _rmsnorm(x_ref[...], g_ref[...]).astype(BF16)
    z = jnp.dot(h, w_ref[...], preferred_element_type=F32)

    def head_major(o_ref):
        for hd in range(N_HEADS):
            o_ref[hd] = z[:, hd * HEAD_DIM:(hd + 1) * HEAD_DIM]

    @pl.when(seg == 0)
    def _():
        head_major(qh_ref)

    @pl.when(seg == 1)
    def _():
        k_ref[...] = z
        head_major(kh_ref)

    @pl.when(seg == 2)
    def _():
        v_ref[...] = z
        head_major(vh_ref)

    @pl.when(seg == 3)
    def _():
        u_ref[...] = z


def _proj_sample(x, g, w_bf):
    m = x.shape[0]
    f32_out = jax.ShapeDtypeStruct((m, ATTN_WIDTH), F32)
    hm_out = jax.ShapeDtypeStruct((N_HEADS, m, HEAD_DIM), F32)
    whole2 = pl.BlockSpec((m, ATTN_WIDTH), lambda s: (0, 0))
    whole3 = pl.BlockSpec((N_HEADS, m, HEAD_DIM), lambda s: (0, 0, 0))
    return pl.pallas_call(
        _proj_sample_kernel,
        grid=(IN_WIDTH // ATTN_WIDTH,),
        in_specs=[
            pl.BlockSpec((m, D_MODEL), lambda s: (0, 0)),
            pl.BlockSpec((1, D_MODEL), lambda s: (0, 0)),
            pl.BlockSpec((D_MODEL, ATTN_WIDTH), lambda s: (0, s)),
        ],
        out_specs=[whole2, whole2, whole2, whole3, whole3, whole3],
        out_shape=[f32_out, f32_out, f32_out, hm_out, hm_out, hm_out],
        compiler_params=_params("arbitrary"),
        name="proj_sample",
    )(x, g, w_bf)


def _alibi_slopes():
    return np.exp2(-8.0 * np.arange(1, N_HEADS + 1, dtype=np.float32) / N_HEADS)


def _alibi_key_table(seq):
    nb = seq // MOBA_BLOCK
    t = np.arange(seq)
    blk, local = t // MOBA_BLOCK, t % MOBA_BLOCK
    table = np.zeros((N_HEADS, seq, HEAD_DIM), np.float32)
    table[:, t, blk] = 1.0
    table[:, :, nb] = _alibi_slopes()[:, None] * local[None, :]
    table[:, :, nb + 1] = _alibi_slopes()[:, None] * (blk * MOBA_BLOCK)[None, :]
    return jnp.asarray(table.astype(BF16))


def _split_bf16(x):
    hi = x.astype(BF16)
    return hi, (x - hi.astype(F32)).astype(BF16)


def _topk_block_bias_t(scores_t, n_past):
    nb = scores_t.shape[0]
    n_iota = lax.broadcasted_iota(jnp.int32, scores_t.shape, 0)
    past = n_iota < n_past
    s = jnp.where(past, scores_t, NEG)
    rank = jnp.zeros(scores_t.shape, jnp.int32)
    for m in range(nb):
        sm = s[m:m + 1, :]
        tie_lower = jnp.logical_and(sm == s, n_iota > m)
        rank = rank + jnp.where(sm > s, 1, jnp.where(tie_lower, 1, 0))
    keep = jnp.logical_or(jnp.logical_and(past, rank < MOBA_TOPK), n_iota == n_past)
    return jnp.where(keep, 0.0, NEG)


def _attn_prompt_kernel(q_ref, kb_ref, vb_ref, km_ref, kx_ref, *refs, nb, n_cast):
    w_refs, o_ref, wbf_refs = refs[:n_cast], refs[n_cast], refs[n_cast + 1:2 * n_cast + 1]
    qa_ref, ka_ref, s_ref, p_ref, acc_ref = refs[2 * n_cast + 1:]
    for w_ref, wbf_ref in zip(w_refs, wbf_refs):
        wbf_ref[...] = w_ref[...].astype(BF16)
    seq = nb * MOBA_BLOCK
    km_hl = jnp.concatenate(_split_bf16(km_ref[...]), axis=0)
    shape = (MOBA_BLOCK, MOBA_BLOCK)
    causal = lax.broadcasted_iota(jnp.int32, shape, 0) >= lax.broadcasted_iota(jnp.int32, shape, 1)

    ka_ref[:, 0:HEAD_DIM] = kb_ref[...]
    ka_ref[:, HEAD_DIM:] = kx_ref[...]
    ones_rows = jnp.where(lax.broadcasted_iota(jnp.int32, (8, MOBA_BLOCK), 0) < 2, 1.0, 0.0)
    zero_rows = jnp.zeros((HEAD_DIM - nb - 8, MOBA_BLOCK), F32)
    for qi in range(nb):
        rows = slice(qi * MOBA_BLOCK, (qi + 1) * MOBA_BLOCK)
        q = q_ref[rows, :]
        q_hi, q_lo = _split_bf16(q)
        sel_hl = lax.dot_general(km_hl, q_hi, _NT, preferred_element_type=F32)
        sel_t = (sel_hl[0:nb] + sel_hl[nb:2 * nb]
                 + lax.dot_general(km_hl[0:nb], q_lo, _NT, preferred_element_type=F32))
        bias_t = _topk_block_bias_t(sel_t, qi)
        qa_ref[rows, 0:HEAD_DIM] = (q * ATTN_SCALE).astype(BF16)
        qa_ref[rows, HEAD_DIM:] = jnp.concatenate([bias_t, ones_rows, zero_rows], axis=0).T.astype(BF16)

    for n in range(nb):
        r0 = n * MOBA_BLOCK
        s = lax.dot_general(qa_ref[r0:seq, :], ka_ref[r0:r0 + MOBA_BLOCK, :], _NT, preferred_element_type=F32)
        s_ref[n, r0:r0 + MOBA_BLOCK, :] = jnp.where(causal, s[0:MOBA_BLOCK], NEG)
        if r0 + MOBA_BLOCK < seq:
            s_ref[n, r0 + MOBA_BLOCK:seq, :] = s[MOBA_BLOCK:]

    inv_l = []
    for qi in range(nb):
        rows = slice(qi * MOBA_BLOCK, (qi + 1) * MOBA_BLOCK)
        m = jnp.max(s_ref[0, rows, :], axis=-1, keepdims=True)
        for n in range(1, qi + 1):
            m = jnp.maximum(m, jnp.max(s_ref[n, rows, :], axis=-1, keepdims=True))
        l = jnp.zeros((MOBA_BLOCK, 1), F32)
        for n in range(qi + 1):
            p = jnp.exp(s_ref[n, rows, :] - m)
            l = l + jnp.sum(p, axis=-1, keepdims=True)
            p_ref[n, rows, :] = p.astype(BF16)
        inv_l.append(1.0 / l)

    for n in range(nb):
        r0 = n * MOBA_BLOCK
        pv = jnp.dot(p_ref[n, r0:seq, :], vb_ref[r0:r0 + MOBA_BLOCK, :], preferred_element_type=F32)
        if n == 0:
            acc_ref[...] = pv
        else:
            acc_ref[r0:seq, :] += pv
    for qi in range(nb):
        rows = slice(qi * MOBA_BLOCK, (qi + 1) * MOBA_BLOCK)
        o_ref[rows, :] = (acc_ref[rows, :] * inv_l[qi]).astype(o_ref.dtype)


def _attn_prompt(qh, kbh, vbh, km, kx, batch, seq, cast_weights):
    nb = seq // MOBA_BLOCK
    assert nb == 8, "bias rows fill one sublane tile"
    steps = batch * N_HEADS
    slab = pl.BlockSpec((None, seq, HEAD_DIM), lambda b, h: (h, b, 0))
    chunk_specs = []
    for w in cast_weights:
        assert w.shape[0] % (16 * steps) == 0, "bf16 row chunks must be whole (16, 128) tiles"
        chunk_specs.append(pl.BlockSpec((w.shape[0] // steps, w.shape[1]), lambda b, h: (b * N_HEADS + h, 0)))
    res = pl.pallas_call(
        functools.partial(_attn_prompt_kernel, nb=nb, n_cast=len(cast_weights)),
        grid=(batch, N_HEADS),
        in_specs=[
            slab, slab, slab,
            pl.BlockSpec((None, nb, HEAD_DIM), lambda b, h: (b, 0, h)),
            pl.BlockSpec((None, seq, HEAD_DIM), lambda b, h: (h, 0, 0)),
        ] + chunk_specs,
        out_specs=[slab] + chunk_specs,
        out_shape=[jax.ShapeDtypeStruct((N_HEADS, batch * seq, HEAD_DIM), BF16)]
        + [jax.ShapeDtypeStruct(w.shape, BF16) for w in cast_weights],
        scratch_shapes=[
            pltpu.VMEM((seq, 2 * HEAD_DIM), BF16),
            pltpu.VMEM((seq, 2 * HEAD_DIM), BF16),
            pltpu.VMEM((nb, seq, MOBA_BLOCK), F32),
            pltpu.VMEM((nb, seq, MOBA_BLOCK), BF16),
            pltpu.VMEM((seq, HEAD_DIM), F32),
        ],
        compiler_params=_params("arbitrary", "arbitrary"),
        name="attn_prompt",
    )(qh, kbh, vbh, km, kx, *cast_weights)
    return res[0], res[1:]


def _pool_diff(ext_ref, r0, n, pos, d_ref, d0):
    for g, w in enumerate(POOL_WINDOWS):
        ls = slice(g * POOL_GROUP_W, (g + 1) * POOL_GROUP_W)
        cur = ext_ref[pl.ds(r0, n), ls]
        win = cur
        for j in range(1, w):
            win = win + ext_ref[pl.ds(r0 - j, n), ls]
        cnt = jnp.minimum(pos + 1, w).astype(F32)
        d_ref[pl.ds(d0, n), ls] = win / cnt - cur


def _mix_out(a_bf, d_ref, wp_ref, ps_ref, wo_ref):
    out = jnp.dot(a_bf, wo_ref[0:ATTN_WIDTH, :], preferred_element_type=F32)
    ys = []
    for g in range(len(POOL_WINDOWS)):
        ls = slice(g * POOL_GROUP_W, (g + 1) * POOL_GROUP_W)
        y = jnp.dot(d_ref[:, ls].astype(BF16), wp_ref[g], preferred_element_type=F32)
        ys.append((y * ps_ref[:, ls]).astype(BF16))
    return out + jnp.dot(jnp.concatenate(ys, axis=-1), wo_ref[ATTN_WIDTH:, :], preferred_element_type=F32)


POOL_PAD = HIST_PAD + 8


def _pool_diff_tile(ext_ref, t1_ref, t2_ref, tm, pos, d_ref):
    lo = 8
    n_all = POOL_PAD - lo + tm
    t1_ref[0:lo, :] = jnp.zeros((lo, POOL_GROUP_W), F32)
    t2_ref[0:lo, :] = jnp.zeros((lo, POOL_GROUP_W), F32)
    for g, w in enumerate(POOL_WINDOWS):
        ls = slice(g * POOL_GROUP_W, (g + 1) * POOL_GROUP_W)
        src, lanes, k, stage = ext_ref, ls, 1, 0
        while 2 * k < w:
            dst = (t1_ref, t2_ref)[stage % 2]
            dst[pl.ds(lo, n_all), :] = src[pl.ds(lo, n_all), lanes] + src[pl.ds(lo - k, n_all), lanes]
            src, lanes, k, stage = dst, slice(None), 2 * k, stage + 1
        win = src[pl.ds(POOL_PAD, tm), lanes] + src[pl.ds(POOL_PAD - k, tm), lanes]
        cnt = jnp.minimum(pos + 1, w).astype(F32)
        d_ref[:, ls] = win / cnt - ext_ref[pl.ds(POOL_PAD, tm), ls]


def _mid_prompt_kernel(a_ref, u_ref, uprev_ref, x_ref, wp_ref, ps_ref, wo_ref, g2_ref,
                       x1_ref, h2_ref, hist_ref, ext_ref, d_ref, t1_ref, t2_ref, *, tm, tiles_per_seq):
    t_in_seq = lax.rem(pl.program_id(0), tiles_per_seq)
    first = POOL_PAD - HIST_PAD
    ext_ref[0:first, :] = jnp.zeros((first, POOL_WIDTH), F32)
    ext_ref[first:POOL_PAD, :] = jnp.where(t_in_seq == 0, 0.0, uprev_ref[...])
    ext_ref[POOL_PAD:POOL_PAD + tm, :] = u_ref[...]
    pos = t_in_seq * tm + lax.broadcasted_iota(jnp.int32, (tm, 1), 0)
    _pool_diff_tile(ext_ref, t1_ref, t2_ref, tm, pos, d_ref)
    a = jnp.concatenate([a_ref[hd] for hd in range(N_HEADS)], axis=-1)
    x1 = x_ref[...] + _mix_out(a, d_ref, wp_ref, ps_ref, wo_ref)
    x1_ref[...] = x1
    h2_ref[...] = _rmsnorm(x1, g2_ref[...]).astype(BF16)

    @pl.when(t_in_seq == tiles_per_seq - 1)
    def _():
        hist_ref[...] = ext_ref[pl.ds(POOL_PAD + tm - POOL_HIST, POOL_HIST), :]


def _mid_prompt(a, u, x, wp_bf, ps, wo_bf, g2, batch, seq, tm=256):
    m = x.shape[0]
    tiles_per_seq = seq // tm
    row = lambda i: (i, 0)
    const2 = lambda i: (0, 0)
    return pl.pallas_call(
        functools.partial(_mid_prompt_kernel, tm=tm, tiles_per_seq=tiles_per_seq),
        grid=(m // tm,),
        in_specs=[
            pl.BlockSpec((N_HEADS, tm, HEAD_DIM), lambda i: (0, i, 0)),
            pl.BlockSpec((tm, POOL_WIDTH), row),
            pl.BlockSpec((HIST_PAD, POOL_WIDTH), lambda i: (jnp.maximum(i * (tm // HIST_PAD) - 1, 0), 0)),
            pl.BlockSpec((tm, D_MODEL), row),
            pl.BlockSpec(wp_bf.shape, lambda i: (0, 0, 0), pipeline_mode=pl.Buffered(1)),
            pl.BlockSpec((1, POOL_WIDTH), const2),
            pl.BlockSpec((D_MODEL, D_MODEL), const2, pipeline_mode=pl.Buffered(1)),
            pl.BlockSpec((1, D_MODEL), const2),
        ],
        out_specs=[
            pl.BlockSpec((tm, D_MODEL), row),
            pl.BlockSpec((tm, D_MODEL), row),
            pl.BlockSpec((None, POOL_HIST, POOL_WIDTH), lambda i: (i // tiles_per_seq, 0, 0)),
        ],
        out_shape=[
            jax.ShapeDtypeStruct((m, D_MODEL), F32),
            jax.ShapeDtypeStruct((m, D_MODEL), BF16),
            jax.ShapeDtypeStruct((batch, POOL_HIST, POOL_WIDTH), F32),
        ],
        scratch_shapes=[pltpu.VMEM((POOL_PAD + tm, POOL_WIDTH), F32), pltpu.VMEM((tm, POOL_WIDTH), F32),
                        pltpu.VMEM((POOL_PAD + tm, POOL_GROUP_W), F32), pltpu.VMEM((POOL_PAD + tm, POOL_GROUP_W), F32)],
        compiler_params=_params("arbitrary"),
        name="mid_prompt",
    )(a, u, u, x, wp_bf, ps, wo_bf, g2)


def _mid_sample_kernel(a_ref, u_ref, hist_ref, x_ref, wp_ref, ps_ref, wo_ref, g2_ref,
                       x1_ref, h2_ref, newhist_ref, ext_ref, d_ref, *, n_seq, t_new, start_pos):
    stride = POOL_HIST + t_new
    stride += -stride % 8
    pos = start_pos + lax.broadcasted_iota(jnp.int32, (t_new, 1), 0)
    for b in range(n_seq):
        base = b * stride
        ext_ref[base:base + POOL_HIST, :] = hist_ref[b]
        ext_ref[base + POOL_HIST:base + POOL_HIST + t_new, :] = u_ref[b * t_new:(b + 1) * t_new, :]
        _pool_diff(ext_ref, base + POOL_HIST, t_new, pos, d_ref, b * t_new)
        newhist_ref[b] = ext_ref[pl.ds(base + t_new, POOL_HIST), :]
    x1 = x_ref[...] + _mix_out(a_ref[...].astype(BF16), d_ref, wp_ref, ps_ref, wo_ref)
    x1_ref[...] = x1
    h2_ref[...] = _rmsnorm(x1, g2_ref[...]).astype(BF16)


def _mid_sample(a, u, hist, x, wp_bf, ps, wo_bf, g2, n_seq, t_new, start_pos):
    m = x.shape[0]
    stride = POOL_HIST + t_new
    stride += -stride % 8
    return pl.pallas_call(
        functools.partial(_mid_sample_kernel, n_seq=n_seq, t_new=t_new, start_pos=start_pos),
        out_shape=[
            jax.ShapeDtypeStruct((m, D_MODEL), F32),
            jax.ShapeDtypeStruct((m, D_MODEL), BF16),
            jax.ShapeDtypeStruct((n_seq, POOL_HIST, POOL_WIDTH), F32),
        ],
        scratch_shapes=[pltpu.VMEM((n_seq * stride, POOL_WIDTH), F32), pltpu.VMEM((m, POOL_WIDTH), F32)],
        compiler_params=pltpu.CompilerParams(vmem_limit_bytes=VMEM_LIMIT_BYTES),
        name="mid_sample",
    )(a, u, hist, x, wp_bf, ps, wo_bf, g2)


def _store_block_mean(o_ref, jb, tot):
    mean = tot * (1.0 / MOBA_BLOCK)
    for hd in range(N_HEADS):
        o_ref[jb:jb + 1, hd * HEAD_DIM:(hd + 1) * HEAD_DIM] = mean[hd:hd + 1, :]


def _ffn_kernel(*refs, n_pages):
    if n_pages:
        refs = refs[1:]
    h2_ref, wg_ref, wu_ref, wd_ref, x1_ref, gf_ref = refs[:6]
    page_refs = refs[6:6 + n_pages]
    y_ref = refs[6 + n_pages]
    km_ref = refs[7 + n_pages] if n_pages else None
    j = pl.program_id(1)

    @pl.when(j == 0)
    def _():
        y_ref[...] = x1_ref[...]

    n_slots = 6
    sums = {}

    def side_job(slot):
        for r in range(slot, n_pages, n_slots):
            sums[r] = jnp.sum(page_refs[r][...], axis=0)
            jb, pos = divmod(r, PAGES_PER_BLOCK)
            if pos == PAGES_PER_BLOCK - 1:
                tot = sums[jb * PAGES_PER_BLOCK]
                for rr in range(jb * PAGES_PER_BLOCK + 1, r + 1):
                    tot = tot + sums[rr]
                _store_block_mean(km_ref, jb, tot)

    h2 = h2_ref[...]
    tf = wg_ref.shape[1]
    gate, up = [], []
    for c, cols in enumerate((slice(0, tf // 2), slice(tf // 2, tf))):
        gate.append(jnp.dot(h2, wg_ref[:, cols], preferred_element_type=F32))
        side_job(c)
    for c, cols in enumerate((slice(0, tf // 2), slice(tf // 2, tf))):
        up.append(jnp.dot(h2, wu_ref[:, cols], preferred_element_type=F32))
        side_job(2 + c)
    gate = jnp.concatenate(gate, axis=-1)
    act = (gate * jax.nn.sigmoid(gate) * jnp.concatenate(up, axis=-1)).astype(BF16)
    for c, cols in enumerate((slice(0, D_MODEL // 2), slice(D_MODEL // 2, D_MODEL))):
        y_ref[:, cols] += jnp.dot(act, wd_ref[:, cols], preferred_element_type=F32)
        side_job(4 + c)

    @pl.when(j == pl.num_programs(1) - 1)
    def _():
        y_ref[...] = _rmsnorm(y_ref[...], gf_ref[...])


def _ffn(h2, wg_bf, wu_bf, wd_bf, x1, gf, tm, tf=512, paged_k=None):
    m = h2.shape[0]
    d_ff = wg_bf.shape[1]
    ni, nj = m // tm, d_ff // tf
    in_specs = [
        pl.BlockSpec((tm, D_MODEL), lambda i, j, *_: (i, 0)),
        pl.BlockSpec((D_MODEL, tf), lambda i, j, *_: (0, j)),
        pl.BlockSpec((D_MODEL, tf), lambda i, j, *_: (0, j)),
        pl.BlockSpec((tf, D_MODEL), lambda i, j, *_: (j, 0)),
        pl.BlockSpec((tm, D_MODEL), lambda i, j, *_: (i, 0)),
        pl.BlockSpec((1, D_MODEL), lambda i, j, *_: (0, 0)),
    ]
    y_spec = pl.BlockSpec((tm, D_MODEL), lambda i, j, *_: (i, 0))
    y_shape = jax.ShapeDtypeStruct((m, D_MODEL), F32)
    if paged_k is None:
        return pl.pallas_call(
            functools.partial(_ffn_kernel, n_pages=0),
            grid=(ni, nj), in_specs=in_specs, out_specs=y_spec, out_shape=y_shape,
            compiler_params=_params("arbitrary", "arbitrary"), name="ffn",
        )(h2, wg_bf, wu_bf, wd_bf, x1, gf)

    page_table_flat, cache_k = paged_k
    total = page_table_flat.shape[0]
    pps = PAGES_PER_BLOCK * pl.cdiv(total, PAGES_PER_BLOCK * ni * nj)
    n_events = pl.cdiv(total, pps)

    def page_spec(r):
        def index_map(i, j, pt):
            event = jnp.minimum(i * nj + j, n_events - 1)
            return (0, pt[jnp.minimum(event * pps + r, total - 1)], 0, 0, 0)
        return pl.BlockSpec((None, None, PAGE_SIZE, N_HEADS, HEAD_DIM), index_map)

    bps = pps // PAGES_PER_BLOCK
    km_spec = pl.BlockSpec((None, bps, ATTN_WIDTH), lambda i, j, pt: (jnp.minimum(i * nj + j, n_events - 1), 0, 0))
    y, km = pl.pallas_call(
        functools.partial(_ffn_kernel, n_pages=pps),
        grid_spec=pltpu.PrefetchScalarGridSpec(
            num_scalar_prefetch=1, grid=(ni, nj),
            in_specs=in_specs + [page_spec(r) for r in range(pps)],
            out_specs=[y_spec, km_spec]),
        out_shape=[y_shape, jax.ShapeDtypeStruct((n_events, bps, ATTN_WIDTH), F32)],
        compiler_params=_params("arbitrary", "arbitrary"), name="ffn_kmean",
    )(page_table_flat, h2, wg_bf, wu_bf, wd_bf, x1, gf, *([cache_k] * pps))
    return y, km.reshape(n_events * bps, ATTN_WIDTH)[:total // PAGES_PER_BLOCK]


def _select_sample_kernel(qh_ref, km_ref, idx_ref, sc_ref, *, n_seq, t_new, n_past_blocks):
    lanes = sc_ref.shape[1]
    lane = lax.broadcasted_iota(jnp.int32, sc_ref.shape, 1)
    sc_ref[...] = jnp.where(lane == n_past_blocks, NEG, -jnp.inf)
    for b in range(n_seq):
        for hd in range(N_HEADS):
            qbh = qh_ref[hd, b * t_new:(b + 1) * t_new, :]
            kmbh = km_ref[b, :, hd * HEAD_DIM:(hd + 1) * HEAD_DIM]
            r0 = (b * N_HEADS + hd) * t_new
            sc_ref[r0:r0 + t_new, 0:n_past_blocks] = lax.dot_general(
                qbh, kmbh, _NT, precision=lax.Precision.HIGHEST, preferred_element_type=F32)
    s = sc_ref[...]
    for r in range(MOBA_TOPK):
        mx = jnp.max(s, axis=-1, keepdims=True)
        am = jnp.min(jnp.where(s == mx, lane, lanes), axis=-1, keepdims=True)
        idx_ref[:, r:r + 1] = am
        s = jnp.where(lane == am, -jnp.inf, s)


def _select_sample(qh, km, n_seq, t_new):
    n_past_blocks = km.shape[1]
    rows = n_seq * N_HEADS * t_new
    return pl.pallas_call(
        functools.partial(_select_sample_kernel, n_seq=n_seq, t_new=t_new, n_past_blocks=n_past_blocks),
        out_shape=jax.ShapeDtypeStruct((rows, MOBA_TOPK), jnp.int32),
        scratch_shapes=[pltpu.VMEM((rows, 128), F32)],
        compiler_params=pltpu.CompilerParams(vmem_limit_bytes=VMEM_LIMIT_BYTES),
        name="select_sample",
    )(qh, km)


GATHER_DEPTH = 2


def _attn_sample_kernel(pt_ref, idx_ref, slopes_ref, qh_ref, kh_ref, vh_ref, ck_hbm, cv_hbm, o_ref,
                        kbuf, vbuf, sem, *, n_seq, t_new, n_pages):
    n_past_blocks = n_pages // PAGES_PER_BLOCK
    past_len = n_pages * PAGE_SIZE
    n_slots = t_new * MOBA_TOPK
    n_keys = n_slots * MOBA_BLOCK
    n_steps = n_seq * N_HEADS
    n_bufs = GATHER_DEPTH + 1
    b = pl.program_id(0)
    hd = pl.program_id(1)
    step = b * N_HEADS + hd

    def block_of(st, qj):
        return idx_ref[st * n_slots + qj]

    def gather(st, start):
        bb, hh, sl = st // N_HEADS, lax.rem(st, N_HEADS), lax.rem(st, n_bufs)
        for qj in range(n_slots):
            blk = jnp.minimum(block_of(st, qj), n_past_blocks - 1)
            for half in range(PAGES_PER_BLOCK):
                phys = pt_ref[bb * n_pages + blk * PAGES_PER_BLOCK + half]
                rows = pl.ds(qj * MOBA_BLOCK + half * PAGE_SIZE, PAGE_SIZE)
                for src, dst, s in ((ck_hbm, kbuf, 0), (cv_hbm, vbuf, 1)):
                    cp = pltpu.make_async_copy(src.at[0, phys, :, hh, :], dst.at[sl, rows, :], sem.at[s, sl])
                    if start:
                        cp.start()
                    else:
                        cp.wait()

    @pl.when(step == 0)
    def _():
        for d in range(min(GATHER_DEPTH, n_steps)):
            gather(step + d, True)

    @pl.when(step + GATHER_DEPTH < n_steps)
    def _():
        gather(step + GATHER_DEPTH, True)

    gather(step, False)

    buf = lax.rem(step, n_bufs)
    slope = slopes_ref[hd]
    rows = pl.ds(b * t_new, t_new)
    qs = (qh_ref[rows, :] * ATTN_SCALE).astype(BF16)

    s = lax.dot_general(qs, kbuf[buf].astype(BF16), _NT, preferred_element_type=F32)
    row = lax.broadcasted_iota(jnp.int32, (t_new, n_keys), 0)
    col = lax.broadcasted_iota(jnp.int32, (t_new, n_keys), 1)
    shift = jnp.zeros((t_new, n_keys), jnp.int32)
    valid = jnp.zeros((t_new, n_keys), jnp.int32)
    for qj in range(n_slots):
        blk = block_of(step, qj)
        in_slot = jnp.logical_and(col >= qj * MOBA_BLOCK, col < (qj + 1) * MOBA_BLOCK)
        shift = jnp.where(in_slot, blk * MOBA_BLOCK - qj * MOBA_BLOCK, shift)
        valid = jnp.where(in_slot, (blk < n_past_blocks).astype(jnp.int32), valid)
    dist = (past_len + row - (col + shift)).astype(F32)
    q_lo = row * (MOBA_TOPK * MOBA_BLOCK)
    mine = jnp.logical_and(col >= q_lo, col < q_lo + MOBA_TOPK * MOBA_BLOCK)
    s = jnp.where(jnp.logical_and(mine, valid > 0), s - slope * dist, NEG)
    own_rel = (lax.broadcasted_iota(jnp.int32, (t_new, t_new), 0)
               - lax.broadcasted_iota(jnp.int32, (t_new, t_new), 1))
    s_own = lax.dot_general(qs, kh_ref[rows, :].astype(BF16), _NT, preferred_element_type=F32)
    s_own = jnp.where(own_rel >= 0, s_own - slope * own_rel.astype(F32), NEG)

    m = jnp.maximum(jnp.max(s_own, axis=-1, keepdims=True), jnp.max(s, axis=-1, keepdims=True))
    p_own = jnp.exp(s_own - m)
    p = jnp.exp(s - m)
    l = jnp.sum(p_own, axis=-1, keepdims=True) + jnp.sum(p, axis=-1, keepdims=True)
    acc = (jnp.dot(p_own.astype(BF16), vh_ref[rows, :].astype(BF16), preferred_element_type=F32)
           + jnp.dot(p.astype(BF16), vbuf[buf].astype(BF16), preferred_element_type=F32))
    o_ref[...] = acc / l


def _attn_sample(page_table_flat, idx_flat, slopes, qh, kh, vh, cache_k, cache_v, n_seq, t_new, n_pages):
    n_slots = t_new * MOBA_TOPK
    head_blk = pl.BlockSpec((None, n_seq * t_new, HEAD_DIM), lambda b, h, pt, ix, sl: (h, 0, 0))
    grid_spec = pltpu.PrefetchScalarGridSpec(
        num_scalar_prefetch=3,
        grid=(n_seq, N_HEADS),
        in_specs=[head_blk, head_blk, head_blk,
                  pl.BlockSpec(memory_space=pl.ANY), pl.BlockSpec(memory_space=pl.ANY)],
        out_specs=pl.BlockSpec((None, t_new, HEAD_DIM), lambda b, h, pt, ix, sl: (b, 0, h)),
        scratch_shapes=[
            pltpu.VMEM((GATHER_DEPTH + 1, n_slots * MOBA_BLOCK, HEAD_DIM), F32),
            pltpu.VMEM((GATHER_DEPTH + 1, n_slots * MOBA_BLOCK, HEAD_DIM), F32),
            pltpu.SemaphoreType.DMA((2, GATHER_DEPTH + 1)),
        ],
    )
    return pl.pallas_call(
        functools.partial(_attn_sample_kernel, n_seq=n_seq, t_new=t_new, n_pages=n_pages),
        grid_spec=grid_spec,
        out_shape=jax.ShapeDtypeStruct((n_seq, t_new, ATTN_WIDTH), F32),
        compiler_params=_params("arbitrary", "arbitrary"),
        name="attn_sample",
    )(page_table_flat, idx_flat, slopes, qh, kh, vh, cache_k, cache_v)


def kernel(x_prompt, x_sample, cache_k, cache_v, state_pool, page_table, norm1_g, w_in, w_pool, pool_scale,
           w_out, norm2_g, w_gate, w_up, w_down, norm_f_g):
    depth = w_in.shape[0]
    assert depth == 1, "single-layer step"
    batch, seq, _ = x_prompt.shape
    n_seq, t_new, _ = x_sample.shape
    n_pages = page_table.shape[1]
    past_len = n_pages * PAGE_SIZE
    assert seq % MOBA_BLOCK == 0 and past_len % MOBA_BLOCK == 0 and t_new <= MOBA_BLOCK

    slopes = jnp.asarray(_alibi_slopes())
    g1 = norm1_g[0].reshape(1, D_MODEL)
    g2 = norm2_g[0].reshape(1, D_MODEL)
    gf = norm_f_g.reshape(1, D_MODEL)
    ps = pool_scale[0].reshape(1, POOL_WIDTH)
    w_in_bf = w_in[0].astype(BF16)
    w_pool_bf = w_pool[0].astype(BF16)

    xp = x_prompt.reshape(batch * seq, D_MODEL)
    pt_flat = page_table.reshape(-1)
    k, v, u, qh_p, kbh, vbh, km = _proj_prompt(xp, g1, w_in_bf)
    km = km.reshape(batch, seq // MOBA_BLOCK, ATTN_WIDTH)
    a, (w_out_bf, w_gate_bf, w_up_bf, w_down_bf) = _attn_prompt(
        qh_p, kbh, vbh, km, _alibi_key_table(seq), batch, seq,
        (w_out[0], w_gate[0], w_up[0], w_down[0]))
    x1, h2, hist_p = _mid_prompt(a, u, xp, w_pool_bf, ps, w_out_bf, g2, batch, seq)
    y_p, km_s = _ffn(h2, w_gate_bf, w_up_bf, w_down_bf, x1, gf, tm=512, paged_k=(pt_flat, cache_k))

    xs = x_sample.reshape(n_seq * t_new, D_MODEL)
    k_s, v_s, u_s, qh, kh, vh = _proj_sample(xs, g1, w_in_bf)
    km_s = km_s.reshape(n_seq, n_pages // PAGES_PER_BLOCK, ATTN_WIDTH)
    idx = _select_sample(qh, km_s, n_seq, t_new)
    a_s = _attn_sample(pt_flat, idx.reshape(-1), slopes, qh, kh, vh, cache_k, cache_v, n_seq, t_new, n_pages)
    x1_s, h2_s, hist_s = _mid_sample(a_s.reshape(n_seq * t_new, ATTN_WIDTH), u_s, state_pool[0], xs,
                                     w_pool_bf, ps, w_out_bf, g2, n_seq, t_new, past_len)
    y_s = _ffn(h2_s, w_gate_bf, w_up_bf, w_down_bf, x1_s, gf, tm=n_seq * t_new)

    kv_p = (depth, batch, seq, N_HEADS, HEAD_DIM)
    kv_s = (depth, n_seq, t_new, N_HEADS, HEAD_DIM)
    return (y_p.reshape(batch, seq, D_MODEL), y_s.reshape(n_seq, t_new, D_MODEL),
            k.reshape(kv_p), v.reshape(kv_p), hist_p[None],
            k_s.reshape(kv_s), v_s.reshape(kv_s), hist_s[None])
```

```python
import functools

import jax
import jax.numpy as jnp
import numpy as np
from jax import lax
from jax.experimental import pallas as pl
from jax.experimental.pallas import tpu as pltpu

D_MODEL = 2048
HEAD_DIM = 128
N_HEADS = 8
ATTN_WIDTH = N_HEADS * HEAD_DIM
POOL_WIDTH = D_MODEL - ATTN_WIDTH
IN_WIDTH = 3 * ATTN_WIDTH + POOL_WIDTH
MOBA_BLOCK = 256
MOBA_TOPK = 3
PAGE_SIZE = 128
PAGES_PER_BLOCK = MOBA_BLOCK // PAGE_SIZE
POOL_WINDOWS = (2, 4, 8, 16)
POOL_GROUP_W = POOL_WIDTH // len(POOL_WINDOWS)
POOL_HIST = max(POOL_WINDOWS) - 1
LANES = 128
SUBLANES = 8
BF16_SUBLANES = 16
HIST_PAD = 2 * SUBLANES
RMS_EPS = 1e-6
NEG = -1e30
ATTN_SCALE = HEAD_DIM ** -0.5

V7X_VMEM_BYTES = 64 * 1024 * 1024
VMEM_LIMIT_BYTES = V7X_VMEM_BYTES - 8 * 1024 * 1024

F32 = jnp.float32
BF16 = jnp.bfloat16
_NT = (((1,), (1,)), ((), ()))


def _rmsnorm(x, g):
    y = x * lax.rsqrt(jnp.mean(x * x, axis=-1, keepdims=True) + RMS_EPS)
    return y * g


def _params(*semantics):
    return pltpu.CompilerParams(dimension_semantics=semantics, vmem_limit_bytes=VMEM_LIMIT_BYTES)


W_CAST_CHUNKS = 8


def _proj_prompt_kernel(x_ref, g_ref, w_hbm, k_ref, v_ref, u_ref, qh_ref, kbh_ref, vbh_ref, km_ref,
                        w_ref, stage_ref, sem, *, tm):
    @pl.when(pl.program_id(0) == 0)
    def _():
        rows = D_MODEL // W_CAST_CHUNKS

        def chunk_copy(c):
            return pltpu.make_async_copy(w_hbm.at[pl.ds(c * rows, rows), :], stage_ref.at[c % 2], sem.at[c % 2])

        chunk_copy(0).start()
        for c in range(W_CAST_CHUNKS):
            if c + 1 < W_CAST_CHUNKS:
                chunk_copy(c + 1).start()
            chunk_copy(c).wait()
            w_ref[c * rows:(c + 1) * rows, :] = stage_ref[c % 2].astype(BF16)

    h = _rmsnorm(x_ref[...], g_ref[...]).astype(BF16)

    def seg(s):
        return jnp.dot(h, w_ref[:, s * ATTN_WIDTH:(s + 1) * ATTN_WIDTH], preferred_element_type=F32)

    zq = seg(0)
    for hd in range(N_HEADS):
        qh_ref[hd] = zq[:, hd * HEAD_DIM:(hd + 1) * HEAD_DIM]
    zk = seg(1)
    k_ref[...] = zk
    for hd in range(N_HEADS):
        kbh_ref[hd] = zk[:, hd * HEAD_DIM:(hd + 1) * HEAD_DIM].astype(BF16)
    for blk in range(tm // MOBA_BLOCK):
        rows = zk[blk * MOBA_BLOCK:(blk + 1) * MOBA_BLOCK]
        km_ref[blk] = jnp.sum(rows, axis=0, keepdims=True) * (1.0 / MOBA_BLOCK)
    zv = seg(2)
    v_ref[...] = zv
    for hd in range(N_HEADS):
        vbh_ref[hd] = zv[:, hd * HEAD_DIM:(hd + 1) * HEAD_DIM].astype(BF16)
    u_ref[...] = seg(3)


def _proj_prompt(x, g, w, tm=256):
    m = x.shape[0]
    row = lambda i: (i, 0)
    f32_out = jax.ShapeDtypeStruct((m, ATTN_WIDTH), F32)
    km_out = jax.ShapeDtypeStruct((m // MOBA_BLOCK, 1, ATTN_WIDTH), F32)
    blk = pl.BlockSpec((tm, ATTN_WIDTH), row)
    hm_blk = pl.BlockSpec((N_HEADS, tm, HEAD_DIM), lambda i: (0, i, 0))
    return pl.pallas_call(
        functools.partial(_proj_prompt_kernel, tm=tm),
        grid=(m // tm,),
        in_specs=[
            pl.BlockSpec((tm, D_MODEL), row),
            pl.BlockSpec((1, D_MODEL), lambda i: (0, 0)),
            pl.BlockSpec(memory_space=pl.ANY),
        ],
        out_specs=[blk, blk, blk, hm_blk, hm_blk, hm_blk,
                   pl.BlockSpec((tm // MOBA_BLOCK, 1, ATTN_WIDTH), lambda i: (i, 0, 0))],
        out_shape=[f32_out, f32_out, f32_out,
                   jax.ShapeDtypeStruct((N_HEADS, m, HEAD_DIM), F32),
                   jax.ShapeDtypeStruct((N_HEADS, m, HEAD_DIM), BF16),
                   jax.ShapeDtypeStruct((N_HEADS, m, HEAD_DIM), BF16), km_out],
        scratch_shapes=[
            pltpu.VMEM((D_MODEL, IN_WIDTH), BF16),
            pltpu.VMEM((2, D_MODEL // W_CAST_CHUNKS, IN_WIDTH), F32),
            pltpu.SemaphoreType.DMA((2,)),
        ],
        compiler_params=_params("arbitrary"),
        name="proj_prompt",
    )(x, g, w)


def _proj_sample_kernel(x_ref, g_ref, w_ref, k_ref, v_ref, u_ref, qh_ref, kh_ref, vh_ref):
    seg = pl.program_id(0)
    h = _rmsnorm(x_ref[...], g_ref[...]).astype(BF16)
    z = jnp.dot(h, w_ref[...].astype(BF16), preferred_element_type=F32)

    def head_major(o_ref):
        for hd in range(N_HEADS):
            o_ref[hd] = z[:, hd * HEAD_DIM:(hd + 1) * HEAD_DIM]

    @pl.when(seg == 0)
    def _():
        head_major(qh_ref)

    @pl.when(seg == 1)
    def _():
        k_ref[...] = z
        head_major(kh_ref)

    @pl.when(seg == 2)
    def _():
        v_ref[...] = z
        head_major(vh_ref)

    @pl.when(seg == 3)
    def _():
        u_ref[...] = z


def _proj_sample(x, g, w):
    m = x.shape[0]
    f32_out = jax.ShapeDtypeStruct((m, ATTN_WIDTH), F32)
    hm_out = jax.ShapeDtypeStruct((N_HEADS, m, HEAD_DIM), F32)
    whole2 = pl.BlockSpec((m, ATTN_WIDTH), lambda s: (0, 0))
    whole3 = pl.BlockSpec((N_HEADS, m, HEAD_DIM), lambda s: (0, 0, 0))
    return pl.pallas_call(
        _proj_sample_kernel,
        grid=(IN_WIDTH // ATTN_WIDTH,),
        in_specs=[
            pl.BlockSpec((m, D_MODEL), lambda s: (0, 0)),
            pl.BlockSpec((1, D_MODEL), lambda s: (0, 0)),
            pl.BlockSpec((D_MODEL, ATTN_WIDTH), lambda s: (0, s)),
        ],
        out_specs=[whole2, whole2, whole2, whole3, whole3, whole3],
        out_shape=[f32_out, f32_out, f32_out, hm_out, hm_out, hm_out],
        compiler_params=_params("arbitrary"),
        name="proj_sample",
    )(x, g, w)


def _alibi_slopes():
    return np.exp2(-8.0 * np.arange(1, N_HEADS + 1, dtype=np.float32) / N_HEADS)


def _alibi_key_table(seq):
    nb = seq // MOBA_BLOCK
    t = np.arange(seq)
    blk, local = t // MOBA_BLOCK, t % MOBA_BLOCK
    table = np.zeros((N_HEADS, seq, HEAD_DIM), np.float32)
    table[:, t, blk] = 1.0
    table[:, :, nb] = _alibi_slopes()[:, None] * local[None, :]
    table[:, :, nb + 1] = _alibi_slopes()[:, None] * (blk * MOBA_BLOCK)[None, :]
    return jnp.asarray(table.astype(BF16))


def _split_bf16(x):
    hi = x.astype(BF16)
    return hi, (x - hi.astype(F32)).astype(BF16)


def _topk_block_bias_t(scores_t, n_past):
    nb = scores_t.shape[0]
    n_iota = lax.broadcasted_iota(jnp.int32, scores_t.shape, 0)
    past = n_iota < n_past
    s = jnp.where(past, scores_t, NEG)
    rank = jnp.zeros(scores_t.shape, jnp.int32)
    for m in range(nb):
        sm = s[m:m + 1, :]
        tie_lower = jnp.logical_and(sm == s, n_iota > m)
        rank = rank + jnp.where(sm > s, 1, jnp.where(tie_lower, 1, 0))
    keep = jnp.logical_or(jnp.logical_and(past, rank < MOBA_TOPK), n_iota == n_past)
    return jnp.where(keep, 0.0, NEG)


def _attn_prompt_kernel(q_ref, kb_ref, vb_ref, km_ref, kx_ref, *refs, nb, n_cast):
    w_refs, o_ref, wbf_refs = refs[:n_cast], refs[n_cast], refs[n_cast + 1:2 * n_cast + 1]
    qa_ref, ka_ref, s_ref, p_ref, acc_ref = refs[2 * n_cast + 1:]
    for w_ref, wbf_ref in zip(w_refs, wbf_refs):
        wbf_ref[...] = w_ref[...].astype(BF16)
    seq = nb * MOBA_BLOCK
    km_hl = jnp.concatenate(_split_bf16(km_ref[...]), axis=0)
    shape = (MOBA_BLOCK, MOBA_BLOCK)
    causal = lax.broadcasted_iota(jnp.int32, shape, 0) >= lax.broadcasted_iota(jnp.int32, shape, 1)

    ka_ref[:, 0:HEAD_DIM] = kb_ref[...]
    ka_ref[:, HEAD_DIM:] = kx_ref[...]
    ones_rows = jnp.where(lax.broadcasted_iota(jnp.int32, (SUBLANES, MOBA_BLOCK), 0) < 2, 1.0, 0.0)
    zero_rows = jnp.zeros((HEAD_DIM - nb - SUBLANES, MOBA_BLOCK), F32)
    for qi in range(nb):
        rows = slice(qi * MOBA_BLOCK, (qi + 1) * MOBA_BLOCK)
        q = q_ref[rows, :]
        q_hi, q_lo = _split_bf16(q)
        sel_hl = lax.dot_general(km_hl, q_hi, _NT, preferred_element_type=F32)
        sel_t = (sel_hl[0:nb] + sel_hl[nb:2 * nb]
                 + lax.dot_general(km_hl[0:nb], q_lo, _NT, preferred_element_type=F32))
        bias_t = _topk_block_bias_t(sel_t, qi)
        qa_ref[rows, 0:HEAD_DIM] = (q * ATTN_SCALE).astype(BF16)
        qa_ref[rows, HEAD_DIM:] = jnp.concatenate([bias_t, ones_rows, zero_rows], axis=0).T.astype(BF16)

    for n in range(nb):
        r0 = n * MOBA_BLOCK
        s = lax.dot_general(qa_ref[r0:seq, :], ka_ref[r0:r0 + MOBA_BLOCK, :], _NT, preferred_element_type=F32)
        s_ref[n, r0:r0 + MOBA_BLOCK, :] = jnp.where(causal, s[0:MOBA_BLOCK], NEG)
        if r0 + MOBA_BLOCK < seq:
            s_ref[n, r0 + MOBA_BLOCK:seq, :] = s[MOBA_BLOCK:]

    inv_l = []
    for qi in range(nb):
        rows = slice(qi * MOBA_BLOCK, (qi + 1) * MOBA_BLOCK)
        m = jnp.max(s_ref[0, rows, :], axis=-1, keepdims=True)
        for n in range(1, qi + 1):
            m = jnp.maximum(m, jnp.max(s_ref[n, rows, :], axis=-1, keepdims=True))
        l = jnp.zeros((MOBA_BLOCK, 1), F32)
        for n in range(qi + 1):
            p = jnp.exp(s_ref[n, rows, :] - m)
            l = l + jnp.sum(p, axis=-1, keepdims=True)
            p_ref[n, rows, :] = p.astype(BF16)
        inv_l.append(1.0 / l)

    for n in range(nb):
        r0 = n * MOBA_BLOCK
        pv = jnp.dot(p_ref[n, r0:seq, :], vb_ref[r0:r0 + MOBA_BLOCK, :], preferred_element_type=F32)
        if n == 0:
            acc_ref[...] = pv
        else:
            acc_ref[r0:seq, :] += pv
    for qi in range(nb):
        rows = slice(qi * MOBA_BLOCK, (qi + 1) * MOBA_BLOCK)
        o_ref[rows, :] = (acc_ref[rows, :] * inv_l[qi]).astype(o_ref.dtype)


def _attn_prompt(qh, kbh, vbh, km, kx, batch, seq, cast_weights):
    nb = seq // MOBA_BLOCK
    assert nb == SUBLANES, "bias rows fill one sublane tile"
    steps = batch * N_HEADS
    slab = pl.BlockSpec((None, seq, HEAD_DIM), lambda b, h: (h, b, 0))
    chunk_specs = []
    for w in cast_weights:
        assert w.shape[0] % (BF16_SUBLANES * steps) == 0, "bf16 row chunks must be whole (16, 128) tiles"
        chunk_specs.append(pl.BlockSpec((w.shape[0] // steps, w.shape[1]), lambda b, h: (b * N_HEADS + h, 0)))
    res = pl.pallas_call(
        functools.partial(_attn_prompt_kernel, nb=nb, n_cast=len(cast_weights)),
        grid=(batch, N_HEADS),
        in_specs=[
            slab, slab, slab,
            pl.BlockSpec((None, nb, HEAD_DIM), lambda b, h: (b, 0, h)),
            pl.BlockSpec((None, seq, HEAD_DIM), lambda b, h: (h, 0, 0)),
        ] + chunk_specs,
        out_specs=[slab] + chunk_specs,
        out_shape=[jax.ShapeDtypeStruct((N_HEADS, batch * seq, HEAD_DIM), BF16)]
        + [jax.ShapeDtypeStruct(w.shape, BF16) for w in cast_weights],
        scratch_shapes=[
            pltpu.VMEM((seq, 2 * HEAD_DIM), BF16),
            pltpu.VMEM((seq, 2 * HEAD_DIM), BF16),
            pltpu.VMEM((nb, seq, MOBA_BLOCK), F32),
            pltpu.VMEM((nb, seq, MOBA_BLOCK), BF16),
            pltpu.VMEM((seq, HEAD_DIM), F32),
        ],
        compiler_params=_params("arbitrary", "arbitrary"),
        name="attn_prompt",
    )(qh, kbh, vbh, km, kx, *cast_weights)
    return res[0], res[1:]


def _pool_diff(ext_ref, r0, n, pos, d_ref, d0):
    for g, w in enumerate(POOL_WINDOWS):
        ls = slice(g * POOL_GROUP_W, (g + 1) * POOL_GROUP_W)
        cur = ext_ref[pl.ds(r0, n), ls]
        win = cur
        for j in range(1, w):
            win = win + ext_ref[pl.ds(r0 - j, n), ls]
        cnt = jnp.minimum(pos + 1, w).astype(F32)
        d_ref[pl.ds(d0, n), ls] = win / cnt - cur


def _mix_out(a_bf, d_ref, wp_ref, ps_ref, wo_ref):
    out = jnp.dot(a_bf, wo_ref[0:ATTN_WIDTH, :], preferred_element_type=F32)
    ys = []
    for g in range(len(POOL_WINDOWS)):
        ls = slice(g * POOL_GROUP_W, (g + 1) * POOL_GROUP_W)
        y = jnp.dot(d_ref[:, ls].astype(BF16), wp_ref[g], preferred_element_type=F32)
        ys.append((y * ps_ref[:, ls]).astype(BF16))
    return out + jnp.dot(jnp.concatenate(ys, axis=-1), wo_ref[ATTN_WIDTH:, :], preferred_element_type=F32)


POOL_PAD = HIST_PAD + SUBLANES


def _pool_diff_tile(ext_ref, t1_ref, t2_ref, tm, pos, d_ref):
    lo = SUBLANES
    n_all = POOL_PAD - lo + tm
    t1_ref[0:lo, :] = jnp.zeros((lo, POOL_GROUP_W), F32)
    t2_ref[0:lo, :] = jnp.zeros((lo, POOL_GROUP_W), F32)
    for g, w in enumerate(POOL_WINDOWS):
        ls = slice(g * POOL_GROUP_W, (g + 1) * POOL_GROUP_W)
        src, lanes, k, stage = ext_ref, ls, 1, 0
        while 2 * k < w:
            dst = (t1_ref, t2_ref)[stage % 2]
            dst[pl.ds(lo, n_all), :] = src[pl.ds(lo, n_all), lanes] + src[pl.ds(lo - k, n_all), lanes]
            src, lanes, k, stage = dst, slice(None), 2 * k, stage + 1
        win = src[pl.ds(POOL_PAD, tm), lanes] + src[pl.ds(POOL_PAD - k, tm), lanes]
        cnt = jnp.minimum(pos + 1, w).astype(F32)
        d_ref[:, ls] = win / cnt - ext_ref[pl.ds(POOL_PAD, tm), ls]


def _mid_prompt_kernel(a_ref, u_ref, uprev_ref, x_ref, wp_ref, ps_ref, wo_ref, g2_ref,
                       x1_ref, h2_ref, hist_ref, ext_ref, d_ref, t1_ref, t2_ref, *, tm, tiles_per_seq):
    t_in_seq = lax.rem(pl.program_id(0), tiles_per_seq)
    first = POOL_PAD - HIST_PAD
    ext_ref[0:first, :] = jnp.zeros((first, POOL_WIDTH), F32)
    ext_ref[first:POOL_PAD, :] = jnp.where(t_in_seq == 0, 0.0, uprev_ref[...])
    ext_ref[POOL_PAD:POOL_PAD + tm, :] = u_ref[...]
    pos = t_in_seq * tm + lax.broadcasted_iota(jnp.int32, (tm, 1), 0)
    _pool_diff_tile(ext_ref, t1_ref, t2_ref, tm, pos, d_ref)
    a = jnp.concatenate([a_ref[hd] for hd in range(N_HEADS)], axis=-1)
    x1 = x_ref[...] + _mix_out(a, d_ref, wp_ref, ps_ref, wo_ref)
    x1_ref[...] = x1
    h2_ref[...] = _rmsnorm(x1, g2_ref[...]).astype(BF16)

    @pl.when(t_in_seq == tiles_per_seq - 1)
    def _():
        hist_ref[...] = ext_ref[pl.ds(POOL_PAD + tm - POOL_HIST, POOL_HIST), :]


def _mid_prompt(a, u, x, wp_bf, ps, wo_bf, g2, batch, seq, tm=256):
    m = x.shape[0]
    tiles_per_seq = seq // tm
    row = lambda i: (i, 0)
    const2 = lambda i: (0, 0)
    return pl.pallas_call(
        functools.partial(_mid_prompt_kernel, tm=tm, tiles_per_seq=tiles_per_seq),
        grid=(m // tm,),
        in_specs=[
            pl.BlockSpec((N_HEADS, tm, HEAD_DIM), lambda i: (0, i, 0)),
            pl.BlockSpec((tm, POOL_WIDTH), row),
            pl.BlockSpec((HIST_PAD, POOL_WIDTH), lambda i: (jnp.maximum(i * (tm // HIST_PAD) - 1, 0), 0)),
            pl.BlockSpec((tm, D_MODEL), row),
            pl.BlockSpec(wp_bf.shape, lambda i: (0, 0, 0), pipeline_mode=pl.Buffered(1)),
            pl.BlockSpec((1, POOL_WIDTH), const2),
            pl.BlockSpec((D_MODEL, D_MODEL), const2, pipeline_mode=pl.Buffered(1)),
            pl.BlockSpec((1, D_MODEL), const2),
        ],
        out_specs=[
            pl.BlockSpec((tm, D_MODEL), row),
            pl.BlockSpec((tm, D_MODEL), row),
            pl.BlockSpec((None, POOL_HIST, POOL_WIDTH), lambda i: (i // tiles_per_seq, 0, 0)),
        ],
        out_shape=[
            jax.ShapeDtypeStruct((m, D_MODEL), F32),
            jax.ShapeDtypeStruct((m, D_MODEL), BF16),
            jax.ShapeDtypeStruct((batch, POOL_HIST, POOL_WIDTH), F32),
        ],
        scratch_shapes=[pltpu.VMEM((POOL_PAD + tm, POOL_WIDTH), F32), pltpu.VMEM((tm, POOL_WIDTH), F32),
                        pltpu.VMEM((POOL_PAD + tm, POOL_GROUP_W), F32), pltpu.VMEM((POOL_PAD + tm, POOL_GROUP_W), F32)],
        compiler_params=_params("arbitrary"),
        name="mid_prompt",
    )(a, u, u, x, wp_bf, ps, wo_bf, g2)


def _mid_sample_kernel(a_ref, u_ref, hist_ref, x_ref, wp_ref, ps_ref, wo_ref, g2_ref,
                       x1_ref, h2_ref, newhist_ref, ext_ref, d_ref, *, n_seq, t_new, start_pos):
    stride = POOL_HIST + t_new
    stride += -stride % SUBLANES
    pos = start_pos + lax.broadcasted_iota(jnp.int32, (t_new, 1), 0)
    for b in range(n_seq):
        base = b * stride
        ext_ref[base:base + POOL_HIST, :] = hist_ref[b]
        ext_ref[base + POOL_HIST:base + POOL_HIST + t_new, :] = u_ref[b * t_new:(b + 1) * t_new, :]
        _pool_diff(ext_ref, base + POOL_HIST, t_new, pos, d_ref, b * t_new)
        newhist_ref[b] = ext_ref[pl.ds(base + t_new, POOL_HIST), :]
    x1 = x_ref[...] + _mix_out(a_ref[...].astype(BF16), d_ref, wp_ref, ps_ref, wo_ref)
    x1_ref[...] = x1
    h2_ref[...] = _rmsnorm(x1, g2_ref[...]).astype(BF16)


def _mid_sample(a, u, hist, x, wp_bf, ps, wo_bf, g2, n_seq, t_new, start_pos):
    m = x.shape[0]
    stride = POOL_HIST + t_new
    stride += -stride % SUBLANES
    return pl.pallas_call(
        functools.partial(_mid_sample_kernel, n_seq=n_seq, t_new=t_new, start_pos=start_pos),
        out_shape=[
            jax.ShapeDtypeStruct((m, D_MODEL), F32),
            jax.ShapeDtypeStruct((m, D_MODEL), BF16),
            jax.ShapeDtypeStruct((n_seq, POOL_HIST, POOL_WIDTH), F32),
        ],
        scratch_shapes=[pltpu.VMEM((n_seq * stride, POOL_WIDTH), F32), pltpu.VMEM((m, POOL_WIDTH), F32)],
        compiler_params=pltpu.CompilerParams(vmem_limit_bytes=VMEM_LIMIT_BYTES),
        name="mid_sample",
    )(a, u, hist, x, wp_bf, ps, wo_bf, g2)


def _store_block_mean(o_ref, jb, tot):
    mean = tot * (1.0 / MOBA_BLOCK)
    for hd in range(N_HEADS):
        o_ref[jb:jb + 1, hd * HEAD_DIM:(hd + 1) * HEAD_DIM] = mean[hd:hd + 1, :]


def _ffn_kernel(*refs, n_pages):
    if n_pages:
        refs = refs[1:]
    h2_ref, wg_ref, wu_ref, wd_ref, x1_ref, gf_ref = refs[:6]
    page_refs = refs[6:6 + n_pages]
    y_ref = refs[6 + n_pages]
    km_ref = refs[7 + n_pages] if n_pages else None
    j = pl.program_id(1)

    @pl.when(j == 0)
    def _():
        y_ref[...] = x1_ref[...]

    n_slots = 6
    sums = {}

    def side_job(slot):
        for r in range(slot, n_pages, n_slots):
            sums[r] = jnp.sum(page_refs[r][...], axis=0)
            jb, pos = divmod(r, PAGES_PER_BLOCK)
            if pos == PAGES_PER_BLOCK - 1:
                tot = sums[jb * PAGES_PER_BLOCK]
                for rr in range(jb * PAGES_PER_BLOCK + 1, r + 1):
                    tot = tot + sums[rr]
                _store_block_mean(km_ref, jb, tot)

    h2 = h2_ref[...]
    tf = wg_ref.shape[1]
    gate, up = [], []
    for c, cols in enumerate((slice(0, tf // 2), slice(tf // 2, tf))):
        gate.append(jnp.dot(h2, wg_ref[:, cols], preferred_element_type=F32))
        side_job(c)
    for c, cols in enumerate((slice(0, tf // 2), slice(tf // 2, tf))):
        up.append(jnp.dot(h2, wu_ref[:, cols], preferred_element_type=F32))
        side_job(2 + c)
    gate = jnp.concatenate(gate, axis=-1)
    act = (gate * jax.nn.sigmoid(gate) * jnp.concatenate(up, axis=-1)).astype(BF16)
    for c, cols in enumerate((slice(0, D_MODEL // 2), slice(D_MODEL // 2, D_MODEL))):
        y_ref[:, cols] += jnp.dot(act, wd_ref[:, cols], preferred_element_type=F32)
        side_job(4 + c)

    @pl.when(j == pl.num_programs(1) - 1)
    def _():
        y_ref[...] = _rmsnorm(y_ref[...], gf_ref[...])


def _ffn(h2, wg_bf, wu_bf, wd_bf, x1, gf, tm, tf=512, paged_k=None):
    m = h2.shape[0]
    d_ff = wg_bf.shape[1]
    ni, nj = m // tm, d_ff // tf
    in_specs = [
        pl.BlockSpec((tm, D_MODEL), lambda i, j, *_: (i, 0)),
        pl.BlockSpec((D_MODEL, tf), lambda i, j, *_: (0, j)),
        pl.BlockSpec((D_MODEL, tf), lambda i, j, *_: (0, j)),
        pl.BlockSpec((tf, D_MODEL), lambda i, j, *_: (j, 0)),
        pl.BlockSpec((tm, D_MODEL), lambda i, j, *_: (i, 0)),
        pl.BlockSpec((1, D_MODEL), lambda i, j, *_: (0, 0)),
    ]
    y_spec = pl.BlockSpec((tm, D_MODEL), lambda i, j, *_: (i, 0))
    y_shape = jax.ShapeDtypeStruct((m, D_MODEL), F32)
    if paged_k is None:
        return pl.pallas_call(
            functools.partial(_ffn_kernel, n_pages=0),
            grid=(ni, nj), in_specs=in_specs, out_specs=y_spec, out_shape=y_shape,
            compiler_params=_params("arbitrary", "arbitrary"), name="ffn",
        )(h2, wg_bf, wu_bf, wd_bf, x1, gf)

    page_table_flat, cache_k = paged_k
    total = page_table_flat.shape[0]
    pps = PAGES_PER_BLOCK * pl.cdiv(total, PAGES_PER_BLOCK * ni * nj)
    n_events = pl.cdiv(total, pps)

    def page_spec(r):
        def index_map(i, j, pt):
            event = jnp.minimum(i * nj + j, n_events - 1)
            return (0, pt[jnp.minimum(event * pps + r, total - 1)], 0, 0, 0)
        return pl.BlockSpec((None, None, PAGE_SIZE, N_HEADS, HEAD_DIM), index_map)

    bps = pps // PAGES_PER_BLOCK
    km_spec = pl.BlockSpec((None, bps, ATTN_WIDTH), lambda i, j, pt: (jnp.minimum(i * nj + j, n_events - 1), 0, 0))
    y, km = pl.pallas_call(
        functools.partial(_ffn_kernel, n_pages=pps),
        grid_spec=pltpu.PrefetchScalarGridSpec(
            num_scalar_prefetch=1, grid=(ni, nj),
            in_specs=in_specs + [page_spec(r) for r in range(pps)],
            out_specs=[y_spec, km_spec]),
        out_shape=[y_shape, jax.ShapeDtypeStruct((n_events, bps, ATTN_WIDTH), F32)],
        compiler_params=_params("arbitrary", "arbitrary"), name="ffn_kmean",
    )(page_table_flat, h2, wg_bf, wu_bf, wd_bf, x1, gf, *([cache_k] * pps))
    return y, km.reshape(n_events * bps, ATTN_WIDTH)[:total // PAGES_PER_BLOCK]


def _select_sample_kernel(qh_ref, km_ref, idx_ref, sc_ref, *, n_seq, t_new, n_past_blocks):
    lanes = sc_ref.shape[1]
    lane = lax.broadcasted_iota(jnp.int32, sc_ref.shape, 1)
    sc_ref[...] = jnp.where(lane == n_past_blocks, NEG, -jnp.inf)
    for b in range(n_seq):
        for hd in range(N_HEADS):
            q_hl = jnp.concatenate(_split_bf16(qh_ref[hd, b * t_new:(b + 1) * t_new, :]), axis=0)
            km_hi, km_lo = _split_bf16(km_ref[b, :, hd * HEAD_DIM:(hd + 1) * HEAD_DIM])
            by_hi = lax.dot_general(q_hl, km_hi, _NT, preferred_element_type=F32)
            r0 = (b * N_HEADS + hd) * t_new
            sc_ref[r0:r0 + t_new, 0:n_past_blocks] = (
                by_hi[0:t_new] + by_hi[t_new:2 * t_new]
                + lax.dot_general(q_hl[0:t_new], km_lo, _NT, preferred_element_type=F32))
    s = sc_ref[...]
    for r in range(MOBA_TOPK):
        mx = jnp.max(s, axis=-1, keepdims=True)
        am = jnp.min(jnp.where(s == mx, lane, lanes), axis=-1, keepdims=True)
        idx_ref[:, r:r + 1] = am
        s = jnp.where(lane == am, -jnp.inf, s)


def _select_sample(qh, km, n_seq, t_new):
    n_past_blocks = km.shape[1]
    rows = n_seq * N_HEADS * t_new
    return pl.pallas_call(
        functools.partial(_select_sample_kernel, n_seq=n_seq, t_new=t_new, n_past_blocks=n_past_blocks),
        out_shape=jax.ShapeDtypeStruct((rows, MOBA_TOPK), jnp.int32),
        scratch_shapes=[pltpu.VMEM((rows, LANES), F32)],
        compiler_params=pltpu.CompilerParams(vmem_limit_bytes=VMEM_LIMIT_BYTES),
        name="select_sample",
    )(qh, km)


GATHER_DEPTH = 2


def _attn_sample_kernel(pt_ref, idx_ref, slopes_ref, qh_ref, kh_ref, vh_ref, ck_hbm, cv_hbm, o_ref,
                        kbuf, vbuf, sem, *, n_seq, t_new, n_pages):
    n_past_blocks = n_pages // PAGES_PER_BLOCK
    past_len = n_pages * PAGE_SIZE
    n_slots = t_new * MOBA_TOPK
    n_keys = n_slots * MOBA_BLOCK
    n_steps = n_seq * N_HEADS
    n_bufs = GATHER_DEPTH + 1
    b = pl.program_id(0)
    hd = pl.program_id(1)
    step = b * N_HEADS + hd

    def block_of(st, qj):
        return idx_ref[st * n_slots + qj]

    def gather(st, start):
        bb, hh, sl = st // N_HEADS, lax.rem(st, N_HEADS), lax.rem(st, n_bufs)
        for qj in range(n_slots):
            blk = jnp.minimum(block_of(st, qj), n_past_blocks - 1)
            for half in range(PAGES_PER_BLOCK):
                phys = pt_ref[bb * n_pages + blk * PAGES_PER_BLOCK + half]
                rows = pl.ds(qj * MOBA_BLOCK + half * PAGE_SIZE, PAGE_SIZE)
                for src, dst, s in ((ck_hbm, kbuf, 0), (cv_hbm, vbuf, 1)):
                    cp = pltpu.make_async_copy(src.at[0, phys, :, hh, :], dst.at[sl, rows, :], sem.at[s, sl])
                    if start:
                        cp.start()
                    else:
                        cp.wait()

    @pl.when(step == 0)
    def _():
        for d in range(min(GATHER_DEPTH, n_steps)):
            gather(step + d, True)

    @pl.when(step + GATHER_DEPTH < n_steps)
    def _():
        gather(step + GATHER_DEPTH, True)

    gather(step, False)

    buf = lax.rem(step, n_bufs)
    slope = slopes_ref[hd]
    rows = pl.ds(b * t_new, t_new)
    qs = (qh_ref[rows, :] * ATTN_SCALE).astype(BF16)

    s = lax.dot_general(qs, kbuf[buf].astype(BF16), _NT, preferred_element_type=F32)
    row = lax.broadcasted_iota(jnp.int32, (t_new, n_keys), 0)
    col = lax.broadcasted_iota(jnp.int32, (t_new, n_keys), 1)
    shift = jnp.zeros((t_new, n_keys), jnp.int32)
    valid = jnp.zeros((t_new, n_keys), jnp.int32)
    for qj in range(n_slots):
        blk = block_of(step, qj)
        in_slot = jnp.logical_and(col >= qj * MOBA_BLOCK, col < (qj + 1) * MOBA_BLOCK)
        shift = jnp.where(in_slot, blk * MOBA_BLOCK - qj * MOBA_BLOCK, shift)
        valid = jnp.where(in_slot, (blk < n_past_blocks).astype(jnp.int32), valid)
    dist = (past_len + row - (col + shift)).astype(F32)
    q_lo = row * (MOBA_TOPK * MOBA_BLOCK)
    mine = jnp.logical_and(col >= q_lo, col < q_lo + MOBA_TOPK * MOBA_BLOCK)
    s = jnp.where(jnp.logical_and(mine, valid > 0), s - slope * dist, NEG)
    own_rel = (lax.broadcasted_iota(jnp.int32, (t_new, t_new), 0)
               - lax.broadcasted_iota(jnp.int32, (t_new, t_new), 1))
    s_own = lax.dot_general(qs, kh_ref[rows, :].astype(BF16), _NT, preferred_element_type=F32)
    s_own = jnp.where(own_rel >= 0, s_own - slope * own_rel.astype(F32), NEG)

    m = jnp.maximum(jnp.max(s_own, axis=-1, keepdims=True), jnp.max(s, axis=-1, keepdims=True))
    p_own = jnp.exp(s_own - m)
    p = jnp.exp(s - m)
    l = jnp.sum(p_own, axis=-1, keepdims=True) + jnp.sum(p, axis=-1, keepdims=True)
    acc = (jnp.dot(p_own.astype(BF16), vh_ref[rows, :].astype(BF16), preferred_element_type=F32)
           + jnp.dot(p.astype(BF16), vbuf[buf].astype(BF16), preferred_element_type=F32))
    o_ref[...] = acc / l


def _attn_sample(page_table_flat, idx_flat, slopes, qh, kh, vh, cache_k, cache_v, n_seq, t_new, n_pages):
    n_slots = t_new * MOBA_TOPK
    head_blk = pl.BlockSpec((None, n_seq * t_new, HEAD_DIM), lambda b, h, pt, ix, sl: (h, 0, 0))
    grid_spec = pltpu.PrefetchScalarGridSpec(
        num_scalar_prefetch=3,
        grid=(n_seq, N_HEADS),
        in_specs=[head_blk, head_blk, head_blk,
                  pl.BlockSpec(memory_space=pl.ANY), pl.BlockSpec(memory_space=pl.ANY)],
        out_specs=pl.BlockSpec((None, t_new, HEAD_DIM), lambda b, h, pt, ix, sl: (b, 0, h)),
        scratch_shapes=[
            pltpu.VMEM((GATHER_DEPTH + 1, n_slots * MOBA_BLOCK, HEAD_DIM), F32),
            pltpu.VMEM((GATHER_DEPTH + 1, n_slots * MOBA_BLOCK, HEAD_DIM), F32),
            pltpu.SemaphoreType.DMA((2, GATHER_DEPTH + 1)),
        ],
    )
    return pl.pallas_call(
        functools.partial(_attn_sample_kernel, n_seq=n_seq, t_new=t_new, n_pages=n_pages),
        grid_spec=grid_spec,
        out_shape=jax.ShapeDtypeStruct((n_seq, t_new, ATTN_WIDTH), F32),
        compiler_params=_params("arbitrary", "arbitrary"),
        name="attn_sample",
    )(page_table_flat, idx_flat, slopes, qh, kh, vh, cache_k, cache_v)


def kernel(x_prompt, x_sample, cache_k, cache_v, state_pool, page_table, norm1_g, w_in, w_pool, pool_scale,
           w_out, norm2_g, w_gate, w_up, w_down, norm_f_g):
    depth = w_in.shape[0]
    assert depth == 1, "single-layer step"
    batch, seq, _ = x_prompt.shape
    n_seq, t_new, _ = x_sample.shape
    n_pages = page_table.shape[1]
    past_len = n_pages * PAGE_SIZE
    assert seq % MOBA_BLOCK == 0 and past_len % MOBA_BLOCK == 0 and t_new <= MOBA_BLOCK

    slopes = jnp.asarray(_alibi_slopes())
    g1 = norm1_g[0].reshape(1, D_MODEL)
    g2 = norm2_g[0].reshape(1, D_MODEL)
    gf = norm_f_g.reshape(1, D_MODEL)
    ps = pool_scale[0].reshape(1, POOL_WIDTH)
    w_pool_bf = w_pool[0].astype(BF16)

    xp = x_prompt.reshape(batch * seq, D_MODEL)
    pt_flat = page_table.reshape(-1)
    k, v, u, qh_p, kbh, vbh, km = _proj_prompt(xp, g1, w_in[0])
    km = km.reshape(batch, seq // MOBA_BLOCK, ATTN_WIDTH)
    a, (w_out_bf, w_gate_bf, w_up_bf, w_down_bf) = _attn_prompt(
        qh_p, kbh, vbh, km, _alibi_key_table(seq), batch, seq,
        (w_out[0], w_gate[0], w_up[0], w_down[0]))
    x1, h2, hist_p = _mid_prompt(a, u, xp, w_pool_bf, ps, w_out_bf, g2, batch, seq)
    y_p, km_s = _ffn(h2, w_gate_bf, w_up_bf, w_down_bf, x1, gf, tm=512, paged_k=(pt_flat, cache_k))

    xs = x_sample.reshape(n_seq * t_new, D_MODEL)
    k_s, v_s, u_s, qh, kh, vh = _proj_sample(xs, g1, w_in[0])
    km_s = km_s.reshape(n_seq, n_pages // PAGES_PER_BLOCK, ATTN_WIDTH)
    idx = _select_sample(qh, km_s, n_seq, t_new)
    a_s = _attn_sample(pt_flat, idx.reshape(-1), slopes, qh, kh, vh, cache_k, cache_v, n_seq, t_new, n_pages)
    x1_s, h2_s, hist_s = _mid_sample(a_s.reshape(n_seq * t_new, ATTN_WIDTH), u_s, state_pool[0], xs,
                                     w_pool_bf, ps, w_out_bf, g2, n_seq, t_new, past_len)
    y_s = _ffn(h2_s, w_gate_bf, w_up_bf, w_down_bf, x1_s, gf, tm=n_seq * t_new)

    kv_p = (depth, batch, seq, N_HEADS, HEAD_DIM)
    kv_s = (depth, n_seq, t_new, N_HEADS, HEAD_DIM)
    return (y_p.reshape(batch, seq, D_MODEL), y_s.reshape(n_seq, t_new, D_MODEL),
            k.reshape(kv_p), v.reshape(kv_p), hist_p[None],
            k_s.reshape(kv_s), v_s.reshape(kv_s), hist_s[None])
```

```python
import functools

import jax
import jax.numpy as jnp
import numpy as np
from jax import lax
from jax.experimental import pallas as pl
from jax.experimental.pallas import tpu as pltpu

D_MODEL = 2048
HEAD_DIM = 128
N_HEADS = 8
ATTN_WIDTH = N_HEADS * HEAD_DIM
POOL_WIDTH = D_MODEL - ATTN_WIDTH
IN_WIDTH = 3 * ATTN_WIDTH + POOL_WIDTH
MOBA_BLOCK = 256
MOBA_TOPK = 3
PAGE_SIZE = 128
PAGES_PER_BLOCK = MOBA_BLOCK // PAGE_SIZE
POOL_WINDOWS = (2, 4, 8, 16)
POOL_GROUP_W = POOL_WIDTH // len(POOL_WINDOWS)
POOL_HIST = max(POOL_WINDOWS) - 1
LANES = 128
SUBLANES = 8
BF16_SUBLANES = 16
HIST_PAD = 2 * SUBLANES
RMS_EPS = 1e-6
NEG = -1e30
ATTN_SCALE = HEAD_DIM ** -0.5

V7X_VMEM_BYTES = 64 * 1024 * 1024
VMEM_LIMIT_BYTES = V7X_VMEM_BYTES - 8 * 1024 * 1024

F32 = jnp.float32
BF16 = jnp.bfloat16
_NT = (((1,), (1,)), ((), ()))


def _rmsnorm(x, g):
    y = x * lax.rsqrt(jnp.mean(x * x, axis=-1, keepdims=True) + RMS_EPS)
    return y * g


def _params(*semantics):
    return pltpu.CompilerParams(dimension_semantics=semantics, vmem_limit_bytes=VMEM_LIMIT_BYTES)


W_CAST_CHUNKS = 8


def _proj_prompt_kernel(x_ref, g_ref, w_hbm, k_ref, v_ref, u_ref, qh_ref, kbh_ref, vbh_ref, km_ref,
                        w_ref, stage_ref, sem, *, tm):
    @pl.when(pl.program_id(0) == 0)
    def _():
        rows = D_MODEL // W_CAST_CHUNKS

        def chunk_copy(c):
            return pltpu.make_async_copy(w_hbm.at[pl.ds(c * rows, rows), :], stage_ref.at[c % 2], sem.at[c % 2])

        chunk_copy(0).start()
        for c in range(W_CAST_CHUNKS):
            if c + 1 < W_CAST_CHUNKS:
                chunk_copy(c + 1).start()
            chunk_copy(c).wait()
            w_ref[c * rows:(c + 1) * rows, :] = stage_ref[c % 2].astype(BF16)

    h = _rmsnorm(x_ref[...], g_ref[...]).astype(BF16)

    def seg(s):
        return jnp.dot(h, w_ref[:, s * ATTN_WIDTH:(s + 1) * ATTN_WIDTH], preferred_element_type=F32)

    zq = seg(0)
    for hd in range(N_HEADS):
        qh_ref[hd] = zq[:, hd * HEAD_DIM:(hd + 1) * HEAD_DIM]
    zk = seg(1)
    k_ref[...] = zk
    for hd in range(N_HEADS):
        kbh_ref[hd] = zk[:, hd * HEAD_DIM:(hd + 1) * HEAD_DIM].astype(BF16)
    for blk in range(tm // MOBA_BLOCK):
        rows = zk[blk * MOBA_BLOCK:(blk + 1) * MOBA_BLOCK]
        km_ref[blk] = jnp.sum(rows, axis=0, keepdims=True) * (1.0 / MOBA_BLOCK)
    zv = seg(2)
    v_ref[...] = zv
    for hd in range(N_HEADS):
        vbh_ref[hd] = zv[:, hd * HEAD_DIM:(hd + 1) * HEAD_DIM].astype(BF16)
    u_ref[...] = seg(3)


def _proj_prompt(x, g, w, tm=256):
    m = x.shape[0]
    row = lambda i: (i, 0)
    f32_out = jax.ShapeDtypeStruct((m, ATTN_WIDTH), F32)
    km_out = jax.ShapeDtypeStruct((m // MOBA_BLOCK, 1, ATTN_WIDTH), F32)
    blk = pl.BlockSpec((tm, ATTN_WIDTH), row)
    hm_blk = pl.BlockSpec((N_HEADS, tm, HEAD_DIM), lambda i: (0, i, 0))
    return pl.pallas_call(
        functools.partial(_proj_prompt_kernel, tm=tm),
        grid=(m // tm,),
        in_specs=[
            pl.BlockSpec((tm, D_MODEL), row),
            pl.BlockSpec((1, D_MODEL), lambda i: (0, 0)),
            pl.BlockSpec(memory_space=pl.ANY),
        ],
        out_specs=[blk, blk, blk, hm_blk, hm_blk, hm_blk,
                   pl.BlockSpec((tm // MOBA_BLOCK, 1, ATTN_WIDTH), lambda i: (i, 0, 0))],
        out_shape=[f32_out, f32_out, f32_out,
                   jax.ShapeDtypeStruct((N_HEADS, m, HEAD_DIM), F32),
                   jax.ShapeDtypeStruct((N_HEADS, m, HEAD_DIM), BF16),
                   jax.ShapeDtypeStruct((N_HEADS, m, HEAD_DIM), BF16), km_out],
        scratch_shapes=[
            pltpu.VMEM((D_MODEL, IN_WIDTH), BF16),
            pltpu.VMEM((2, D_MODEL // W_CAST_CHUNKS, IN_WIDTH), F32),
            pltpu.SemaphoreType.DMA((2,)),
        ],
        compiler_params=_params("arbitrary"),
        name="proj_prompt",
    )(x, g, w)


def _proj_sample_kernel(x_ref, g_ref, w_ref, k_ref, v_ref, u_ref, qh_ref, kh_ref, vh_ref):
    seg = pl.program_id(0)
    h = _rmsnorm(x_ref[...], g_ref[...]).astype(BF16)
    z = jnp.dot(h, w_ref[...].astype(BF16), preferred_element_type=F32)

    def head_major(o_ref):
        for hd in range(N_HEADS):
            o_ref[hd] = z[:, hd * HEAD_DIM:(hd + 1) * HEAD_DIM]

    @pl.when(seg == 0)
    def _():
        head_major(qh_ref)

    @pl.when(seg == 1)
    def _():
        k_ref[...] = z
        head_major(kh_ref)

    @pl.when(seg == 2)
    def _():
        v_ref[...] = z
        head_major(vh_ref)

    @pl.when(seg == 3)
    def _():
        u_ref[...] = z


def _proj_sample(x, g, w):
    m = x.shape[0]
    f32_out = jax.ShapeDtypeStruct((m, ATTN_WIDTH), F32)
    hm_out = jax.ShapeDtypeStruct((N_HEADS, m, HEAD_DIM), F32)
    whole2 = pl.BlockSpec((m, ATTN_WIDTH), lambda s: (0, 0))
    whole3 = pl.BlockSpec((N_HEADS, m, HEAD_DIM), lambda s: (0, 0, 0))
    return pl.pallas_call(
        _proj_sample_kernel,
        grid=(IN_WIDTH // ATTN_WIDTH,),
        in_specs=[
            pl.BlockSpec((m, D_MODEL), lambda s: (0, 0)),
            pl.BlockSpec((1, D_MODEL), lambda s: (0, 0)),
            pl.BlockSpec((D_MODEL, ATTN_WIDTH), lambda s: (0, s)),
        ],
        out_specs=[whole2, whole2, whole2, whole3, whole3, whole3],
        out_shape=[f32_out, f32_out, f32_out, hm_out, hm_out, hm_out],
        compiler_params=_params("arbitrary"),
        name="proj_sample",
    )(x, g, w)


def _alibi_slopes():
    return np.exp2(-8.0 * np.arange(1, N_HEADS + 1, dtype=np.float32) / N_HEADS)


def _alibi_key_table(seq):
    nb = seq // MOBA_BLOCK
    t = np.arange(seq)
    blk, local = t // MOBA_BLOCK, t % MOBA_BLOCK
    table = np.zeros((N_HEADS, seq, HEAD_DIM), np.float32)
    table[:, t, blk] = 1.0
    table[:, :, nb] = _alibi_slopes()[:, None] * local[None, :]
    table[:, :, nb + 1] = _alibi_slopes()[:, None] * (blk * MOBA_BLOCK)[None, :]
    return jnp.asarray(table.astype(BF16))


def _split_bf16(x):
    hi = x.astype(BF16)
    return hi, (x - hi.astype(F32)).astype(BF16)


def _topk_block_bias_t(scores_t, n_past):
    nb = scores_t.shape[0]
    n_iota = lax.broadcasted_iota(jnp.int32, scores_t.shape, 0)
    past = n_iota < n_past
    s = jnp.where(past, scores_t, NEG)
    rank = jnp.zeros(scores_t.shape, jnp.int32)
    for m in range(nb):
        sm = s[m:m + 1, :]
        tie_lower = jnp.logical_and(sm == s, n_iota > m)
        rank = rank + jnp.where(sm > s, 1, jnp.where(tie_lower, 1, 0))
    keep = jnp.logical_or(jnp.logical_and(past, rank < MOBA_TOPK), n_iota == n_past)
    return jnp.where(keep, 0.0, NEG)


def _attn_prompt_kernel(q_ref, kb_ref, vb_ref, km_ref, kx_ref, *refs, nb, n_cast):
    w_refs, o_ref, wbf_refs = refs[:n_cast], refs[n_cast], refs[n_cast + 1:2 * n_cast + 1]
    qa_ref, ka_ref, s_ref, p_ref, acc_ref = refs[2 * n_cast + 1:]
    for w_ref, wbf_ref in zip(w_refs, wbf_refs):
        wbf_ref[...] = w_ref[...].astype(BF16)
    seq = nb * MOBA_BLOCK
    km_hl = jnp.concatenate(_split_bf16(km_ref[...]), axis=0)
    shape = (MOBA_BLOCK, MOBA_BLOCK)
    causal = lax.broadcasted_iota(jnp.int32, shape, 0) >= lax.broadcasted_iota(jnp.int32, shape, 1)

    ka_ref[:, 0:HEAD_DIM] = kb_ref[...]
    ka_ref[:, HEAD_DIM:] = kx_ref[...]
    ones_rows = jnp.where(lax.broadcasted_iota(jnp.int32, (SUBLANES, MOBA_BLOCK), 0) < 2, 1.0, 0.0)
    zero_rows = jnp.zeros((HEAD_DIM - nb - SUBLANES, MOBA_BLOCK), F32)
    for qi in range(nb):
        rows = slice(qi * MOBA_BLOCK, (qi + 1) * MOBA_BLOCK)
        q = q_ref[rows, :]
        q_hi, q_lo = _split_bf16(q)
        sel_hl = lax.dot_general(km_hl, q_hi, _NT, preferred_element_type=F32)
        sel_t = (sel_hl[0:nb] + sel_hl[nb:2 * nb]
                 + lax.dot_general(km_hl[0:nb], q_lo, _NT, preferred_element_type=F32))
        bias_t = _topk_block_bias_t(sel_t, qi)
        qa_ref[rows, 0:HEAD_DIM] = (q * ATTN_SCALE).astype(BF16)
        qa_ref[rows, HEAD_DIM:] = jnp.concatenate([bias_t, ones_rows, zero_rows], axis=0).T.astype(BF16)

    for n in range(nb):
        r0 = n * MOBA_BLOCK
        s = lax.dot_general(qa_ref[r0:seq, :], ka_ref[r0:r0 + MOBA_BLOCK, :], _NT, preferred_element_type=F32)
        s_ref[n, r0:r0 + MOBA_BLOCK, :] = jnp.where(causal, s[0:MOBA_BLOCK], NEG)
        if r0 + MOBA_BLOCK < seq:
            s_ref[n, r0 + MOBA_BLOCK:seq, :] = s[MOBA_BLOCK:]

    inv_l = []
    for qi in range(nb):
        rows = slice(qi * MOBA_BLOCK, (qi + 1) * MOBA_BLOCK)
        m = jnp.max(s_ref[0, rows, :], axis=-1, keepdims=True)
        for n in range(1, qi + 1):
            m = jnp.maximum(m, jnp.max(s_ref[n, rows, :], axis=-1, keepdims=True))
        l = jnp.zeros((MOBA_BLOCK, 1), F32)
        for n in range(qi + 1):
            p = jnp.exp(s_ref[n, rows, :] - m)
            l = l + jnp.sum(p, axis=-1, keepdims=True)
            p_ref[n, rows, :] = p.astype(BF16)
        inv_l.append(1.0 / l)

    for n in range(nb):
        r0 = n * MOBA_BLOCK
        pv = jnp.dot(p_ref[n, r0:seq, :], vb_ref[r0:r0 + MOBA_BLOCK, :], preferred_element_type=F32)
        if n == 0:
            acc_ref[...] = pv
        else:
            acc_ref[r0:seq, :] += pv
    for qi in range(nb):
        rows = slice(qi * MOBA_BLOCK, (qi + 1) * MOBA_BLOCK)
        o_ref[rows, :] = (acc_ref[rows, :] * inv_l[qi]).astype(o_ref.dtype)


def _attn_prompt(qh, kbh, vbh, km, kx, batch, seq, cast_weights):
    nb = seq // MOBA_BLOCK
    assert nb == SUBLANES, "bias rows fill one sublane tile"
    steps = batch * N_HEADS
    slab = pl.BlockSpec((None, seq, HEAD_DIM), lambda b, h: (h, b, 0))
    chunk_specs = []
    for w in cast_weights:
        assert w.shape[0] % (BF16_SUBLANES * steps) == 0, "bf16 row chunks must be whole (16, 128) tiles"
        chunk_specs.append(pl.BlockSpec((w.shape[0] // steps, w.shape[1]), lambda b, h: (b * N_HEADS + h, 0)))
    res = pl.pallas_call(
        functools.partial(_attn_prompt_kernel, nb=nb, n_cast=len(cast_weights)),
        grid=(batch, N_HEADS),
        in_specs=[
            slab, slab, slab,
            pl.BlockSpec((None, nb, HEAD_DIM), lambda b, h: (b, 0, h)),
            pl.BlockSpec((None, seq, HEAD_DIM), lambda b, h: (h, 0, 0)),
        ] + chunk_specs,
        out_specs=[slab] + chunk_specs,
        out_shape=[jax.ShapeDtypeStruct((N_HEADS, batch * seq, HEAD_DIM), BF16)]
        + [jax.ShapeDtypeStruct(w.shape, BF16) for w in cast_weights],
        scratch_shapes=[
            pltpu.VMEM((seq, 2 * HEAD_DIM), BF16),
            pltpu.VMEM((seq, 2 * HEAD_DIM), BF16),
            pltpu.VMEM((nb, seq, MOBA_BLOCK), F32),
            pltpu.VMEM((nb, seq, MOBA_BLOCK), BF16),
            pltpu.VMEM((seq, HEAD_DIM), F32),
        ],
        compiler_params=_params("arbitrary", "arbitrary"),
        name="attn_prompt",
    )(qh, kbh, vbh, km, kx, *cast_weights)
    return res[0], res[1:]


def _pool_diff(ext_ref, r0, n, pos, d_ref, d0):
    for g, w in enumerate(POOL_WINDOWS):
        ls = slice(g * POOL_GROUP_W, (g + 1) * POOL_GROUP_W)
        cur = ext_ref[pl.ds(r0, n), ls]
        win = cur
        for j in range(1, w):
            win = win + ext_ref[pl.ds(r0 - j, n), ls]
        cnt = jnp.minimum(pos + 1, w).astype(F32)
        d_ref[pl.ds(d0, n), ls] = win / cnt - cur


def _mix_out(a_bf, d_ref, wp_ref, ps_ref, wo_ref):
    out = jnp.dot(a_bf, wo_ref[0:ATTN_WIDTH, :], preferred_element_type=F32)
    ys = []
    for g in range(len(POOL_WINDOWS)):
        ls = slice(g * POOL_GROUP_W, (g + 1) * POOL_GROUP_W)
        y = jnp.dot(d_ref[:, ls].astype(BF16), wp_ref[g], preferred_element_type=F32)
        ys.append((y * ps_ref[:, ls]).astype(BF16))
    return out + jnp.dot(jnp.concatenate(ys, axis=-1), wo_ref[ATTN_WIDTH:, :], preferred_element_type=F32)


POOL_PAD = HIST_PAD + SUBLANES


def _pool_diff_tile(ext_ref, t1_ref, t2_ref, tm, pos, d_ref):
    lo = SUBLANES
    n_all = POOL_PAD - lo + tm
    t1_ref[0:lo, :] = jnp.zeros((lo, POOL_GROUP_W), F32)
    t2_ref[0:lo, :] = jnp.zeros((lo, POOL_GROUP_W), F32)
    for g, w in enumerate(POOL_WINDOWS):
        ls = slice(g * POOL_GROUP_W, (g + 1) * POOL_GROUP_W)
        src, lanes, k, stage = ext_ref, ls, 1, 0
        while 2 * k < w:
            dst = (t1_ref, t2_ref)[stage % 2]
            dst[pl.ds(lo, n_all), :] = src[pl.ds(lo, n_all), lanes] + src[pl.ds(lo - k, n_all), lanes]
            src, lanes, k, stage = dst, slice(None), 2 * k, stage + 1
        win = src[pl.ds(POOL_PAD, tm), lanes] + src[pl.ds(POOL_PAD - k, tm), lanes]
        cnt = jnp.minimum(pos + 1, w).astype(F32)
        d_ref[:, ls] = win / cnt - ext_ref[pl.ds(POOL_PAD, tm), ls]


def _mid_prompt_kernel(a_ref, u_ref, uprev_ref, x_ref, wp_ref, ps_ref, wo_ref, g2_ref,
                       x1_ref, h2_ref, hist_ref, ext_ref, d_ref, t1_ref, t2_ref, *, tm, tiles_per_seq):
    t_in_seq = lax.rem(pl.program_id(0), tiles_per_seq)
    first = POOL_PAD - HIST_PAD
    ext_ref[0:first, :] = jnp.zeros((first, POOL_WIDTH), F32)
    ext_ref[first:POOL_PAD, :] = jnp.where(t_in_seq == 0, 0.0, uprev_ref[...])
    ext_ref[POOL_PAD:POOL_PAD + tm, :] = u_ref[...]
    pos = t_in_seq * tm + lax.broadcasted_iota(jnp.int32, (tm, 1), 0)
    _pool_diff_tile(ext_ref, t1_ref, t2_ref, tm, pos, d_ref)
    a = jnp.concatenate([a_ref[hd] for hd in range(N_HEADS)], axis=-1)
    x1 = x_ref[...] + _mix_out(a, d_ref, wp_ref, ps_ref, wo_ref)
    x1_ref[...] = x1
    h2_ref[...] = _rmsnorm(x1, g2_ref[...]).astype(BF16)

    @pl.when(t_in_seq == tiles_per_seq - 1)
    def _():
        hist_ref[...] = ext_ref[pl.ds(POOL_PAD + tm - POOL_HIST, POOL_HIST), :]


def _mid_prompt(a, u, x, wp_bf, ps, wo_bf, g2, batch, seq, tm=256):
    m = x.shape[0]
    tiles_per_seq = seq // tm
    row = lambda i: (i, 0)
    const2 = lambda i: (0, 0)
    return pl.pallas_call(
        functools.partial(_mid_prompt_kernel, tm=tm, tiles_per_seq=tiles_per_seq),
        grid=(m // tm,),
        in_specs=[
            pl.BlockSpec((N_HEADS, tm, HEAD_DIM), lambda i: (0, i, 0)),
            pl.BlockSpec((tm, POOL_WIDTH), row),
            pl.BlockSpec((HIST_PAD, POOL_WIDTH), lambda i: (jnp.maximum(i * (tm // HIST_PAD) - 1, 0), 0)),
            pl.BlockSpec((tm, D_MODEL), row),
            pl.BlockSpec(wp_bf.shape, lambda i: (0, 0, 0), pipeline_mode=pl.Buffered(1)),
            pl.BlockSpec((1, POOL_WIDTH), const2),
            pl.BlockSpec((D_MODEL, D_MODEL), const2, pipeline_mode=pl.Buffered(1)),
            pl.BlockSpec((1, D_MODEL), const2),
        ],
        out_specs=[
            pl.BlockSpec((tm, D_MODEL), row),
            pl.BlockSpec((tm, D_MODEL), row),
            pl.BlockSpec((None, POOL_HIST, POOL_WIDTH), lambda i: (i // tiles_per_seq, 0, 0)),
        ],
        out_shape=[
            jax.ShapeDtypeStruct((m, D_MODEL), F32),
            jax.ShapeDtypeStruct((m, D_MODEL), BF16),
            jax.ShapeDtypeStruct((batch, POOL_HIST, POOL_WIDTH), F32),
        ],
        scratch_shapes=[pltpu.VMEM((POOL_PAD + tm, POOL_WIDTH), F32), pltpu.VMEM((tm, POOL_WIDTH), F32),
                        pltpu.VMEM((POOL_PAD + tm, POOL_GROUP_W), F32), pltpu.VMEM((POOL_PAD + tm, POOL_GROUP_W), F32)],
        compiler_params=_params("arbitrary"),
        name="mid_prompt",
    )(a, u, u, x, wp_bf, ps, wo_bf, g2)


def _mid_sample_kernel(a_ref, u_ref, hist_ref, x_ref, wp_ref, ps_ref, wo_ref, g2_ref,
                       x1_ref, h2_ref, newhist_ref, ext_ref, d_ref, *, n_seq, t_new, start_pos):
    stride = POOL_HIST + t_new
    stride += -stride % SUBLANES
    pos = start_pos + lax.broadcasted_iota(jnp.int32, (t_new, 1), 0)
    for b in range(n_seq):
        base = b * stride
        ext_ref[base:base + POOL_HIST, :] = hist_ref[b]
        ext_ref[base + POOL_HIST:base + POOL_HIST + t_new, :] = u_ref[b * t_new:(b + 1) * t_new, :]
        _pool_diff(ext_ref, base + POOL_HIST, t_new, pos, d_ref, b * t_new)
        newhist_ref[b] = ext_ref[pl.ds(base + t_new, POOL_HIST), :]
    x1 = x_ref[...] + _mix_out(a_ref[...].astype(BF16), d_ref, wp_ref, ps_ref, wo_ref)
    x1_ref[...] = x1
    h2_ref[...] = _rmsnorm(x1, g2_ref[...]).astype(BF16)


def _mid_sample(a, u, hist, x, wp_bf, ps, wo_bf, g2, n_seq, t_new, start_pos):
    m = x.shape[0]
    stride = POOL_HIST + t_new
    stride += -stride % SUBLANES
    return pl.pallas_call(
        functools.partial(_mid_sample_kernel, n_seq=n_seq, t_new=t_new, start_pos=start_pos),
        out_shape=[
            jax.ShapeDtypeStruct((m, D_MODEL), F32),
            jax.ShapeDtypeStruct((m, D_MODEL), BF16),
            jax.ShapeDtypeStruct((n_seq, POOL_HIST, POOL_WIDTH), F32),
        ],
        scratch_shapes=[pltpu.VMEM((n_seq * stride, POOL_WIDTH), F32), pltpu.VMEM((m, POOL_WIDTH), F32)],
        compiler_params=pltpu.CompilerParams(vmem_limit_bytes=VMEM_LIMIT_BYTES),
        name="mid_sample",
    )(a, u, hist, x, wp_bf, ps, wo_bf, g2)


def _ffn_kernel(*refs, n_pages):
    if n_pages:
        refs = refs[1:]
    h2_ref, wg_ref, wu_ref, wd_ref, x1_ref, gf_ref = refs[:6]
    page_refs = refs[6:6 + n_pages]
    y_ref = refs[6 + n_pages]
    km_ref = refs[7 + n_pages] if n_pages else None
    j = pl.program_id(1)

    @pl.when(j == 0)
    def _():
        y_ref[...] = x1_ref[...]

    n_slots = 6
    sums = {}

    def side_job(slot):
        for r in range(slot, n_pages, n_slots):
            sums[r] = jnp.sum(page_refs[r][...], axis=0)
            jb, pos = divmod(r, PAGES_PER_BLOCK)
            if pos == PAGES_PER_BLOCK - 1:
                tot = sums[jb * PAGES_PER_BLOCK]
                for rr in range(jb * PAGES_PER_BLOCK + 1, r + 1):
                    tot = tot + sums[rr]
                km_ref[jb] = tot * (1.0 / MOBA_BLOCK)

    h2 = h2_ref[...]
    tf = wg_ref.shape[1]
    gate, up = [], []
    for c, cols in enumerate((slice(0, tf // 2), slice(tf // 2, tf))):
        gate.append(jnp.dot(h2, wg_ref[:, cols], preferred_element_type=F32))
        side_job(c)
    for c, cols in enumerate((slice(0, tf // 2), slice(tf // 2, tf))):
        up.append(jnp.dot(h2, wu_ref[:, cols], preferred_element_type=F32))
        side_job(2 + c)
    gate = jnp.concatenate(gate, axis=-1)
    act = (gate * jax.nn.sigmoid(gate) * jnp.concatenate(up, axis=-1)).astype(BF16)
    for c, cols in enumerate((slice(0, D_MODEL // 2), slice(D_MODEL // 2, D_MODEL))):
        y_ref[:, cols] += jnp.dot(act, wd_ref[:, cols], preferred_element_type=F32)
        side_job(4 + c)

    @pl.when(j == pl.num_programs(1) - 1)
    def _():
        y_ref[...] = _rmsnorm(y_ref[...], gf_ref[...])


def _ffn(h2, wg_bf, wu_bf, wd_bf, x1, gf, tm, tf=512, paged_k=None):
    m = h2.shape[0]
    d_ff = wg_bf.shape[1]
    ni, nj = m // tm, d_ff // tf
    in_specs = [
        pl.BlockSpec((tm, D_MODEL), lambda i, j, *_: (i, 0)),
        pl.BlockSpec((D_MODEL, tf), lambda i, j, *_: (0, j)),
        pl.BlockSpec((D_MODEL, tf), lambda i, j, *_: (0, j)),
        pl.BlockSpec((tf, D_MODEL), lambda i, j, *_: (j, 0)),
        pl.BlockSpec((tm, D_MODEL), lambda i, j, *_: (i, 0)),
        pl.BlockSpec((1, D_MODEL), lambda i, j, *_: (0, 0)),
    ]
    y_spec = pl.BlockSpec((tm, D_MODEL), lambda i, j, *_: (i, 0))
    y_shape = jax.ShapeDtypeStruct((m, D_MODEL), F32)
    if paged_k is None:
        return pl.pallas_call(
            functools.partial(_ffn_kernel, n_pages=0),
            grid=(ni, nj), in_specs=in_specs, out_specs=y_spec, out_shape=y_shape,
            compiler_params=_params("arbitrary", "arbitrary"), name="ffn",
        )(h2, wg_bf, wu_bf, wd_bf, x1, gf)

    page_table_flat, cache_k = paged_k
    total = page_table_flat.shape[0]
    pps = PAGES_PER_BLOCK * pl.cdiv(total, PAGES_PER_BLOCK * ni * nj)
    n_events = pl.cdiv(total, pps)

    def page_spec(r):
        def index_map(i, j, pt):
            event = jnp.minimum(i * nj + j, n_events - 1)
            return (0, pt[jnp.minimum(event * pps + r, total - 1)], 0, 0, 0)
        return pl.BlockSpec((None, None, PAGE_SIZE, N_HEADS, HEAD_DIM), index_map)

    bps = pps // PAGES_PER_BLOCK
    km_spec = pl.BlockSpec((None, bps, N_HEADS, HEAD_DIM),
                           lambda i, j, pt: (jnp.minimum(i * nj + j, n_events - 1), 0, 0, 0))
    y, km = pl.pallas_call(
        functools.partial(_ffn_kernel, n_pages=pps),
        grid_spec=pltpu.PrefetchScalarGridSpec(
            num_scalar_prefetch=1, grid=(ni, nj),
            in_specs=in_specs + [page_spec(r) for r in range(pps)],
            out_specs=[y_spec, km_spec]),
        out_shape=[y_shape, jax.ShapeDtypeStruct((n_events, bps, N_HEADS, HEAD_DIM), F32)],
        compiler_params=_params("arbitrary", "arbitrary"), name="ffn_kmean",
    )(page_table_flat, h2, wg_bf, wu_bf, wd_bf, x1, gf, *([cache_k] * pps))
    return y, km.reshape(n_events * bps, N_HEADS, HEAD_DIM)


def _select_sample_kernel(qh_ref, km_ref, idx_ref, sc_ref, *, n_seq, t_new, n_past_blocks):
    lanes = sc_ref.shape[1]
    lane = lax.broadcasted_iota(jnp.int32, sc_ref.shape, 1)
    sc_ref[...] = jnp.where(lane == n_past_blocks, NEG, -jnp.inf)
    for b in range(n_seq):
        for hd in range(N_HEADS):
            q_hl = jnp.concatenate(_split_bf16(qh_ref[hd, b * t_new:(b + 1) * t_new, :]), axis=0)
            km_hi, km_lo = _split_bf16(km_ref[b * n_past_blocks:(b + 1) * n_past_blocks, hd, :])
            by_hi = lax.dot_general(q_hl, km_hi, _NT, preferred_element_type=F32)
            r0 = (b * N_HEADS + hd) * t_new
            sc_ref[r0:r0 + t_new, 0:n_past_blocks] = (
                by_hi[0:t_new] + by_hi[t_new:2 * t_new]
                + lax.dot_general(q_hl[0:t_new], km_lo, _NT, preferred_element_type=F32))
    s = sc_ref[...]
    for r in range(MOBA_TOPK):
        mx = jnp.max(s, axis=-1, keepdims=True)
        am = jnp.min(jnp.where(s == mx, lane, lanes), axis=-1, keepdims=True)
        idx_ref[:, r:r + 1] = am
        s = jnp.where(lane == am, -jnp.inf, s)


def _select_sample(qh, km, n_seq, t_new, n_past_blocks):
    rows = n_seq * N_HEADS * t_new
    return pl.pallas_call(
        functools.partial(_select_sample_kernel, n_seq=n_seq, t_new=t_new, n_past_blocks=n_past_blocks),
        out_shape=jax.ShapeDtypeStruct((rows, MOBA_TOPK), jnp.int32),
        scratch_shapes=[pltpu.VMEM((rows, LANES), F32)],
        compiler_params=pltpu.CompilerParams(vmem_limit_bytes=VMEM_LIMIT_BYTES),
        name="select_sample",
    )(qh, km)


GATHER_DEPTH = 2


def _attn_sample_kernel(pt_ref, idx_ref, slopes_ref, qh_ref, kh_ref, vh_ref, ck_hbm, cv_hbm, o_ref,
                        kbuf, vbuf, sem, *, n_seq, t_new, n_pages):
    n_past_blocks = n_pages // PAGES_PER_BLOCK
    past_len = n_pages * PAGE_SIZE
    n_slots = t_new * MOBA_TOPK
    n_keys = n_slots * MOBA_BLOCK
    n_steps = n_seq * N_HEADS
    n_bufs = GATHER_DEPTH + 1
    b = pl.program_id(0)
    hd = pl.program_id(1)
    step = b * N_HEADS + hd

    def block_of(st, qj):
        return idx_ref[st * n_slots + qj]

    def gather(st, start):
        bb, hh, sl = st // N_HEADS, lax.rem(st, N_HEADS), lax.rem(st, n_bufs)
        for qj in range(n_slots):
            blk = jnp.minimum(block_of(st, qj), n_past_blocks - 1)
            for half in range(PAGES_PER_BLOCK):
                phys = pt_ref[bb * n_pages + blk * PAGES_PER_BLOCK + half]
                rows = pl.ds(qj * MOBA_BLOCK + half * PAGE_SIZE, PAGE_SIZE)
                for src, dst, s in ((ck_hbm, kbuf, 0), (cv_hbm, vbuf, 1)):
                    cp = pltpu.make_async_copy(src.at[0, phys, :, hh, :], dst.at[sl, rows, :], sem.at[s, sl])
                    if start:
                        cp.start(priority=s)
                    else:
                        cp.wait()

    @pl.when(step == 0)
    def _():
        for d in range(min(GATHER_DEPTH, n_steps)):
            gather(step + d, True)

    @pl.when(step + GATHER_DEPTH < n_steps)
    def _():
        gather(step + GATHER_DEPTH, True)

    gather(step, False)

    buf = lax.rem(step, n_bufs)
    slope = slopes_ref[hd]
    rows = pl.ds(b * t_new, t_new)
    qs = (qh_ref[rows, :] * ATTN_SCALE).astype(BF16)

    s = lax.dot_general(qs, kbuf[buf].astype(BF16), _NT, preferred_element_type=F32)
    row = lax.broadcasted_iota(jnp.int32, (t_new, n_keys), 0)
    col = lax.broadcasted_iota(jnp.int32, (t_new, n_keys), 1)
    shift = jnp.zeros((t_new, n_keys), jnp.int32)
    valid = jnp.zeros((t_new, n_keys), jnp.int32)
    for qj in range(n_slots):
        blk = block_of(step, qj)
        in_slot = jnp.logical_and(col >= qj * MOBA_BLOCK, col < (qj + 1) * MOBA_BLOCK)
        shift = jnp.where(in_slot, blk * MOBA_BLOCK - qj * MOBA_BLOCK, shift)
        valid = jnp.where(in_slot, (blk < n_past_blocks).astype(jnp.int32), valid)
    dist = (past_len + row - (col + shift)).astype(F32)
    q_lo = row * (MOBA_TOPK * MOBA_BLOCK)
    mine = jnp.logical_and(col >= q_lo, col < q_lo + MOBA_TOPK * MOBA_BLOCK)
    s = jnp.where(jnp.logical_and(mine, valid > 0), s - slope * dist, NEG)
    own_rel = (lax.broadcasted_iota(jnp.int32, (t_new, t_new), 0)
               - lax.broadcasted_iota(jnp.int32, (t_new, t_new), 1))
    s_own = lax.dot_general(qs, kh_ref[rows, :].astype(BF16), _NT, preferred_element_type=F32)
    s_own = jnp.where(own_rel >= 0, s_own - slope * own_rel.astype(F32), NEG)

    m = jnp.maximum(jnp.max(s_own, axis=-1, keepdims=True), jnp.max(s, axis=-1, keepdims=True))
    p_own = jnp.exp(s_own - m)
    p = jnp.exp(s - m)
    l = jnp.sum(p_own, axis=-1, keepdims=True) + jnp.sum(p, axis=-1, keepdims=True)
    acc = (jnp.dot(p_own.astype(BF16), vh_ref[rows, :].astype(BF16), preferred_element_type=F32)
           + jnp.dot(p.astype(BF16), vbuf[buf].astype(BF16), preferred_element_type=F32))
    o_ref[...] = acc / l


def _attn_sample(page_table_flat, idx_flat, slopes, qh, kh, vh, cache_k, cache_v, n_seq, t_new, n_pages):
    n_slots = t_new * MOBA_TOPK
    head_blk = pl.BlockSpec((None, n_seq * t_new, HEAD_DIM), lambda b, h, pt, ix, sl: (h, 0, 0))
    grid_spec = pltpu.PrefetchScalarGridSpec(
        num_scalar_prefetch=3,
        grid=(n_seq, N_HEADS),
        in_specs=[head_blk, head_blk, head_blk,
                  pl.BlockSpec(memory_space=pl.ANY), pl.BlockSpec(memory_space=pl.ANY)],
        out_specs=pl.BlockSpec((None, t_new, HEAD_DIM), lambda b, h, pt, ix, sl: (b, 0, h)),
        scratch_shapes=[
            pltpu.VMEM((GATHER_DEPTH + 1, n_slots * MOBA_BLOCK, HEAD_DIM), F32),
            pltpu.VMEM((GATHER_DEPTH + 1, n_slots * MOBA_BLOCK, HEAD_DIM), F32),
            pltpu.SemaphoreType.DMA((2, GATHER_DEPTH + 1)),
        ],
    )
    return pl.pallas_call(
        functools.partial(_attn_sample_kernel, n_seq=n_seq, t_new=t_new, n_pages=n_pages),
        grid_spec=grid_spec,
        out_shape=jax.ShapeDtypeStruct((n_seq, t_new, ATTN_WIDTH), F32),
        compiler_params=_params("arbitrary", "arbitrary"),
        name="attn_sample",
    )(page_table_flat, idx_flat, slopes, qh, kh, vh, cache_k, cache_v)


def kernel(x_prompt, x_sample, cache_k, cache_v, state_pool, page_table, norm1_g, w_in, w_pool, pool_scale,
           w_out, norm2_g, w_gate, w_up, w_down, norm_f_g):
    depth = w_in.shape[0]
    assert depth == 1, "single-layer step"
    batch, seq, _ = x_prompt.shape
    n_seq, t_new, _ = x_sample.shape
    n_pages = page_table.shape[1]
    past_len = n_pages * PAGE_SIZE
    assert seq % MOBA_BLOCK == 0 and past_len % MOBA_BLOCK == 0 and t_new <= MOBA_BLOCK

    slopes = jnp.asarray(_alibi_slopes())
    g1 = norm1_g[0].reshape(1, D_MODEL)
    g2 = norm2_g[0].reshape(1, D_MODEL)
    gf = norm_f_g.reshape(1, D_MODEL)
    ps = pool_scale[0].reshape(1, POOL_WIDTH)
    w_pool_bf = w_pool[0].astype(BF16)

    xp = x_prompt.reshape(batch * seq, D_MODEL)
    pt_flat = page_table.reshape(-1)
    k, v, u, qh_p, kbh, vbh, km = _proj_prompt(xp, g1, w_in[0])
    km = km.reshape(batch, seq // MOBA_BLOCK, ATTN_WIDTH)
    a, (w_out_bf, w_gate_bf, w_up_bf, w_down_bf) = _attn_prompt(
        qh_p, kbh, vbh, km, _alibi_key_table(seq), batch, seq,
        (w_out[0], w_gate[0], w_up[0], w_down[0]))
    x1, h2, hist_p = _mid_prompt(a, u, xp, w_pool_bf, ps, w_out_bf, g2, batch, seq)
    y_p, km_s = _ffn(h2, w_gate_bf, w_up_bf, w_down_bf, x1, gf, tm=512, paged_k=(pt_flat, cache_k))

    xs = x_sample.reshape(n_seq * t_new, D_MODEL)
    k_s, v_s, u_s, qh, kh, vh = _proj_sample(xs, g1, w_in[0])
    idx = _select_sample(qh, km_s, n_seq, t_new, n_pages // PAGES_PER_BLOCK)
    a_s = _attn_sample(pt_flat, idx.reshape(-1), slopes, qh, kh, vh, cache_k, cache_v, n_seq, t_new, n_pages)
    x1_s, h2_s, hist_s = _mid_sample(a_s.reshape(n_seq * t_new, ATTN_WIDTH), u_s, state_pool[0], xs,
                                     w_pool_bf, ps, w_out_bf, g2, n_seq, t_new, past_len)
    y_s = _ffn(h2_s, w_gate_bf, w_up_bf, w_down_bf, x1_s, gf, tm=n_seq * t_new)

    kv_p = (depth, batch, seq, N_HEADS, HEAD_DIM)
    kv_s = (depth, n_seq, t_new, N_HEADS, HEAD_DIM)
    return (y_p.reshape(batch, seq, D_MODEL), y_s.reshape(n_seq, t_new, D_MODEL),
            k.reshape(kv_p), v.reshape(kv_p), hist_p[None],
            k_s.reshape(kv_s), v_s.reshape(kv_s), hist_s[None])
```

```python
import functools

import jax
import jax.numpy as jnp
import numpy as np
from jax import lax
from jax.experimental import pallas as pl
from jax.experimental.pallas import tpu as pltpu

D_MODEL = 2048
HEAD_DIM = 128
N_HEADS = 8
ATTN_WIDTH = N_HEADS * HEAD_DIM
POOL_WIDTH = D_MODEL - ATTN_WIDTH
IN_WIDTH = 3 * ATTN_WIDTH + POOL_WIDTH
MOBA_BLOCK = 256
MOBA_TOPK = 3
PAGE_SIZE = 128
PAGES_PER_BLOCK = MOBA_BLOCK // PAGE_SIZE
POOL_WINDOWS = (2, 4, 8, 16)
POOL_GROUP_W = POOL_WIDTH // len(POOL_WINDOWS)
POOL_HIST = max(POOL_WINDOWS) - 1
LANES = 128
SUBLANES = 8
BF16_SUBLANES = 16
HIST_PAD = 2 * SUBLANES
RMS_EPS = 1e-6
NEG = -1e30
ATTN_SCALE = HEAD_DIM ** -0.5
LOG2E = 1.4426950408889634

V7X_VMEM_BYTES = 64 * 1024 * 1024
VMEM_LIMIT_BYTES = V7X_VMEM_BYTES - 8 * 1024 * 1024

F32 = jnp.float32
BF16 = jnp.bfloat16
_NT = (((1,), (1,)), ((), ()))


def _rmsnorm(x, g):
    y = x * lax.rsqrt(jnp.mean(x * x, axis=-1, keepdims=True) + RMS_EPS)
    return y * g


def _params(*semantics):
    return pltpu.CompilerParams(dimension_semantics=semantics, vmem_limit_bytes=VMEM_LIMIT_BYTES)


W_CAST_CHUNKS = 8


def _proj_prompt_kernel(x_ref, g_ref, w_hbm, k_ref, v_ref, u_ref, qh_ref, kbh_ref, vbh_ref, km_ref,
                        w_ref, stage_ref, sem, *, tm):
    @pl.when(pl.program_id(0) == 0)
    def _():
        rows = D_MODEL // W_CAST_CHUNKS

        def chunk_copy(c):
            return pltpu.make_async_copy(w_hbm.at[pl.ds(c * rows, rows), :], stage_ref.at[c % 2], sem.at[c % 2])

        chunk_copy(0).start()
        for c in range(W_CAST_CHUNKS):
            if c + 1 < W_CAST_CHUNKS:
                chunk_copy(c + 1).start()
            chunk_copy(c).wait()
            w_ref[c * rows:(c + 1) * rows, :] = stage_ref[c % 2].astype(BF16)

    h = _rmsnorm(x_ref[...], g_ref[...]).astype(BF16)

    def seg(s):
        return jnp.dot(h, w_ref[:, s * ATTN_WIDTH:(s + 1) * ATTN_WIDTH], preferred_element_type=F32)

    zq = seg(0)
    for hd in range(N_HEADS):
        qh_ref[hd] = zq[:, hd * HEAD_DIM:(hd + 1) * HEAD_DIM]
    zk = seg(1)
    k_ref[...] = zk
    for hd in range(N_HEADS):
        kbh_ref[hd] = zk[:, hd * HEAD_DIM:(hd + 1) * HEAD_DIM].astype(BF16)
    for blk in range(tm // MOBA_BLOCK):
        rows = zk[blk * MOBA_BLOCK:(blk + 1) * MOBA_BLOCK]
        km_ref[blk] = jnp.sum(rows, axis=0, keepdims=True) * (1.0 / MOBA_BLOCK)
    zv = seg(2)
    v_ref[...] = zv
    for hd in range(N_HEADS):
        vbh_ref[hd] = zv[:, hd * HEAD_DIM:(hd + 1) * HEAD_DIM].astype(BF16)
    u_ref[...] = seg(3)


def _proj_prompt(x, g, w, tm=256):
    m = x.shape[0]
    row = lambda i: (i, 0)
    f32_out = jax.ShapeDtypeStruct((m, ATTN_WIDTH), F32)
    km_out = jax.ShapeDtypeStruct((m // MOBA_BLOCK, 1, ATTN_WIDTH), F32)
    blk = pl.BlockSpec((tm, ATTN_WIDTH), row)
    hm_blk = pl.BlockSpec((N_HEADS, tm, HEAD_DIM), lambda i: (0, i, 0))
    return pl.pallas_call(
        functools.partial(_proj_prompt_kernel, tm=tm),
        grid=(m // tm,),
        in_specs=[
            pl.BlockSpec((tm, D_MODEL), row),
            pl.BlockSpec((1, D_MODEL), lambda i: (0, 0)),
            pl.BlockSpec(memory_space=pl.ANY),
        ],
        out_specs=[blk, blk, blk, hm_blk, hm_blk, hm_blk,
                   pl.BlockSpec((tm // MOBA_BLOCK, 1, ATTN_WIDTH), lambda i: (i, 0, 0))],
        out_shape=[f32_out, f32_out, f32_out,
                   jax.ShapeDtypeStruct((N_HEADS, m, HEAD_DIM), F32),
                   jax.ShapeDtypeStruct((N_HEADS, m, HEAD_DIM), BF16),
                   jax.ShapeDtypeStruct((N_HEADS, m, HEAD_DIM), BF16), km_out],
        scratch_shapes=[
            pltpu.VMEM((D_MODEL, IN_WIDTH), BF16),
            pltpu.VMEM((2, D_MODEL // W_CAST_CHUNKS, IN_WIDTH), F32),
            pltpu.SemaphoreType.DMA((2,)),
        ],
        compiler_params=_params("arbitrary"),
        name="proj_prompt",
    )(x, g, w)


def _proj_sample_kernel(x_ref, g_ref, w_ref, k_ref, v_ref, u_ref, qh_ref, kh_ref, vh_ref):
    seg = pl.program_id(0)
    h = _rmsnorm(x_ref[...], g_ref[...]).astype(BF16)
    z = jnp.dot(h, w_ref[...].astype(BF16), preferred_element_type=F32)

    def head_major(o_ref):
        for hd in range(N_HEADS):
            o_ref[hd] = z[:, hd * HEAD_DIM:(hd + 1) * HEAD_DIM]

    @pl.when(seg == 0)
    def _():
        head_major(qh_ref)

    @pl.when(seg == 1)
    def _():
        k_ref[...] = z
        head_major(kh_ref)

    @pl.when(seg == 2)
    def _():
        v_ref[...] = z
        head_major(vh_ref)

    @pl.when(seg == 3)
    def _():
        u_ref[...] = z


def _proj_sample(x, g, w):
    m = x.shape[0]
    f32_out = jax.ShapeDtypeStruct((m, ATTN_WIDTH), F32)
    hm_out = jax.ShapeDtypeStruct((N_HEADS, m, HEAD_DIM), F32)
    whole2 = pl.BlockSpec((m, ATTN_WIDTH), lambda s: (0, 0))
    whole3 = pl.BlockSpec((N_HEADS, m, HEAD_DIM), lambda s: (0, 0, 0))
    return pl.pallas_call(
        _proj_sample_kernel,
        grid=(IN_WIDTH // ATTN_WIDTH,),
        in_specs=[
            pl.BlockSpec((m, D_MODEL), lambda s: (0, 0)),
            pl.BlockSpec((1, D_MODEL), lambda s: (0, 0)),
            pl.BlockSpec((D_MODEL, ATTN_WIDTH), lambda s: (0, s)),
        ],
        out_specs=[whole2, whole2, whole2, whole3, whole3, whole3],
        out_shape=[f32_out, f32_out, f32_out, hm_out, hm_out, hm_out],
        compiler_params=_params("arbitrary"),
        name="proj_sample",
    )(x, g, w)


def _alibi_slopes():
    return np.exp2(-8.0 * np.arange(1, N_HEADS + 1, dtype=np.float32) / N_HEADS)


def _bf16_parts(x, n=3):
    parts = []
    for _ in range(n):
        p = float(np.float32(x).astype(BF16))
        parts.append(p)
        x = x - p
    return parts


def _alibi_key_table(seq):
    nb = seq // MOBA_BLOCK
    t = np.arange(seq)
    blk, local = t // MOBA_BLOCK, t % MOBA_BLOCK
    table = np.zeros((N_HEADS, seq, HEAD_DIM), np.float32)
    table[:, t, blk] = 1.0
    for c in range(3):
        table[:, :, nb + c] = _alibi_slopes()[:, None] * local[None, :]
        table[:, :, nb + 3 + c] = _alibi_slopes()[:, None] * (blk * MOBA_BLOCK)[None, :]
    return jnp.asarray(table.astype(BF16))


def _split_bf16(x):
    hi = x.astype(BF16)
    return hi, (x - hi.astype(F32)).astype(BF16)


def _topk_block_bias_t(scores_t, n_past):
    nb = scores_t.shape[0]
    n_iota = lax.broadcasted_iota(jnp.int32, scores_t.shape, 0)
    past = n_iota < n_past
    s = jnp.where(past, scores_t, NEG)
    rank = jnp.zeros(scores_t.shape, jnp.int32)
    for m in range(nb):
        sm = s[m:m + 1, :]
        tie_lower = jnp.logical_and(sm == s, n_iota > m)
        rank = rank + jnp.where(sm > s, 1, jnp.where(tie_lower, 1, 0))
    keep = jnp.logical_or(jnp.logical_and(past, rank < MOBA_TOPK), n_iota == n_past)
    return jnp.where(keep, 0.0, NEG)


def _attn_prompt_kernel(q_ref, kb_ref, vb_ref, km_ref, kx_ref, *refs, nb, n_cast):
    w_refs, o_ref, wbf_refs = refs[:n_cast], refs[n_cast], refs[n_cast + 1:2 * n_cast + 1]
    qa_ref, ka_ref, va_ref, s_ref, p_ref, acc_ref = refs[2 * n_cast + 1:]
    for w_ref, wbf_ref in zip(w_refs, wbf_refs):
        wbf_ref[...] = w_ref[...].astype(BF16)
    seq = nb * MOBA_BLOCK
    km_hl = jnp.concatenate(_split_bf16(km_ref[...]), axis=0)
    shape = (MOBA_BLOCK, MOBA_BLOCK)
    causal = lax.broadcasted_iota(jnp.int32, shape, 0) >= lax.broadcasted_iota(jnp.int32, shape, 1)

    ka_ref[:, 0:HEAD_DIM] = kb_ref[...]
    ka_ref[:, HEAD_DIM:] = kx_ref[...]
    va_ref[:, 0:HEAD_DIM] = vb_ref[...]
    va_ref[:, HEAD_DIM:] = jnp.where(lax.broadcasted_iota(jnp.int32, (seq, HEAD_DIM), 1) == 0, 1.0, 0.0).astype(BF16)
    sub = lax.broadcasted_iota(jnp.int32, (SUBLANES, MOBA_BLOCK), 0)
    c_hi, c_mid, c_lo = _bf16_parts(LOG2E)
    third = lax.rem(sub, 3)
    coef_rows = jnp.where(sub < 6, jnp.where(third == 0, c_hi, jnp.where(third == 1, c_mid, c_lo)), 0.0)
    zero_rows = jnp.zeros((HEAD_DIM - nb - SUBLANES, MOBA_BLOCK), F32)
    for qi in range(nb):
        rows = slice(qi * MOBA_BLOCK, (qi + 1) * MOBA_BLOCK)
        q = q_ref[rows, :]
        q_hi, q_lo = _split_bf16(q)
        sel_hl = lax.dot_general(km_hl, q_hi, _NT, preferred_element_type=F32)
        sel_t = (sel_hl[0:nb] + sel_hl[nb:2 * nb]
                 + lax.dot_general(km_hl[0:nb], q_lo, _NT, preferred_element_type=F32))
        bias_t = _topk_block_bias_t(sel_t, qi)
        qa_ref[rows, 0:HEAD_DIM] = (q * (ATTN_SCALE * LOG2E)).astype(BF16)
        qa_ref[rows, HEAD_DIM:] = jnp.concatenate([bias_t, coef_rows, zero_rows], axis=0).T.astype(BF16)

    for n in range(nb):
        r0 = n * MOBA_BLOCK
        s = lax.dot_general(qa_ref[r0:seq, :], ka_ref[r0:r0 + MOBA_BLOCK, :], _NT, preferred_element_type=F32)
        s_ref[n, r0:r0 + MOBA_BLOCK, :] = jnp.where(causal, s[0:MOBA_BLOCK], NEG)
        if r0 + MOBA_BLOCK < seq:
            s_ref[n, r0 + MOBA_BLOCK:seq, :] = s[MOBA_BLOCK:]

    for qi in range(nb):
        rows = slice(qi * MOBA_BLOCK, (qi + 1) * MOBA_BLOCK)
        m = jnp.max(s_ref[0, rows, :], axis=-1, keepdims=True)
        for n in range(1, qi + 1):
            m = jnp.maximum(m, jnp.max(s_ref[n, rows, :], axis=-1, keepdims=True))
        for n in range(qi + 1):
            p_ref[n, rows, :] = jnp.exp2(s_ref[n, rows, :] - m).astype(BF16)

    for n in range(nb):
        r0 = n * MOBA_BLOCK
        pv = jnp.dot(p_ref[n, r0:seq, :], va_ref[r0:r0 + MOBA_BLOCK, :], preferred_element_type=F32)
        if n == 0:
            acc_ref[...] = pv
        else:
            acc_ref[r0:seq, :] += pv
    for qi in range(nb):
        rows = slice(qi * MOBA_BLOCK, (qi + 1) * MOBA_BLOCK)
        inv_l = 1.0 / acc_ref[rows, HEAD_DIM:HEAD_DIM + 1]
        o_ref[rows, :] = (acc_ref[rows, 0:HEAD_DIM] * inv_l).astype(o_ref.dtype)


def _attn_prompt(qh, kbh, vbh, km, kx, batch, seq, cast_weights):
    nb = seq // MOBA_BLOCK
    assert nb == SUBLANES, "bias rows fill one sublane tile"
    steps = batch * N_HEADS
    slab = pl.BlockSpec((None, seq, HEAD_DIM), lambda b, h: (h, b, 0))
    chunk_specs = []
    for w in cast_weights:
        assert w.shape[0] % (BF16_SUBLANES * steps) == 0, "bf16 row chunks must be whole (16, 128) tiles"
        chunk_specs.append(pl.BlockSpec((w.shape[0] // steps, w.shape[1]), lambda b, h: (b * N_HEADS + h, 0)))
    res = pl.pallas_call(
        functools.partial(_attn_prompt_kernel, nb=nb, n_cast=len(cast_weights)),
        grid=(batch, N_HEADS),
        in_specs=[
            slab, slab, slab,
            pl.BlockSpec((None, nb, HEAD_DIM), lambda b, h: (b, 0, h)),
            pl.BlockSpec((None, seq, HEAD_DIM), lambda b, h: (h, 0, 0)),
        ] + chunk_specs,
        out_specs=[slab] + chunk_specs,
        out_shape=[jax.ShapeDtypeStruct((N_HEADS, batch * seq, HEAD_DIM), BF16)]
        + [jax.ShapeDtypeStruct(w.shape, BF16) for w in cast_weights],
        scratch_shapes=[
            pltpu.VMEM((seq, 2 * HEAD_DIM), BF16),
            pltpu.VMEM((seq, 2 * HEAD_DIM), BF16),
            pltpu.VMEM((seq, 2 * HEAD_DIM), BF16),
            pltpu.VMEM((nb, seq, MOBA_BLOCK), F32),
            pltpu.VMEM((nb, seq, MOBA_BLOCK), BF16),
            pltpu.VMEM((seq, 2 * HEAD_DIM), F32),
        ],
        compiler_params=_params("arbitrary", "arbitrary"),
        name="attn_prompt",
    )(qh, kbh, vbh, km, kx, *cast_weights)
    return res[0], res[1:]


def _pool_diff(ext_ref, r0, n, pos, d_ref, d0):
    for g, w in enumerate(POOL_WINDOWS):
        ls = slice(g * POOL_GROUP_W, (g + 1) * POOL_GROUP_W)
        cur = ext_ref[pl.ds(r0, n), ls]
        win = cur
        for j in range(1, w):
            win = win + ext_ref[pl.ds(r0 - j, n), ls]
        cnt = jnp.minimum(pos + 1, w).astype(F32)
        d_ref[pl.ds(d0, n), ls] = win / cnt - cur


def _mix_out(a_bf, d_ref, wp_ref, ps_ref, wo_ref):
    out = jnp.dot(a_bf, wo_ref[0:ATTN_WIDTH, :], preferred_element_type=F32)
    ys = []
    for g in range(len(POOL_WINDOWS)):
        ls = slice(g * POOL_GROUP_W, (g + 1) * POOL_GROUP_W)
        y = jnp.dot(d_ref[:, ls].astype(BF16), wp_ref[g], preferred_element_type=F32)
        ys.append((y * ps_ref[:, ls]).astype(BF16))
    return out + jnp.dot(jnp.concatenate(ys, axis=-1), wo_ref[ATTN_WIDTH:, :], preferred_element_type=F32)


POOL_PAD = HIST_PAD + SUBLANES


def _pool_diff_tile(ext_ref, t1_ref, t2_ref, tm, pos, d_ref):
    lo = SUBLANES
    n_all = POOL_PAD - lo + tm
    t1_ref[0:lo, :] = jnp.zeros((lo, POOL_GROUP_W), F32)
    t2_ref[0:lo, :] = jnp.zeros((lo, POOL_GROUP_W), F32)
    for g, w in enumerate(POOL_WINDOWS):
        ls = slice(g * POOL_GROUP_W, (g + 1) * POOL_GROUP_W)
        src, lanes, k, stage = ext_ref, ls, 1, 0
        while 2 * k < w:
            dst = (t1_ref, t2_ref)[stage % 2]
            dst[pl.ds(lo, n_all), :] = src[pl.ds(lo, n_all), lanes] + src[pl.ds(lo - k, n_all), lanes]
            src, lanes, k, stage = dst, slice(None), 2 * k, stage + 1
        win = src[pl.ds(POOL_PAD, tm), lanes] + src[pl.ds(POOL_PAD - k, tm), lanes]
        cnt = jnp.minimum(pos + 1, w).astype(F32)
        d_ref[:, ls] = win / cnt - ext_ref[pl.ds(POOL_PAD, tm), ls]


def _mid_prompt_kernel(a_ref, u_ref, uprev_ref, x_ref, wp_ref, ps_ref, wo_ref, g2_ref,
                       x1_ref, h2_ref, hist_ref, ext_ref, d_ref, t1_ref, t2_ref, *, tm, tiles_per_seq):
    t_in_seq = lax.rem(pl.program_id(0), tiles_per_seq)
    first = POOL_PAD - HIST_PAD
    ext_ref[0:first, :] = jnp.zeros((first, POOL_WIDTH), F32)
    ext_ref[first:POOL_PAD, :] = jnp.where(t_in_seq == 0, 0.0, uprev_ref[...])
    ext_ref[POOL_PAD:POOL_PAD + tm, :] = u_ref[...]
    pos = t_in_seq * tm + lax.broadcasted_iota(jnp.int32, (tm, 1), 0)
    _pool_diff_tile(ext_ref, t1_ref, t2_ref, tm, pos, d_ref)
    a = jnp.concatenate([a_ref[hd] for hd in range(N_HEADS)], axis=-1)
    x1 = x_ref[...] + _mix_out(a, d_ref, wp_ref, ps_ref, wo_ref)
    x1_ref[...] = x1
    h2_ref[...] = _rmsnorm(x1, g2_ref[...]).astype(BF16)

    @pl.when(t_in_seq == tiles_per_seq - 1)
    def _():
        hist_ref[...] = ext_ref[pl.ds(POOL_PAD + tm - POOL_HIST, POOL_HIST), :]


def _mid_prompt(a, u, x, wp_bf, ps, wo_bf, g2, batch, seq, tm=512):
    m = x.shape[0]
    tiles_per_seq = seq // tm
    row = lambda i: (i, 0)
    const2 = lambda i: (0, 0)
    return pl.pallas_call(
        functools.partial(_mid_prompt_kernel, tm=tm, tiles_per_seq=tiles_per_seq),
        grid=(m // tm,),
        in_specs=[
            pl.BlockSpec((N_HEADS, tm, HEAD_DIM), lambda i: (0, i, 0)),
            pl.BlockSpec((tm, POOL_WIDTH), row),
            pl.BlockSpec((HIST_PAD, POOL_WIDTH), lambda i: (jnp.maximum(i * (tm // HIST_PAD) - 1, 0), 0)),
            pl.BlockSpec((tm, D_MODEL), row),
            pl.BlockSpec(wp_bf.shape, lambda i: (0, 0, 0), pipeline_mode=pl.Buffered(1)),
            pl.BlockSpec((1, POOL_WIDTH), const2),
            pl.BlockSpec((D_MODEL, D_MODEL), const2, pipeline_mode=pl.Buffered(1)),
            pl.BlockSpec((1, D_MODEL), const2),
        ],
        out_specs=[
            pl.BlockSpec((tm, D_MODEL), row),
            pl.BlockSpec((tm, D_MODEL), row),
            pl.BlockSpec((None, POOL_HIST, POOL_WIDTH), lambda i: (i // tiles_per_seq, 0, 0)),
        ],
        out_shape=[
            jax.ShapeDtypeStruct((m, D_MODEL), F32),
            jax.ShapeDtypeStruct((m, D_MODEL), BF16),
            jax.ShapeDtypeStruct((batch, POOL_HIST, POOL_WIDTH), F32),
        ],
        scratch_shapes=[pltpu.VMEM((POOL_PAD + tm, POOL_WIDTH), F32), pltpu.VMEM((tm, POOL_WIDTH), F32),
                        pltpu.VMEM((POOL_PAD + tm, POOL_GROUP_W), F32), pltpu.VMEM((POOL_PAD + tm, POOL_GROUP_W), F32)],
        compiler_params=_params("arbitrary"),
        name="mid_prompt",
    )(a, u, u, x, wp_bf, ps, wo_bf, g2)


def _mid_sample_kernel(a_ref, u_ref, hist_ref, x_ref, wp_ref, ps_ref, wo_ref, g2_ref,
                       x1_ref, h2_ref, newhist_ref, ext_ref, d_ref, *, n_seq, t_new, start_pos):
    stride = POOL_HIST + t_new
    stride += -stride % SUBLANES
    pos = start_pos + lax.broadcasted_iota(jnp.int32, (t_new, 1), 0)
    for b in range(n_seq):
        base = b * stride
        ext_ref[base:base + POOL_HIST, :] = hist_ref[b]
        ext_ref[base + POOL_HIST:base + POOL_HIST + t_new, :] = u_ref[b * t_new:(b + 1) * t_new, :]
        _pool_diff(ext_ref, base + POOL_HIST, t_new, pos, d_ref, b * t_new)
        newhist_ref[b] = ext_ref[pl.ds(base + t_new, POOL_HIST), :]
    x1 = x_ref[...] + _mix_out(a_ref[...].astype(BF16), d_ref, wp_ref, ps_ref, wo_ref)
    x1_ref[...] = x1
    h2_ref[...] = _rmsnorm(x1, g2_ref[...]).astype(BF16)


def _mid_sample(a, u, hist, x, wp_bf, ps, wo_bf, g2, n_seq, t_new, start_pos):
    m = x.shape[0]
    stride = POOL_HIST + t_new
    stride += -stride % SUBLANES
    return pl.pallas_call(
        functools.partial(_mid_sample_kernel, n_seq=n_seq, t_new=t_new, start_pos=start_pos),
        out_shape=[
            jax.ShapeDtypeStruct((m, D_MODEL), F32),
            jax.ShapeDtypeStruct((m, D_MODEL), BF16),
            jax.ShapeDtypeStruct((n_seq, POOL_HIST, POOL_WIDTH), F32),
        ],
        scratch_shapes=[pltpu.VMEM((n_seq * stride, POOL_WIDTH), F32), pltpu.VMEM((m, POOL_WIDTH), F32)],
        compiler_params=pltpu.CompilerParams(vmem_limit_bytes=VMEM_LIMIT_BYTES),
        name="mid_sample",
    )(a, u, hist, x, wp_bf, ps, wo_bf, g2)


def _store_block_mean(o_ref, jb, tot):
    mean = tot * (1.0 / MOBA_BLOCK)
    for hd in range(N_HEADS):
        o_ref[jb:jb + 1, hd * HEAD_DIM:(hd + 1) * HEAD_DIM] = mean[hd:hd + 1, :]


def _ffn_kernel(*refs, n_pages):
    if n_pages:
        refs = refs[1:]
    h2_ref, wg_ref, wu_ref, wd_ref, x1_ref, gf_ref = refs[:6]
    page_refs = refs[6:6 + n_pages]
    y_ref = refs[6 + n_pages]
    km_ref = refs[7 + n_pages] if n_pages else None
    j = pl.program_id(1)

    @pl.when(j == 0)
    def _():
        y_ref[...] = x1_ref[...]

    n_slots = 6
    sums = {}

    def side_job(slot):
        for r in range(slot, n_pages, n_slots):
            sums[r] = jnp.sum(page_refs[r][...], axis=0)
            jb, pos = divmod(r, PAGES_PER_BLOCK)
            if pos == PAGES_PER_BLOCK - 1:
                tot = sums[jb * PAGES_PER_BLOCK]
                for rr in range(jb * PAGES_PER_BLOCK + 1, r + 1):
                    tot = tot + sums[rr]
                _store_block_mean(km_ref, jb, tot)

    h2 = h2_ref[...]
    tf = wg_ref.shape[1]
    gate, up = [], []
    for c, cols in enumerate((slice(0, tf // 2), slice(tf // 2, tf))):
        gate.append(jnp.dot(h2, wg_ref[:, cols], preferred_element_type=F32))
        side_job(c)
    for c, cols in enumerate((slice(0, tf // 2), slice(tf // 2, tf))):
        up.append(jnp.dot(h2, wu_ref[:, cols], preferred_element_type=F32))
        side_job(2 + c)
    gate = jnp.concatenate(gate, axis=-1)
    act = (gate * jax.nn.sigmoid(gate) * jnp.concatenate(up, axis=-1)).astype(BF16)
    for c, cols in enumerate((slice(0, D_MODEL // 2), slice(D_MODEL // 2, D_MODEL))):
        y_ref[:, cols] += jnp.dot(act, wd_ref[:, cols], preferred_element_type=F32)
        side_job(4 + c)

    @pl.when(j == pl.num_programs(1) - 1)
    def _():
        y_ref[...] = _rmsnorm(y_ref[...], gf_ref[...])


def _ffn(h2, wg_bf, wu_bf, wd_bf, x1, gf, tm, tf=512, paged_k=None):
    m = h2.shape[0]
    d_ff = wg_bf.shape[1]
    ni, nj = m // tm, d_ff // tf
    in_specs = [
        pl.BlockSpec((tm, D_MODEL), lambda i, j, *_: (i, 0)),
        pl.BlockSpec((D_MODEL, tf), lambda i, j, *_: (0, j)),
        pl.BlockSpec((D_MODEL, tf), lambda i, j, *_: (0, j)),
        pl.BlockSpec((tf, D_MODEL), lambda i, j, *_: (j, 0)),
        pl.BlockSpec((tm, D_MODEL), lambda i, j, *_: (i, 0)),
        pl.BlockSpec((1, D_MODEL), lambda i, j, *_: (0, 0)),
    ]
    y_spec = pl.BlockSpec((tm, D_MODEL), lambda i, j, *_: (i, 0))
    y_shape = jax.ShapeDtypeStruct((m, D_MODEL), F32)
    if paged_k is None:
        return pl.pallas_call(
            functools.partial(_ffn_kernel, n_pages=0),
            grid=(ni, nj), in_specs=in_specs, out_specs=y_spec, out_shape=y_shape,
            compiler_params=_params("arbitrary", "arbitrary"), name="ffn",
        )(h2, wg_bf, wu_bf, wd_bf, x1, gf)

    page_table_flat, cache_k = paged_k
    total = page_table_flat.shape[0]
    pps = PAGES_PER_BLOCK * pl.cdiv(total, PAGES_PER_BLOCK * ni * nj)
    n_events = pl.cdiv(total, pps)

    def page_spec(r):
        def index_map(i, j, pt):
            event = jnp.minimum(i * nj + j, n_events - 1)
            return (0, pt[jnp.minimum(event * pps + r, total - 1)], 0, 0, 0)
        return pl.BlockSpec((None, None, PAGE_SIZE, N_HEADS, HEAD_DIM), index_map)

    bps = pps // PAGES_PER_BLOCK
    km_spec = pl.BlockSpec((None, bps, ATTN_WIDTH), lambda i, j, pt: (jnp.minimum(i * nj + j, n_events - 1), 0, 0))
    y, km = pl.pallas_call(
        functools.partial(_ffn_kernel, n_pages=pps),
        grid_spec=pltpu.PrefetchScalarGridSpec(
            num_scalar_prefetch=1, grid=(ni, nj),
            in_specs=in_specs + [page_spec(r) for r in range(pps)],
            out_specs=[y_spec, km_spec]),
        out_shape=[y_shape, jax.ShapeDtypeStruct((n_events, bps, ATTN_WIDTH), F32)],
        compiler_params=_params("arbitrary", "arbitrary"), name="ffn_kmean",
    )(page_table_flat, h2, wg_bf, wu_bf, wd_bf, x1, gf, *([cache_k] * pps))
    return y, km.reshape(n_events * bps, ATTN_WIDTH)[:total // PAGES_PER_BLOCK]


def _select_sample_kernel(qh_ref, km_ref, idx_ref, sc_ref, *, n_seq, t_new, n_past_blocks):
    lanes = sc_ref.shape[1]
    lane = lax.broadcasted_iota(jnp.int32, sc_ref.shape, 1)
    sc_ref[...] = jnp.where(lane == n_past_blocks, NEG, -jnp.inf)
    for b in range(n_seq):
        for hd in range(N_HEADS):
            q_hl = jnp.concatenate(_split_bf16(qh_ref[hd, b * t_new:(b + 1) * t_new, :]), axis=0)
            km_hi, km_lo = _split_bf16(km_ref[b, :, hd * HEAD_DIM:(hd + 1) * HEAD_DIM])
            by_hi = lax.dot_general(q_hl, km_hi, _NT, preferred_element_type=F32)
            r0 = (b * N_HEADS + hd) * t_new
            sc_ref[r0:r0 + t_new, 0:n_past_blocks] = (
                by_hi[0:t_new] + by_hi[t_new:2 * t_new]
                + lax.dot_general(q_hl[0:t_new], km_lo, _NT, preferred_element_type=F32))
    s = sc_ref[...]
    for r in range(MOBA_TOPK):
        mx = jnp.max(s, axis=-1, keepdims=True)
        am = jnp.min(jnp.where(s == mx, lane, lanes), axis=-1, keepdims=True)
        idx_ref[:, r:r + 1] = am
        s = jnp.where(lane == am, -jnp.inf, s)


def _select_sample(qh, km, n_seq, t_new):
    n_past_blocks = km.shape[1]
    rows = n_seq * N_HEADS * t_new
    return pl.pallas_call(
        functools.partial(_select_sample_kernel, n_seq=n_seq, t_new=t_new, n_past_blocks=n_past_blocks),
        out_shape=jax.ShapeDtypeStruct((rows, MOBA_TOPK), jnp.int32),
        scratch_shapes=[pltpu.VMEM((rows, LANES), F32)],
        compiler_params=pltpu.CompilerParams(vmem_limit_bytes=VMEM_LIMIT_BYTES),
        name="select_sample",
    )(qh, km)


GATHER_DEPTH = 2


def _attn_sample_kernel(pt_ref, idx_ref, slopes_ref, qh_ref, kh_ref, vh_ref, ck_hbm, cv_hbm, o_ref,
                        kbuf, vbuf, sem, *, n_seq, t_new, n_pages):
    n_past_blocks = n_pages // PAGES_PER_BLOCK
    past_len = n_pages * PAGE_SIZE
    n_slots = t_new * MOBA_TOPK
    n_keys = n_slots * MOBA_BLOCK
    n_steps = n_seq * N_HEADS
    n_bufs = GATHER_DEPTH + 1
    b = pl.program_id(0)
    hd = pl.program_id(1)
    step = b * N_HEADS + hd

    def block_of(st, qj):
        return idx_ref[st * n_slots + qj]

    def gather(st, start):
        bb, hh, sl = st // N_HEADS, lax.rem(st, N_HEADS), lax.rem(st, n_bufs)
        for qj in range(n_slots):
            blk = jnp.minimum(block_of(st, qj), n_past_blocks - 1)
            for half in range(PAGES_PER_BLOCK):
                phys = pt_ref[bb * n_pages + blk * PAGES_PER_BLOCK + half]
                rows = pl.ds(qj * MOBA_BLOCK + half * PAGE_SIZE, PAGE_SIZE)
                for src, dst, s in ((ck_hbm, kbuf, 0), (cv_hbm, vbuf, 1)):
                    cp = pltpu.make_async_copy(src.at[0, phys, :, hh, :], dst.at[sl, rows, :], sem.at[s, sl])
                    if start:
                        cp.start()
                    else:
                        cp.wait()

    @pl.when(step == 0)
    def _():
        for d in range(min(GATHER_DEPTH, n_steps)):
            gather(step + d, True)

    @pl.when(step + GATHER_DEPTH < n_steps)
    def _():
        gather(step + GATHER_DEPTH, True)

    gather(step, False)

    buf = lax.rem(step, n_bufs)
    slope = slopes_ref[hd]
    rows = pl.ds(b * t_new, t_new)
    qs = (qh_ref[rows, :] * ATTN_SCALE).astype(BF16)

    s = lax.dot_general(qs, kbuf[buf].astype(BF16), _NT, preferred_element_type=F32)
    row = lax.broadcasted_iota(jnp.int32, (t_new, n_keys), 0)
    col = lax.broadcasted_iota(jnp.int32, (t_new, n_keys), 1)
    shift = jnp.zeros((t_new, n_keys), jnp.int32)
    valid = jnp.zeros((t_new, n_keys), jnp.int32)
    for qj in range(n_slots):
        blk = block_of(step, qj)
        in_slot = jnp.logical_and(col >= qj * MOBA_BLOCK, col < (qj + 1) * MOBA_BLOCK)
        shift = jnp.where(in_slot, blk * MOBA_BLOCK - qj * MOBA_BLOCK, shift)
        valid = jnp.where(in_slot, (blk < n_past_blocks).astype(jnp.int32), valid)
    dist = (past_len + row - (col + shift)).astype(F32)
    q_lo = row * (MOBA_TOPK * MOBA_BLOCK)
    mine = jnp.logical_and(col >= q_lo, col < q_lo + MOBA_TOPK * MOBA_BLOCK)
    s = jnp.where(jnp.logical_and(mine, valid > 0), s - slope * dist, NEG)
    own_rel = (lax.broadcasted_iota(jnp.int32, (t_new, t_new), 0)
               - lax.broadcasted_iota(jnp.int32, (t_new, t_new), 1))
    s_own = lax.dot_general(qs, kh_ref[rows, :].astype(BF16), _NT, preferred_element_type=F32)
    s_own = jnp.where(own_rel >= 0, s_own - slope * own_rel.astype(F32), NEG)

    m = jnp.maximum(jnp.max(s_own, axis=-1, keepdims=True), jnp.max(s, axis=-1, keepdims=True))
    p_own = jnp.exp(s_own - m)
    p = jnp.exp(s - m)
    l = jnp.sum(p_own, axis=-1, keepdims=True) + jnp.sum(p, axis=-1, keepdims=True)
    acc = (jnp.dot(p_own.astype(BF16), vh_ref[rows, :].astype(BF16), preferred_element_type=F32)
           + jnp.dot(p.astype(BF16), vbuf[buf].astype(BF16), preferred_element_type=F32))
    o_ref[...] = acc / l


def _attn_sample(page_table_flat, idx_flat, slopes, qh, kh, vh, cache_k, cache_v, n_seq, t_new, n_pages):
    n_slots = t_new * MOBA_TOPK
    head_blk = pl.BlockSpec((None, n_seq * t_new, HEAD_DIM), lambda b, h, pt, ix, sl: (h, 0, 0))
    grid_spec = pltpu.PrefetchScalarGridSpec(
        num_scalar_prefetch=3,
        grid=(n_seq, N_HEADS),
        in_specs=[head_blk, head_blk, head_blk,
                  pl.BlockSpec(memory_space=pl.ANY), pl.BlockSpec(memory_space=pl.ANY)],
        out_specs=pl.BlockSpec((None, t_new, HEAD_DIM), lambda b, h, pt, ix, sl: (b, 0, h)),
        scratch_shapes=[
            pltpu.VMEM((GATHER_DEPTH + 1, n_slots * MOBA_BLOCK, HEAD_DIM), F32),
            pltpu.VMEM((GATHER_DEPTH + 1, n_slots * MOBA_BLOCK, HEAD_DIM), F32),
            pltpu.SemaphoreType.DMA((2, GATHER_DEPTH + 1)),
        ],
    )
    return pl.pallas_call(
        functools.partial(_attn_sample_kernel, n_seq=n_seq, t_new=t_new, n_pages=n_pages),
        grid_spec=grid_spec,
        out_shape=jax.ShapeDtypeStruct((n_seq, t_new, ATTN_WIDTH), F32),
        compiler_params=_params("arbitrary", "arbitrary"),
        name="attn_sample",
    )(page_table_flat, idx_flat, slopes, qh, kh, vh, cache_k, cache_v)


def kernel(x_prompt, x_sample, cache_k, cache_v, state_pool, page_table, norm1_g, w_in, w_pool, pool_scale,
           w_out, norm2_g, w_gate, w_up, w_down, norm_f_g):
    depth = w_in.shape[0]
    assert depth == 1, "single-layer step"
    batch, seq, _ = x_prompt.shape
    n_seq, t_new, _ = x_sample.shape
    n_pages = page_table.shape[1]
    past_len = n_pages * PAGE_SIZE
    assert seq % MOBA_BLOCK == 0 and past_len % MOBA_BLOCK == 0 and t_new <= MOBA_BLOCK

    slopes = jnp.asarray(_alibi_slopes())
    g1 = norm1_g[0].reshape(1, D_MODEL)
    g2 = norm2_g[0].reshape(1, D_MODEL)
    gf = norm_f_g.reshape(1, D_MODEL)
    ps = pool_scale[0].reshape(1, POOL_WIDTH)
    w_pool_bf = w_pool[0].astype(BF16)

    xp = x_prompt.reshape(batch * seq, D_MODEL)
    pt_flat = page_table.reshape(-1)
    k, v, u, qh_p, kbh, vbh, km = _proj_prompt(xp, g1, w_in[0])
    km = km.reshape(batch, seq // MOBA_BLOCK, ATTN_WIDTH)
    a, (w_out_bf, w_gate_bf, w_up_bf, w_down_bf) = _attn_prompt(
        qh_p, kbh, vbh, km, _alibi_key_table(seq), batch, seq,
        (w_out[0], w_gate[0], w_up[0], w_down[0]))
    x1, h2, hist_p = _mid_prompt(a, u, xp, w_pool_bf, ps, w_out_bf, g2, batch, seq)
    y_p, km_s = _ffn(h2, w_gate_bf, w_up_bf, w_down_bf, x1, gf, tm=512, paged_k=(pt_flat, cache_k))

    xs = x_sample.reshape(n_seq * t_new, D_MODEL)
    k_s, v_s, u_s, qh, kh, vh = _proj_sample(xs, g1, w_in[0])
    km_s = km_s.reshape(n_seq, n_pages // PAGES_PER_BLOCK, ATTN_WIDTH)
    idx = _select_sample(qh, km_s, n_seq, t_new)
    a_s = _attn_sample(pt_flat, idx.reshape(-1), slopes, qh, kh, vh, cache_k, cache_v, n_seq, t_new, n_pages)
    x1_s, h2_s, hist_s = _mid_sample(a_s.reshape(n_seq * t_new, ATTN_WIDTH), u_s, state_pool[0], xs,
                                     w_pool_bf, ps, w_out_bf, g2, n_seq, t_new, past_len)
    y_s = _ffn(h2_s, w_gate_bf, w_up_bf, w_down_bf, x1_s, gf, tm=n_seq * t_new)

    kv_p = (depth, batch, seq, N_HEADS, HEAD_DIM)
    kv_s = (depth, n_seq, t_new, N_HEADS, HEAD_DIM)
    return (y_p.reshape(batch, seq, D_MODEL), y_s.reshape(n_seq, t_new, D_MODEL),
            k.reshape(kv_p), v.reshape(kv_p), hist_p[None],
            k_s.reshape(kv_s), v_s.reshape(kv_s), hist_s[None])
```

```python
import functools

import jax
import jax.numpy as jnp
import numpy as np
from jax import lax
from jax.experimental import pallas as pl
from jax.experimental.pallas import tpu as pltpu

D_MODEL = 2048
HEAD_DIM = 128
N_HEADS = 8
ATTN_WIDTH = N_HEADS * HEAD_DIM
POOL_WIDTH = D_MODEL - ATTN_WIDTH
IN_WIDTH = 3 * ATTN_WIDTH + POOL_WIDTH
MOBA_BLOCK = 256
MOBA_TOPK = 3
PAGE_SIZE = 128
PAGES_PER_BLOCK = MOBA_BLOCK // PAGE_SIZE
POOL_WINDOWS = (2, 4, 8, 16)
POOL_GROUP_W = POOL_WIDTH // len(POOL_WINDOWS)
POOL_HIST = max(POOL_WINDOWS) - 1
LANES = 128
SUBLANES = 8
BF16_SUBLANES = 16
HIST_PAD = 2 * SUBLANES
RMS_EPS = 1e-6
NEG = -1e30
ATTN_SCALE = HEAD_DIM ** -0.5

V7X_VMEM_BYTES = 64 * 1024 * 1024
VMEM_LIMIT_BYTES = V7X_VMEM_BYTES - 8 * 1024 * 1024

F32 = jnp.float32
BF16 = jnp.bfloat16
_NT = (((1,), (1,)), ((), ()))


def _rmsnorm(x, g):
    y = x * lax.rsqrt(jnp.mean(x * x, axis=-1, keepdims=True) + RMS_EPS)
    return y * g


def _params(*semantics):
    return pltpu.CompilerParams(dimension_semantics=semantics, vmem_limit_bytes=VMEM_LIMIT_BYTES)


W_CAST_CHUNKS = 8


def _proj_prompt_kernel(x_ref, g_ref, w_hbm, k_ref, v_ref, u_ref, qh_ref, kbh_ref, vbh_ref, km_ref,
                        w_ref, stage_ref, sem, *, tm):
    @pl.when(pl.program_id(0) == 0)
    def _():
        rows = D_MODEL // W_CAST_CHUNKS

        def chunk_copy(c):
            return pltpu.make_async_copy(w_hbm.at[pl.ds(c * rows, rows), :], stage_ref.at[c % 2], sem.at[c % 2])

        chunk_copy(0).start()
        for c in range(W_CAST_CHUNKS):
            if c + 1 < W_CAST_CHUNKS:
                chunk_copy(c + 1).start()
            chunk_copy(c).wait()
            w_ref[c * rows:(c + 1) * rows, :] = stage_ref[c % 2].astype(BF16)

    h = _rmsnorm(x_ref[...], g_ref[...]).astype(BF16)

    def seg(s):
        return jnp.dot(h, w_ref[:, s * ATTN_WIDTH:(s + 1) * ATTN_WIDTH], preferred_element_type=F32)

    zq = seg(0)
    for hd in range(N_HEADS):
        qh_ref[hd] = zq[:, hd * HEAD_DIM:(hd + 1) * HEAD_DIM]
    zk = seg(1)
    k_ref[...] = zk
    for hd in range(N_HEADS):
        kbh_ref[hd] = zk[:, hd * HEAD_DIM:(hd + 1) * HEAD_DIM].astype(BF16)
    for blk in range(tm // MOBA_BLOCK):
        rows = zk[blk * MOBA_BLOCK:(blk + 1) * MOBA_BLOCK]
        km_ref[blk] = jnp.sum(rows, axis=0, keepdims=True) * (1.0 / MOBA_BLOCK)
    zv = seg(2)
    v_ref[...] = zv
    for hd in range(N_HEADS):
        vbh_ref[hd] = zv[:, hd * HEAD_DIM:(hd + 1) * HEAD_DIM].astype(BF16)
    u_ref[...] = seg(3)


def _proj_prompt(x, g, w, tm=256):
    m = x.shape[0]
    row = lambda i: (i, 0)
    f32_out = jax.ShapeDtypeStruct((m, ATTN_WIDTH), F32)
    km_out = jax.ShapeDtypeStruct((m // MOBA_BLOCK, 1, ATTN_WIDTH), F32)
    blk = pl.BlockSpec((tm, ATTN_WIDTH), row)
    hm_blk = pl.BlockSpec((N_HEADS, tm, HEAD_DIM), lambda i: (0, i, 0))
    return pl.pallas_call(
        functools.partial(_proj_prompt_kernel, tm=tm),
        grid=(m // tm,),
        in_specs=[
            pl.BlockSpec((tm, D_MODEL), row),
            pl.BlockSpec((1, D_MODEL), lambda i: (0, 0)),
            pl.BlockSpec(memory_space=pl.ANY),
        ],
        out_specs=[blk, blk, blk, hm_blk, hm_blk, hm_blk,
                   pl.BlockSpec((tm // MOBA_BLOCK, 1, ATTN_WIDTH), lambda i: (i, 0, 0))],
        out_shape=[f32_out, f32_out, f32_out,
                   jax.ShapeDtypeStruct((N_HEADS, m, HEAD_DIM), F32),
                   jax.ShapeDtypeStruct((N_HEADS, m, HEAD_DIM), BF16),
                   jax.ShapeDtypeStruct((N_HEADS, m, HEAD_DIM), BF16), km_out],
        scratch_shapes=[
            pltpu.VMEM((D_MODEL, IN_WIDTH), BF16),
            pltpu.VMEM((2, D_MODEL // W_CAST_CHUNKS, IN_WIDTH), F32),
            pltpu.SemaphoreType.DMA((2,)),
        ],
        compiler_params=_params("arbitrary"),
        name="proj_prompt",
    )(x, g, w)


def _proj_sample_kernel(x_ref, g_ref, w_ref, k_ref, v_ref, u_ref, qh_ref, kh_ref, vh_ref):
    seg = pl.program_id(0)
    h = _rmsnorm(x_ref[...], g_ref[...]).astype(BF16)
    z = jnp.dot(h, w_ref[...].astype(BF16), preferred_element_type=F32)

    def head_major(o_ref):
        for hd in range(N_HEADS):
            o_ref[hd] = z[:, hd * HEAD_DIM:(hd + 1) * HEAD_DIM]

    @pl.when(seg == 0)
    def _():
        head_major(qh_ref)

    @pl.when(seg == 1)
    def _():
        k_ref[...] = z
        head_major(kh_ref)

    @pl.when(seg == 2)
    def _():
        v_ref[...] = z
        head_major(vh_ref)

    @pl.when(seg == 3)
    def _():
        u_ref[...] = z


def _proj_sample(x, g, w):
    m = x.shape[0]
    f32_out = jax.ShapeDtypeStruct((m, ATTN_WIDTH), F32)
    hm_out = jax.ShapeDtypeStruct((N_HEADS, m, HEAD_DIM), F32)
    whole2 = pl.BlockSpec((m, ATTN_WIDTH), lambda s: (0, 0))
    whole3 = pl.BlockSpec((N_HEADS, m, HEAD_DIM), lambda s: (0, 0, 0))
    return pl.pallas_call(
        _proj_sample_kernel,
        grid=(IN_WIDTH // ATTN_WIDTH,),
        in_specs=[
            pl.BlockSpec((m, D_MODEL), lambda s: (0, 0)),
            pl.BlockSpec((1, D_MODEL), lambda s: (0, 0)),
            pl.BlockSpec((D_MODEL, ATTN_WIDTH), lambda s: (0, s)),
        ],
        out_specs=[whole2, whole2, whole2, whole3, whole3, whole3],
        out_shape=[f32_out, f32_out, f32_out, hm_out, hm_out, hm_out],
        compiler_params=_params("arbitrary"),
        name="proj_sample",
    )(x, g, w)


def _alibi_slopes():
    return np.exp2(-8.0 * np.arange(1, N_HEADS + 1, dtype=np.float32) / N_HEADS)


def _alibi_key_table(seq):
    nb = seq // MOBA_BLOCK
    t = np.arange(seq)
    blk, local = t // MOBA_BLOCK, t % MOBA_BLOCK
    table = np.zeros((N_HEADS, seq, HEAD_DIM), np.float32)
    table[:, t, blk] = 1.0
    table[:, :, nb] = _alibi_slopes()[:, None] * local[None, :]
    table[:, :, nb + 1] = _alibi_slopes()[:, None] * (blk * MOBA_BLOCK)[None, :]
    return jnp.asarray(table.astype(BF16))


def _split_bf16(x):
    hi = x.astype(BF16)
    return hi, (x - hi.astype(F32)).astype(BF16)


def _topk_block_bias_t(scores_t, n_past):
    nb = scores_t.shape[0]
    n_iota = lax.broadcasted_iota(jnp.int32, scores_t.shape, 0)
    past = n_iota < n_past
    s = jnp.where(past, scores_t, NEG)
    rank = jnp.zeros(scores_t.shape, jnp.int32)
    for m in range(nb):
        sm = s[m:m + 1, :]
        tie_lower = jnp.logical_and(sm == s, n_iota > m)
        rank = rank + jnp.where(sm > s, 1, jnp.where(tie_lower, 1, 0))
    keep = jnp.logical_or(jnp.logical_and(past, rank < MOBA_TOPK), n_iota == n_past)
    return jnp.where(keep, 0.0, NEG)


def _attn_prompt_kernel(q_ref, kb_ref, vb_ref, km_ref, kx_ref, *refs, nb, n_cast):
    w_refs, o_ref, wbf_refs = refs[:n_cast], refs[n_cast], refs[n_cast + 1:2 * n_cast + 1]
    qa_ref, ka_ref, s_ref, p_ref, acc_ref = refs[2 * n_cast + 1:]
    for w_ref, wbf_ref in zip(w_refs, wbf_refs):
        wbf_ref[...] = w_ref[...].astype(BF16)
    seq = nb * MOBA_BLOCK
    km_hl = jnp.concatenate(_split_bf16(km_ref[...]), axis=0)
    shape = (MOBA_BLOCK, MOBA_BLOCK)
    causal = lax.broadcasted_iota(jnp.int32, shape, 0) >= lax.broadcasted_iota(jnp.int32, shape, 1)

    ka_ref[:, 0:HEAD_DIM] = kb_ref[...]
    ka_ref[:, HEAD_DIM:] = kx_ref[...]
    ones_rows = jnp.where(lax.broadcasted_iota(jnp.int32, (SUBLANES, MOBA_BLOCK), 0) < 2, 1.0, 0.0)
    zero_rows = jnp.zeros((HEAD_DIM - nb - SUBLANES, MOBA_BLOCK), F32)
    for qi in range(nb):
        rows = slice(qi * MOBA_BLOCK, (qi + 1) * MOBA_BLOCK)
        q = q_ref[rows, :]
        q_hi, q_lo = _split_bf16(q)
        sel_hl = lax.dot_general(km_hl, q_hi, _NT, preferred_element_type=F32)
        sel_t = (sel_hl[0:nb] + sel_hl[nb:2 * nb]
                 + lax.dot_general(km_hl[0:nb], q_lo, _NT, preferred_element_type=F32))
        bias_t = _topk_block_bias_t(sel_t, qi)
        qa_ref[rows, 0:HEAD_DIM] = (q * ATTN_SCALE).astype(BF16)
        qa_ref[rows, HEAD_DIM:] = jnp.concatenate([bias_t, ones_rows, zero_rows], axis=0).T.astype(BF16)

    for n in range(nb):
        r0 = n * MOBA_BLOCK
        s = lax.dot_general(qa_ref[r0:seq, :], ka_ref[r0:r0 + MOBA_BLOCK, :], _NT, preferred_element_type=F32)
        s_ref[n, r0:r0 + MOBA_BLOCK, :] = jnp.where(causal, s[0:MOBA_BLOCK], NEG)
        if r0 + MOBA_BLOCK < seq:
            s_ref[n, r0 + MOBA_BLOCK:seq, :] = s[MOBA_BLOCK:]

    inv_l = []
    for qi in range(nb):
        rows = slice(qi * MOBA_BLOCK, (qi + 1) * MOBA_BLOCK)
        m = jnp.max(s_ref[0, rows, :], axis=-1, keepdims=True)
        for n in range(1, qi + 1):
            m = jnp.maximum(m, jnp.max(s_ref[n, rows, :], axis=-1, keepdims=True))
        l = jnp.zeros((MOBA_BLOCK, 1), F32)
        for n in range(qi + 1):
            p = jnp.exp(s_ref[n, rows, :] - m)
            l = l + jnp.sum(p, axis=-1, keepdims=True)
            p_ref[n, rows, :] = p.astype(BF16)
        inv_l.append(1.0 / l)

    for n in range(nb):
        r0 = n * MOBA_BLOCK
        pv = jnp.dot(p_ref[n, r0:seq, :], vb_ref[r0:r0 + MOBA_BLOCK, :], preferred_element_type=F32)
        if n == 0:
            acc_ref[...] = pv
        else:
            acc_ref[r0:seq, :] += pv
    for qi in range(nb):
        rows = slice(qi * MOBA_BLOCK, (qi + 1) * MOBA_BLOCK)
        o_ref[rows, :] = (acc_ref[rows, :] * inv_l[qi]).astype(o_ref.dtype)


def _attn_prompt(qh, kbh, vbh, km, kx, batch, seq, cast_weights):
    nb = seq // MOBA_BLOCK
    assert nb == SUBLANES, "bias rows fill one sublane tile"
    steps = batch * N_HEADS
    slab = pl.BlockSpec((None, seq, HEAD_DIM), lambda b, h: (h, b, 0))
    chunk_specs = []
    for w in cast_weights:
        assert w.shape[0] % (BF16_SUBLANES * steps) == 0, "bf16 row chunks must be whole (16, 128) tiles"
        chunk_specs.append(pl.BlockSpec((w.shape[0] // steps, w.shape[1]), lambda b, h: (b * N_HEADS + h, 0)))
    res = pl.pallas_call(
        functools.partial(_attn_prompt_kernel, nb=nb, n_cast=len(cast_weights)),
        grid=(batch, N_HEADS),
        in_specs=[
            slab, slab, slab,
            pl.BlockSpec((None, nb, HEAD_DIM), lambda b, h: (b, 0, h)),
            pl.BlockSpec((None, seq, HEAD_DIM), lambda b, h: (h, 0, 0)),
        ] + chunk_specs,
        out_specs=[slab] + chunk_specs,
        out_shape=[jax.ShapeDtypeStruct((N_HEADS, batch * seq, HEAD_DIM), BF16)]
        + [jax.ShapeDtypeStruct(w.shape, BF16) for w in cast_weights],
        scratch_shapes=[
            pltpu.VMEM((seq, 2 * HEAD_DIM), BF16),
            pltpu.VMEM((seq, 2 * HEAD_DIM), BF16),
            pltpu.VMEM((nb, seq, MOBA_BLOCK), F32),
            pltpu.VMEM((nb, seq, MOBA_BLOCK), BF16),
            pltpu.VMEM((seq, HEAD_DIM), F32),
        ],
        compiler_params=_params("arbitrary", "arbitrary"),
        name="attn_prompt",
    )(qh, kbh, vbh, km, kx, *cast_weights)
    return res[0], res[1:]


def _pool_diff(ext_ref, r0, n, pos, d_ref, d0):
    for g, w in enumerate(POOL_WINDOWS):
        ls = slice(g * POOL_GROUP_W, (g + 1) * POOL_GROUP_W)
        cur = ext_ref[pl.ds(r0, n), ls]
        win = cur
        for j in range(1, w):
            win = win + ext_ref[pl.ds(r0 - j, n), ls]
        cnt = jnp.minimum(pos + 1, w).astype(F32)
        d_ref[pl.ds(d0, n), ls] = win / cnt - cur


def _mix_out(a_bf, d_ref, wp_ref, ps_ref, wo_ref):
    out = jnp.dot(a_bf, wo_ref[0:ATTN_WIDTH, :], preferred_element_type=F32)
    ys = []
    for g in range(len(POOL_WINDOWS)):
        ls = slice(g * POOL_GROUP_W, (g + 1) * POOL_GROUP_W)
        y = jnp.dot(d_ref[:, ls].astype(BF16), wp_ref[g], preferred_element_type=F32)
        ys.append((y * ps_ref[:, ls]).astype(BF16))
    return out + jnp.dot(jnp.concatenate(ys, axis=-1), wo_ref[ATTN_WIDTH:, :], preferred_element_type=F32)


POOL_PAD = HIST_PAD + SUBLANES


def _pool_diff_tile(ext_ref, t1_ref, t2_ref, tm, pos, d_ref):
    lo = SUBLANES
    n_all = POOL_PAD - lo + tm
    t1_ref[0:lo, :] = jnp.zeros((lo, POOL_GROUP_W), F32)
    t2_ref[0:lo, :] = jnp.zeros((lo, POOL_GROUP_W), F32)
    for g, w in enumerate(POOL_WINDOWS):
        ls = slice(g * POOL_GROUP_W, (g + 1) * POOL_GROUP_W)
        src, lanes, k, stage = ext_ref, ls, 1, 0
        while 2 * k < w:
            dst = (t1_ref, t2_ref)[stage % 2]
            dst[pl.ds(lo, n_all), :] = src[pl.ds(lo, n_all), lanes] + src[pl.ds(lo - k, n_all), lanes]
            src, lanes, k, stage = dst, slice(None), 2 * k, stage + 1
        win = src[pl.ds(POOL_PAD, tm), lanes] + src[pl.ds(POOL_PAD - k, tm), lanes]
        cnt = jnp.minimum(pos + 1, w).astype(F32)
        d_ref[:, ls] = win / cnt - ext_ref[pl.ds(POOL_PAD, tm), ls]


def _mid_prompt_kernel(a_ref, u_ref, uprev_ref, x_ref, wp_ref, ps_ref, wo_ref, g2_ref,
                       x1_ref, h2_ref, hist_ref, ext_ref, d_ref, t1_ref, t2_ref, *, tm, tiles_per_seq):
    t_in_seq = lax.rem(pl.program_id(0), tiles_per_seq)
    first = POOL_PAD - HIST_PAD
    ext_ref[0:first, :] = jnp.zeros((first, POOL_WIDTH), F32)
    ext_ref[first:POOL_PAD, :] = jnp.where(t_in_seq == 0, 0.0, uprev_ref[...])
    ext_ref[POOL_PAD:POOL_PAD + tm, :] = u_ref[...]
    pos = t_in_seq * tm + lax.broadcasted_iota(jnp.int32, (tm, 1), 0)
    _pool_diff_tile(ext_ref, t1_ref, t2_ref, tm, pos, d_ref)
    a = jnp.concatenate([a_ref[hd] for hd in range(N_HEADS)], axis=-1)
    x1 = x_ref[...] + _mix_out(a, d_ref, wp_ref, ps_ref, wo_ref)
    x1_ref[...] = x1
    h2_ref[...] = _rmsnorm(x1, g2_ref[...]).astype(BF16)

    @pl.when(t_in_seq == tiles_per_seq - 1)
    def _():
        hist_ref[...] = ext_ref[pl.ds(POOL_PAD + tm - POOL_HIST, POOL_HIST), :]


def _mid_prompt(a, u, x, wp_bf, ps, wo_bf, g2, batch, seq, tm=512):
    m = x.shape[0]
    tiles_per_seq = seq // tm
    row = lambda i: (i, 0)
    const2 = lambda i: (0, 0)
    return pl.pallas_call(
        functools.partial(_mid_prompt_kernel, tm=tm, tiles_per_seq=tiles_per_seq),
        grid=(m // tm,),
        in_specs=[
            pl.BlockSpec((N_HEADS, tm, HEAD_DIM), lambda i: (0, i, 0)),
            pl.BlockSpec((tm, POOL_WIDTH), row),
            pl.BlockSpec((HIST_PAD, POOL_WIDTH), lambda i: (jnp.maximum(i * (tm // HIST_PAD) - 1, 0), 0)),
            pl.BlockSpec((tm, D_MODEL), row),
            pl.BlockSpec(wp_bf.shape, lambda i: (0, 0, 0), pipeline_mode=pl.Buffered(1)),
            pl.BlockSpec((1, POOL_WIDTH), const2),
            pl.BlockSpec((D_MODEL, D_MODEL), const2, pipeline_mode=pl.Buffered(1)),
            pl.BlockSpec((1, D_MODEL), const2),
        ],
        out_specs=[
            pl.BlockSpec((tm, D_MODEL), row),
            pl.BlockSpec((tm, D_MODEL), row),
            pl.BlockSpec((None, POOL_HIST, POOL_WIDTH), lambda i: (i // tiles_per_seq, 0, 0)),
        ],
        out_shape=[
            jax.ShapeDtypeStruct((m, D_MODEL), F32),
            jax.ShapeDtypeStruct((m, D_MODEL), BF16),
            jax.ShapeDtypeStruct((batch, POOL_HIST, POOL_WIDTH), F32),
        ],
        scratch_shapes=[pltpu.VMEM((POOL_PAD + tm, POOL_WIDTH), F32), pltpu.VMEM((tm, POOL_WIDTH), F32),
                        pltpu.VMEM((POOL_PAD + tm, POOL_GROUP_W), F32), pltpu.VMEM((POOL_PAD + tm, POOL_GROUP_W), F32)],
        compiler_params=_params("arbitrary"),
        name="mid_prompt",
    )(a, u, u, x, wp_bf, ps, wo_bf, g2)


def _mid_sample_kernel(a_ref, u_ref, hist_ref, x_ref, wp_ref, ps_ref, wo_ref, g2_ref,
                       x1_ref, h2_ref, newhist_ref, ext_ref, d_ref, *, n_seq, t_new, start_pos):
    stride = POOL_HIST + t_new
    stride += -stride % SUBLANES
    pos = start_pos + lax.broadcasted_iota(jnp.int32, (t_new, 1), 0)
    for b in range(n_seq):
        base = b * stride
        ext_ref[base:base + POOL_HIST, :] = hist_ref[b]
        ext_ref[base + POOL_HIST:base + POOL_HIST + t_new, :] = u_ref[b * t_new:(b + 1) * t_new, :]
        _pool_diff(ext_ref, base + POOL_HIST, t_new, pos, d_ref, b * t_new)
        newhist_ref[b] = ext_ref[pl.ds(base + t_new, POOL_HIST), :]
    x1 = x_ref[...] + _mix_out(a_ref[...].astype(BF16), d_ref, wp_ref, ps_ref, wo_ref)
    x1_ref[...] = x1
    h2_ref[...] = _rmsnorm(x1, g2_ref[...]).astype(BF16)


def _mid_sample(a, u, hist, x, wp_bf, ps, wo_bf, g2, n_seq, t_new, start_pos):
    m = x.shape[0]
    stride = POOL_HIST + t_new
    stride += -stride % SUBLANES
    return pl.pallas_call(
        functools.partial(_mid_sample_kernel, n_seq=n_seq, t_new=t_new, start_pos=start_pos),
        out_shape=[
            jax.ShapeDtypeStruct((m, D_MODEL), F32),
            jax.ShapeDtypeStruct((m, D_MODEL), BF16),
            jax.ShapeDtypeStruct((n_seq, POOL_HIST, POOL_WIDTH), F32),
        ],
        scratch_shapes=[pltpu.VMEM((n_seq * stride, POOL_WIDTH), F32), pltpu.VMEM((m, POOL_WIDTH), F32)],
        compiler_params=pltpu.CompilerParams(vmem_limit_bytes=VMEM_LIMIT_BYTES),
        name="mid_sample",
    )(a, u, hist, x, wp_bf, ps, wo_bf, g2)


def _store_block_mean(o_ref, jb, tot):
    mean = tot * (1.0 / MOBA_BLOCK)
    for hd in range(N_HEADS):
        o_ref[jb:jb + 1, hd * HEAD_DIM:(hd + 1) * HEAD_DIM] = mean[hd:hd + 1, :]


def _ffn_kernel(*refs, n_pages):
    if n_pages:
        refs = refs[1:]
    h2_ref, wg_ref, wu_ref, wd_ref, x1_ref, gf_ref = refs[:6]
    page_refs = refs[6:6 + n_pages]
    y_ref = refs[6 + n_pages]
    km_ref = refs[7 + n_pages] if n_pages else None
    j = pl.program_id(1)

    @pl.when(j == 0)
    def _():
        y_ref[...] = x1_ref[...]

    n_slots = 6
    sums = {}

    def side_job(slot):
        for r in range(slot, n_pages, n_slots):
            sums[r] = jnp.sum(page_refs[r][...], axis=0)
            jb, pos = divmod(r, PAGES_PER_BLOCK)
            if pos == PAGES_PER_BLOCK - 1:
                tot = sums[jb * PAGES_PER_BLOCK]
                for rr in range(jb * PAGES_PER_BLOCK + 1, r + 1):
                    tot = tot + sums[rr]
                _store_block_mean(km_ref, jb, tot)

    h2 = h2_ref[...]
    tf = wg_ref.shape[1]
    gate, up = [], []
    for c, cols in enumerate((slice(0, tf // 2), slice(tf // 2, tf))):
        gate.append(jnp.dot(h2, wg_ref[:, cols], preferred_element_type=F32))
        side_job(c)
    for c, cols in enumerate((slice(0, tf // 2), slice(tf // 2, tf))):
        up.append(jnp.dot(h2, wu_ref[:, cols], preferred_element_type=F32))
        side_job(2 + c)
    gate = jnp.concatenate(gate, axis=-1)
    act = (gate * jax.nn.sigmoid(gate) * jnp.concatenate(up, axis=-1)).astype(BF16)
    for c, cols in enumerate((slice(0, D_MODEL // 2), slice(D_MODEL // 2, D_MODEL))):
        y_ref[:, cols] += jnp.dot(act, wd_ref[:, cols], preferred_element_type=F32)
        side_job(4 + c)

    @pl.when(j == pl.num_programs(1) - 1)
    def _():
        y_ref[...] = _rmsnorm(y_ref[...], gf_ref[...])


def _ffn(h2, wg_bf, wu_bf, wd_bf, x1, gf, tm, tf=512, paged_k=None):
    m = h2.shape[0]
    d_ff = wg_bf.shape[1]
    ni, nj = m // tm, d_ff // tf
    in_specs = [
        pl.BlockSpec((tm, D_MODEL), lambda i, j, *_: (i, 0)),
        pl.BlockSpec((D_MODEL, tf), lambda i, j, *_: (0, j)),
        pl.BlockSpec((D_MODEL, tf), lambda i, j, *_: (0, j)),
        pl.BlockSpec((tf, D_MODEL), lambda i, j, *_: (j, 0)),
        pl.BlockSpec((tm, D_MODEL), lambda i, j, *_: (i, 0)),
        pl.BlockSpec((1, D_MODEL), lambda i, j, *_: (0, 0)),
    ]
    y_spec = pl.BlockSpec((tm, D_MODEL), lambda i, j, *_: (i, 0))
    y_shape = jax.ShapeDtypeStruct((m, D_MODEL), F32)
    if paged_k is None:
        return pl.pallas_call(
            functools.partial(_ffn_kernel, n_pages=0),
            grid=(ni, nj), in_specs=in_specs, out_specs=y_spec, out_shape=y_shape,
            compiler_params=_params("arbitrary", "arbitrary"), name="ffn",
        )(h2, wg_bf, wu_bf, wd_bf, x1, gf)

    page_table_flat, cache_k = paged_k
    total = page_table_flat.shape[0]
    pps = PAGES_PER_BLOCK * pl.cdiv(total, PAGES_PER_BLOCK * ni * nj)
    n_events = pl.cdiv(total, pps)

    def page_spec(r):
        def index_map(i, j, pt):
            event = jnp.minimum(i * nj + j, n_events - 1)
            return (0, pt[jnp.minimum(event * pps + r, total - 1)], 0, 0, 0)
        return pl.BlockSpec((None, None, PAGE_SIZE, N_HEADS, HEAD_DIM), index_map)

    bps = pps // PAGES_PER_BLOCK
    km_spec = pl.BlockSpec((None, bps, ATTN_WIDTH), lambda i, j, pt: (jnp.minimum(i * nj + j, n_events - 1), 0, 0))
    y, km = pl.pallas_call(
        functools.partial(_ffn_kernel, n_pages=pps),
        grid_spec=pltpu.PrefetchScalarGridSpec(
            num_scalar_prefetch=1, grid=(ni, nj),
            in_specs=in_specs + [page_spec(r) for r in range(pps)],
            out_specs=[y_spec, km_spec]),
        out_shape=[y_shape, jax.ShapeDtypeStruct((n_events, bps, ATTN_WIDTH), F32)],
        compiler_params=_params("arbitrary", "arbitrary"), name="ffn_kmean",
    )(page_table_flat, h2, wg_bf, wu_bf, wd_bf, x1, gf, *([cache_k] * pps))
    return y, km.reshape(n_events * bps, ATTN_WIDTH)[:total // PAGES_PER_BLOCK]


def _select_sample_kernel(qh_ref, km_ref, idx_ref, sc_ref, *, n_seq, t_new, n_past_blocks):
    lanes = sc_ref.shape[1]
    lane = lax.broadcasted_iota(jnp.int32, sc_ref.shape, 1)
    sc_ref[...] = jnp.where(lane == n_past_blocks, NEG, -jnp.inf)
    for b in range(n_seq):
        for hd in range(N_HEADS):
            q_hl = jnp.concatenate(_split_bf16(qh_ref[hd, b * t_new:(b + 1) * t_new, :]), axis=0)
            km_hi, km_lo = _split_bf16(km_ref[b, :, hd * HEAD_DIM:(hd + 1) * HEAD_DIM])
            by_hi = lax.dot_general(q_hl, km_hi, _NT, preferred_element_type=F32)
            r0 = (b * N_HEADS + hd) * t_new
            sc_ref[r0:r0 + t_new, 0:n_past_blocks] = (
                by_hi[0:t_new] + by_hi[t_new:2 * t_new]
                + lax.dot_general(q_hl[0:t_new], km_lo, _NT, preferred_element_type=F32))
    s = sc_ref[...]
    for r in range(MOBA_TOPK):
        mx = jnp.max(s, axis=-1, keepdims=True)
        am = jnp.min(jnp.where(s == mx, lane, lanes), axis=-1, keepdims=True)
        idx_ref[:, r:r + 1] = am
        s = jnp.where(lane == am, -jnp.inf, s)


def _select_sample(qh, km, n_seq, t_new):
    n_past_blocks = km.shape[1]
    rows = n_seq * N_HEADS * t_new
    return pl.pallas_call(
        functools.partial(_select_sample_kernel, n_seq=n_seq, t_new=t_new, n_past_blocks=n_past_blocks),
        out_shape=jax.ShapeDtypeStruct((rows, MOBA_TOPK), jnp.int32),
        scratch_shapes=[pltpu.VMEM((rows, LANES), F32)],
        compiler_params=pltpu.CompilerParams(vmem_limit_bytes=VMEM_LIMIT_BYTES),
        name="select_sample",
    )(qh, km)


GATHER_DEPTH = 2


def _attn_sample_kernel(pt_ref, idx_ref, slopes_ref, qh_ref, kh_ref, vh_ref, ck_hbm, cv_hbm, o_ref,
                        kbuf, vbuf, sem, *, n_seq, t_new, n_pages):
    n_past_blocks = n_pages // PAGES_PER_BLOCK
    past_len = n_pages * PAGE_SIZE
    n_slots = t_new * MOBA_TOPK
    n_keys = n_slots * MOBA_BLOCK
    n_steps = n_seq * N_HEADS
    n_bufs = GATHER_DEPTH + 1
    b = pl.program_id(0)
    hd = pl.program_id(1)
    step = b * N_HEADS + hd

    def block_of(st, qj):
        return idx_ref[st * n_slots + qj]

    def gather(st, start):
        bb, hh, sl = st // N_HEADS, lax.rem(st, N_HEADS), lax.rem(st, n_bufs)
        for qj in range(n_slots):
            blk = jnp.minimum(block_of(st, qj), n_past_blocks - 1)
            for half in range(PAGES_PER_BLOCK):
                phys = pt_ref[bb * n_pages + blk * PAGES_PER_BLOCK + half]
                rows = pl.ds(qj * MOBA_BLOCK + half * PAGE_SIZE, PAGE_SIZE)
                for src, dst, s in ((ck_hbm, kbuf, 0), (cv_hbm, vbuf, 1)):
                    cp = pltpu.make_async_copy(src.at[0, phys, :, hh, :], dst.at[sl, rows, :], sem.at[s, sl])
                    if start:
                        cp.start()
                    else:
                        cp.wait()

    @pl.when(step == 0)
    def _():
        for d in range(min(GATHER_DEPTH, n_steps)):
            gather(step + d, True)

    @pl.when(step + GATHER_DEPTH < n_steps)
    def _():
        gather(step + GATHER_DEPTH, True)

    gather(step, False)

    buf = lax.rem(step, n_bufs)
    slope = slopes_ref[hd]
    rows = pl.ds(b * t_new, t_new)
    qs = (qh_ref[rows, :] * ATTN_SCALE).astype(BF16)

    s = lax.dot_general(qs, kbuf[buf].astype(BF16), _NT, preferred_element_type=F32)
    row = lax.broadcasted_iota(jnp.int32, (t_new, n_keys), 0)
    col = lax.broadcasted_iota(jnp.int32, (t_new, n_keys), 1)
    shift = jnp.zeros((t_new, n_keys), jnp.int32)
    valid = jnp.zeros((t_new, n_keys), jnp.int32)
    for qj in range(n_slots):
        blk = block_of(step, qj)
        in_slot = jnp.logical_and(col >= qj * MOBA_BLOCK, col < (qj + 1) * MOBA_BLOCK)
        shift = jnp.where(in_slot, blk * MOBA_BLOCK - qj * MOBA_BLOCK, shift)
        valid = jnp.where(in_slot, (blk < n_past_blocks).astype(jnp.int32), valid)
    dist = (past_len + row - (col + shift)).astype(F32)
    q_lo = row * (MOBA_TOPK * MOBA_BLOCK)
    mine = jnp.logical_and(col >= q_lo, col < q_lo + MOBA_TOPK * MOBA_BLOCK)
    s = jnp.where(jnp.logical_and(mine, valid > 0), s - slope * dist, NEG)
    own_rel = (lax.broadcasted_iota(jnp.int32, (t_new, t_new), 0)
               - lax.broadcasted_iota(jnp.int32, (t_new, t_new), 1))
    s_own = lax.dot_general(qs, kh_ref[rows, :].astype(BF16), _NT, preferred_element_type=F32)
    s_own = jnp.where(own_rel >= 0, s_own - slope * own_rel.astype(F32), NEG)

    m = jnp.maximum(jnp.max(s_own, axis=-1, keepdims=True), jnp.max(s, axis=-1, keepdims=True))
    p_own = jnp.exp(s_own - m)
    p = jnp.exp(s - m)
    l = jnp.sum(p_own, axis=-1, keepdims=True) + jnp.sum(p, axis=-1, keepdims=True)
    acc = (jnp.dot(p_own.astype(BF16), vh_ref[rows, :].astype(BF16), preferred_element_type=F32)
           + jnp.dot(p.astype(BF16), vbuf[buf].astype(BF16), preferred_element_type=F32))
    o_ref[...] = acc / l


def _attn_sample(page_table_flat, idx_flat, slopes, qh, kh, vh, cache_k, cache_v, n_seq, t_new, n_pages):
    n_slots = t_new * MOBA_TOPK
    head_blk = pl.BlockSpec((None, n_seq * t_new, HEAD_DIM), lambda b, h, pt, ix, sl: (h, 0, 0))
    grid_spec = pltpu.PrefetchScalarGridSpec(
        num_scalar_prefetch=3,
        grid=(n_seq, N_HEADS),
        in_specs=[head_blk, head_blk, head_blk,
                  pl.BlockSpec(memory_space=pl.ANY), pl.BlockSpec(memory_space=pl.ANY)],
        out_specs=pl.BlockSpec((None, t_new, HEAD_DIM), lambda b, h, pt, ix, sl: (b, 0, h)),
        scratch_shapes=[
            pltpu.VMEM((GATHER_DEPTH + 1, n_slots * MOBA_BLOCK, HEAD_DIM), F32),
            pltpu.VMEM((GATHER_DEPTH + 1, n_slots * MOBA_BLOCK, HEAD_DIM), F32),
            pltpu.SemaphoreType.DMA((2, GATHER_DEPTH + 1)),
        ],
    )
    return pl.pallas_call(
        functools.partial(_attn_sample_kernel, n_seq=n_seq, t_new=t_new, n_pages=n_pages),
        grid_spec=grid_spec,
        out_shape=jax.ShapeDtypeStruct((n_seq, t_new, ATTN_WIDTH), F32),
        compiler_params=_params("arbitrary", "arbitrary"),
        name="attn_sample",
    )(page_table_flat, idx_flat, slopes, qh, kh, vh, cache_k, cache_v)


def kernel(x_prompt, x_sample, cache_k, cache_v, state_pool, page_table, norm1_g, w_in, w_pool, pool_scale,
           w_out, norm2_g, w_gate, w_up, w_down, norm_f_g):
    depth = w_in.shape[0]
    assert depth == 1, "single-layer step"
    batch, seq, _ = x_prompt.shape
    n_seq, t_new, _ = x_sample.shape
    n_pages = page_table.shape[1]
    past_len = n_pages * PAGE_SIZE
    assert seq % MOBA_BLOCK == 0 and past_len % MOBA_BLOCK == 0 and t_new <= MOBA_BLOCK

    slopes = jnp.asarray(_alibi_slopes())
    g1 = norm1_g[0].reshape(1, D_MODEL)
    g2 = norm2_g[0].reshape(1, D_MODEL)
    gf = norm_f_g.reshape(1, D_MODEL)
    ps = pool_scale[0].reshape(1, POOL_WIDTH)
    w_pool_bf = w_pool[0].astype(BF16)

    xp = x_prompt.reshape(batch * seq, D_MODEL)
    pt_flat = page_table.reshape(-1)
    k, v, u, qh_p, kbh, vbh, km = _proj_prompt(xp, g1, w_in[0])
    km = km.reshape(batch, seq // MOBA_BLOCK, ATTN_WIDTH)
    a, (w_out_bf, w_gate_bf, w_up_bf, w_down_bf) = _attn_prompt(
        qh_p, kbh, vbh, km, _alibi_key_table(seq), batch, seq,
        (w_out[0], w_gate[0], w_up[0], w_down[0]))
    x1, h2, hist_p = _mid_prompt(a, u, xp, w_pool_bf, ps, w_out_bf, g2, batch, seq)
    y_p, km_s = _ffn(h2, w_gate_bf, w_up_bf, w_down_bf, x1, gf, tm=512, paged_k=(pt_flat, cache_k))

    xs = x_sample.reshape(n_seq * t_new, D_MODEL)
    k_s, v_s, u_s, qh, kh, vh = _proj_sample(xs, g1, w_in[0])
    km_s = km_s.reshape(n_seq, n_pages // PAGES_PER_BLOCK, ATTN_WIDTH)
    idx = _select_sample(qh, km_s, n_seq, t_new)
    a_s = _attn_sample(pt_flat, idx.reshape(-1), slopes, qh, kh, vh, cache_k, cache_v, n_seq, t_new, n_pages)
    x1_s, h2_s, hist_s = _mid_sample(a_s.reshape(n_seq * t_new, ATTN_WIDTH), u_s, state_pool[0], xs,
                                     w_pool_bf, ps, w_out_bf, g2, n_seq, t_new, past_len)
    y_s = _ffn(h2_s, w_gate_bf, w_up_bf, w_down_bf, x1_s, gf, tm=n_seq * t_new)

    kv_p = (depth, batch, seq, N_HEADS, HEAD_DIM)
    kv_s = (depth, n_seq, t_new, N_HEADS, HEAD_DIM)
    return (y_p.reshape(batch, seq, D_MODEL), y_s.reshape(n_seq, t_new, D_MODEL),
            k.reshape(kv_p), v.reshape(kv_p), hist_p[None],
            k_s.reshape(kv_s), v_s.reshape(kv_s), hist_s[None])
```

```python
import functools

import jax
import jax.numpy as jnp
import numpy as np
from jax import lax
from jax.experimental import pallas as pl
from jax.experimental.pallas import tpu as pltpu

D_MODEL = 2048
HEAD_DIM = 128
N_HEADS = 8
ATTN_WIDTH = N_HEADS * HEAD_DIM
POOL_WIDTH = D_MODEL - ATTN_WIDTH
IN_WIDTH = 3 * ATTN_WIDTH + POOL_WIDTH
MOBA_BLOCK = 256
MOBA_TOPK = 3
PAGE_SIZE = 128
PAGES_PER_BLOCK = MOBA_BLOCK // PAGE_SIZE
POOL_WINDOWS = (2, 4, 8, 16)
POOL_GROUP_W = POOL_WIDTH // len(POOL_WINDOWS)
POOL_HIST = max(POOL_WINDOWS) - 1
LANES = 128
SUBLANES = 8
BF16_SUBLANES = 16
HIST_PAD = 2 * SUBLANES
RMS_EPS = 1e-6
NEG = -1e30
ATTN_SCALE = HEAD_DIM ** -0.5

V7X_VMEM_BYTES = 64 * 1024 * 1024
VMEM_LIMIT_BYTES = V7X_VMEM_BYTES - 8 * 1024 * 1024

F32 = jnp.float32
BF16 = jnp.bfloat16
_NT = (((1,), (1,)), ((), ()))


def _rmsnorm(x, g):
    y = x * lax.rsqrt(jnp.mean(x * x, axis=-1, keepdims=True) + RMS_EPS)
    return y * g


def _params(*semantics):
    return pltpu.CompilerParams(dimension_semantics=semantics, vmem_limit_bytes=VMEM_LIMIT_BYTES)


W_CAST_CHUNKS = 8


def _proj_prompt_kernel(x_ref, g_ref, w_hbm, k_hbm, v_hbm, u_ref, qh_ref, kbh_ref, vbh_ref, km_ref,
                        w_ref, stage_ref, sem, kv_ref, kv_sem, *, tm):
    @pl.when(pl.program_id(0) == 0)
    def _():
        rows = D_MODEL // W_CAST_CHUNKS

        def chunk_copy(c):
            return pltpu.make_async_copy(w_hbm.at[pl.ds(c * rows, rows), :], stage_ref.at[c % 2], sem.at[c % 2])

        chunk_copy(0).start()
        for c in range(W_CAST_CHUNKS):
            if c + 1 < W_CAST_CHUNKS:
                chunk_copy(c + 1).start()
            chunk_copy(c).wait()
            w_ref[c * rows:(c + 1) * rows, :] = stage_ref[c % 2].astype(BF16)

    h = _rmsnorm(x_ref[...], g_ref[...]).astype(BF16)

    def seg(s):
        return jnp.dot(h, w_ref[:, s * ATTN_WIDTH:(s + 1) * ATTN_WIDTH], preferred_element_type=F32)

    step = pl.program_id(0)
    slot = lax.rem(step, 2)

    def kv_copies(st, sl):
        return [pltpu.make_async_copy(kv_ref.at[sl, t, :, pl.ds(hd * HEAD_DIM, HEAD_DIM)],
                                      dst.at[pl.ds(st * tm, tm), hd, :], kv_sem.at[sl, t])
                for t, dst in enumerate((k_hbm, v_hbm)) for hd in range(N_HEADS)]

    @pl.when(step >= 2)
    def _():
        for cp in kv_copies(step - 2, slot):
            cp.wait()

    zq = seg(0)
    for hd in range(N_HEADS):
        qh_ref[hd] = zq[:, hd * HEAD_DIM:(hd + 1) * HEAD_DIM]
    zk = seg(1)
    kv_ref[slot, 0] = zk
    for hd in range(N_HEADS):
        kbh_ref[hd] = zk[:, hd * HEAD_DIM:(hd + 1) * HEAD_DIM].astype(BF16)
    for blk in range(tm // MOBA_BLOCK):
        rows = zk[blk * MOBA_BLOCK:(blk + 1) * MOBA_BLOCK]
        km_ref[blk] = jnp.sum(rows, axis=0, keepdims=True) * (1.0 / MOBA_BLOCK)
    zv = seg(2)
    kv_ref[slot, 1] = zv
    for hd in range(N_HEADS):
        vbh_ref[hd] = zv[:, hd * HEAD_DIM:(hd + 1) * HEAD_DIM].astype(BF16)
    u_ref[...] = seg(3)
    for cp in kv_copies(step, slot):
        cp.start()

    n_steps = pl.num_programs(0)

    @pl.when(step == n_steps - 1)
    def _():
        @pl.when(n_steps >= 2)
        def _():
            for cp in kv_copies(step - 1, 1 - slot):
                cp.wait()
        for cp in kv_copies(step, slot):
            cp.wait()


def _proj_prompt(x, g, w, tm=256):
    m = x.shape[0]
    row = lambda i: (i, 0)
    f32_out = jax.ShapeDtypeStruct((m, ATTN_WIDTH), F32)
    km_out = jax.ShapeDtypeStruct((m // MOBA_BLOCK, 1, ATTN_WIDTH), F32)
    blk = pl.BlockSpec((tm, ATTN_WIDTH), row)
    hm_blk = pl.BlockSpec((N_HEADS, tm, HEAD_DIM), lambda i: (0, i, 0))
    any_spec = pl.BlockSpec(memory_space=pl.ANY)
    kv_out = jax.ShapeDtypeStruct((m, N_HEADS, HEAD_DIM), F32)
    return pl.pallas_call(
        functools.partial(_proj_prompt_kernel, tm=tm),
        grid=(m // tm,),
        in_specs=[
            pl.BlockSpec((tm, D_MODEL), row),
            pl.BlockSpec((1, D_MODEL), lambda i: (0, 0)),
            pl.BlockSpec(memory_space=pl.ANY),
        ],
        out_specs=[any_spec, any_spec, blk, hm_blk, hm_blk, hm_blk,
                   pl.BlockSpec((tm // MOBA_BLOCK, 1, ATTN_WIDTH), lambda i: (i, 0, 0))],
        out_shape=[kv_out, kv_out, f32_out,
                   jax.ShapeDtypeStruct((N_HEADS, m, HEAD_DIM), F32),
                   jax.ShapeDtypeStruct((N_HEADS, m, HEAD_DIM), BF16),
                   jax.ShapeDtypeStruct((N_HEADS, m, HEAD_DIM), BF16), km_out],
        scratch_shapes=[
            pltpu.VMEM((D_MODEL, IN_WIDTH), BF16),
            pltpu.VMEM((2, D_MODEL // W_CAST_CHUNKS, IN_WIDTH), F32),
            pltpu.SemaphoreType.DMA((2,)),
            pltpu.VMEM((2, 2, tm, ATTN_WIDTH), F32),
            pltpu.SemaphoreType.DMA((2, 2)),
        ],
        compiler_params=_params("arbitrary"),
        name="proj_prompt",
    )(x, g, w)


def _proj_sample_kernel(x_ref, g_ref, w_ref, k_ref, v_ref, u_ref, qh_ref, kh_ref, vh_ref):
    seg = pl.program_id(0)
    h = _rmsnorm(x_ref[...], g_ref[...]).astype(BF16)
    z = jnp.dot(h, w_ref[...].astype(BF16), preferred_element_type=F32)

    def head_major(o_ref):
        for hd in range(N_HEADS):
            o_ref[hd] = z[:, hd * HEAD_DIM:(hd + 1) * HEAD_DIM]

    @pl.when(seg == 0)
    def _():
        head_major(qh_ref)

    @pl.when(seg == 1)
    def _():
        k_ref[...] = z
        head_major(kh_ref)

    @pl.when(seg == 2)
    def _():
        v_ref[...] = z
        head_major(vh_ref)

    @pl.when(seg == 3)
    def _():
        u_ref[...] = z


def _proj_sample(x, g, w):
    m = x.shape[0]
    f32_out = jax.ShapeDtypeStruct((m, ATTN_WIDTH), F32)
    hm_out = jax.ShapeDtypeStruct((N_HEADS, m, HEAD_DIM), F32)
    whole2 = pl.BlockSpec((m, ATTN_WIDTH), lambda s: (0, 0))
    whole3 = pl.BlockSpec((N_HEADS, m, HEAD_DIM), lambda s: (0, 0, 0))
    return pl.pallas_call(
        _proj_sample_kernel,
        grid=(IN_WIDTH // ATTN_WIDTH,),
        in_specs=[
            pl.BlockSpec((m, D_MODEL), lambda s: (0, 0)),
            pl.BlockSpec((1, D_MODEL), lambda s: (0, 0)),
            pl.BlockSpec((D_MODEL, ATTN_WIDTH), lambda s: (0, s)),
        ],
        out_specs=[whole2, whole2, whole2, whole3, whole3, whole3],
        out_shape=[f32_out, f32_out, f32_out, hm_out, hm_out, hm_out],
        compiler_params=_params("arbitrary"),
        name="proj_sample",
    )(x, g, w)


def _alibi_slopes():
    return np.exp2(-8.0 * np.arange(1, N_HEADS + 1, dtype=np.float32) / N_HEADS)


def _alibi_key_table(seq):
    nb = seq // MOBA_BLOCK
    t = np.arange(seq)
    blk, local = t // MOBA_BLOCK, t % MOBA_BLOCK
    table = np.zeros((N_HEADS, seq, HEAD_DIM), np.float32)
    table[:, t, blk] = 1.0
    table[:, :, nb] = _alibi_slopes()[:, None] * local[None, :]
    table[:, :, nb + 1] = _alibi_slopes()[:, None] * (blk * MOBA_BLOCK)[None, :]
    return jnp.asarray(table.astype(BF16))


def _split_bf16(x):
    hi = x.astype(BF16)
    return hi, (x - hi.astype(F32)).astype(BF16)


def _topk_block_bias_t(scores_t, n_past):
    nb = scores_t.shape[0]
    n_iota = lax.broadcasted_iota(jnp.int32, scores_t.shape, 0)
    past = n_iota < n_past
    s = jnp.where(past, scores_t, NEG)
    rank = jnp.zeros(scores_t.shape, jnp.int32)
    for m in range(nb):
        sm = s[m:m + 1, :]
        tie_lower = jnp.logical_and(sm == s, n_iota > m)
        rank = rank + jnp.where(sm > s, 1, jnp.where(tie_lower, 1, 0))
    keep = jnp.logical_or(jnp.logical_and(past, rank < MOBA_TOPK), n_iota == n_past)
    return jnp.where(keep, 0.0, NEG)


def _attn_prompt_kernel(q_ref, kb_ref, vb_ref, km_ref, kx_ref, *refs, nb, n_cast):
    w_refs, o_ref, wbf_refs = refs[:n_cast], refs[n_cast], refs[n_cast + 1:2 * n_cast + 1]
    qa_ref, ka_ref, s_ref, p_ref, acc_ref = refs[2 * n_cast + 1:]
    for w_ref, wbf_ref in zip(w_refs, wbf_refs):
        wbf_ref[...] = w_ref[...].astype(BF16)
    seq = nb * MOBA_BLOCK
    km_hl = jnp.concatenate(_split_bf16(km_ref[...]), axis=0)
    shape = (MOBA_BLOCK, MOBA_BLOCK)
    causal = lax.broadcasted_iota(jnp.int32, shape, 0) >= lax.broadcasted_iota(jnp.int32, shape, 1)

    ka_ref[:, 0:HEAD_DIM] = kb_ref[...]
    ka_ref[:, HEAD_DIM:] = kx_ref[...]
    ones_rows = jnp.where(lax.broadcasted_iota(jnp.int32, (SUBLANES, MOBA_BLOCK), 0) < 2, 1.0, 0.0)
    zero_rows = jnp.zeros((HEAD_DIM - nb - SUBLANES, MOBA_BLOCK), F32)
    for qi in range(nb):
        rows = slice(qi * MOBA_BLOCK, (qi + 1) * MOBA_BLOCK)
        q = q_ref[rows, :]
        q_hi, q_lo = _split_bf16(q)
        sel_hl = lax.dot_general(km_hl, q_hi, _NT, preferred_element_type=F32)
        sel_t = (sel_hl[0:nb] + sel_hl[nb:2 * nb]
                 + lax.dot_general(km_hl[0:nb], q_lo, _NT, preferred_element_type=F32))
        bias_t = _topk_block_bias_t(sel_t, qi)
        qa_ref[rows, 0:HEAD_DIM] = (q * ATTN_SCALE).astype(BF16)
        qa_ref[rows, HEAD_DIM:] = jnp.concatenate([bias_t, ones_rows, zero_rows], axis=0).T.astype(BF16)

    for n in range(nb):
        r0 = n * MOBA_BLOCK
        s = lax.dot_general(qa_ref[r0:seq, :], ka_ref[r0:r0 + MOBA_BLOCK, :], _NT, preferred_element_type=F32)
        s_ref[n, r0:r0 + MOBA_BLOCK, :] = jnp.where(causal, s[0:MOBA_BLOCK], NEG)
        if r0 + MOBA_BLOCK < seq:
            s_ref[n, r0 + MOBA_BLOCK:seq, :] = s[MOBA_BLOCK:]

    inv_l = []
    for qi in range(nb):
        rows = slice(qi * MOBA_BLOCK, (qi + 1) * MOBA_BLOCK)
        m = jnp.max(s_ref[0, rows, :], axis=-1, keepdims=True)
        for n in range(1, qi + 1):
            m = jnp.maximum(m, jnp.max(s_ref[n, rows, :], axis=-1, keepdims=True))
        l = jnp.zeros((MOBA_BLOCK, 1), F32)
        for n in range(qi + 1):
            p = jnp.exp(s_ref[n, rows, :] - m)
            l = l + jnp.sum(p, axis=-1, keepdims=True)
            p_ref[n, rows, :] = p.astype(BF16)
        inv_l.append(1.0 / l)

    for n in range(nb):
        r0 = n * MOBA_BLOCK
        pv = jnp.dot(p_ref[n, r0:seq, :], vb_ref[r0:r0 + MOBA_BLOCK, :], preferred_element_type=F32)
        if n == 0:
            acc_ref[...] = pv
        else:
            acc_ref[r0:seq, :] += pv
    for qi in range(nb):
        rows = slice(qi * MOBA_BLOCK, (qi + 1) * MOBA_BLOCK)
        o_ref[rows, :] = (acc_ref[rows, :] * inv_l[qi]).astype(o_ref.dtype)


def _attn_prompt(qh, kbh, vbh, km, kx, batch, seq, cast_weights):
    nb = seq // MOBA_BLOCK
    assert nb == SUBLANES, "bias rows fill one sublane tile"
    steps = batch * N_HEADS
    slab = pl.BlockSpec((None, seq, HEAD_DIM), lambda b, h: (h, b, 0))
    chunk_specs = []
    for w in cast_weights:
        assert w.shape[0] % (BF16_SUBLANES * steps) == 0, "bf16 row chunks must be whole (16, 128) tiles"
        chunk_specs.append(pl.BlockSpec((w.shape[0] // steps, w.shape[1]), lambda b, h: (b * N_HEADS + h, 0)))
    res = pl.pallas_call(
        functools.partial(_attn_prompt_kernel, nb=nb, n_cast=len(cast_weights)),
        grid=(batch, N_HEADS),
        in_specs=[
            slab, slab, slab,
            pl.BlockSpec((None, nb, HEAD_DIM), lambda b, h: (b, 0, h)),
            pl.BlockSpec((None, seq, HEAD_DIM), lambda b, h: (h, 0, 0)),
        ] + chunk_specs,
        out_specs=[slab] + chunk_specs,
        out_shape=[jax.ShapeDtypeStruct((N_HEADS, batch * seq, HEAD_DIM), BF16)]
        + [jax.ShapeDtypeStruct(w.shape, BF16) for w in cast_weights],
        scratch_shapes=[
            pltpu.VMEM((seq, 2 * HEAD_DIM), BF16),
            pltpu.VMEM((seq, 2 * HEAD_DIM), BF16),
            pltpu.VMEM((nb, seq, MOBA_BLOCK), F32),
            pltpu.VMEM((nb, seq, MOBA_BLOCK), BF16),
            pltpu.VMEM((seq, HEAD_DIM), F32),
        ],
        compiler_params=_params("arbitrary", "arbitrary"),
        name="attn_prompt",
    )(qh, kbh, vbh, km, kx, *cast_weights)
    return res[0], res[1:]


def _pool_diff(ext_ref, r0, n, pos, d_ref, d0):
    for g, w in enumerate(POOL_WINDOWS):
        ls = slice(g * POOL_GROUP_W, (g + 1) * POOL_GROUP_W)
        cur = ext_ref[pl.ds(r0, n), ls]
        win = cur
        for j in range(1, w):
            win = win + ext_ref[pl.ds(r0 - j, n), ls]
        cnt = jnp.minimum(pos + 1, w).astype(F32)
        d_ref[pl.ds(d0, n), ls] = win / cnt - cur


def _mix_out(a_bf, d_ref, wp_ref, ps_ref, wo_ref):
    out = jnp.dot(a_bf, wo_ref[0:ATTN_WIDTH, :], preferred_element_type=F32)
    ys = []
    for g in range(len(POOL_WINDOWS)):
        ls = slice(g * POOL_GROUP_W, (g + 1) * POOL_GROUP_W)
        y = jnp.dot(d_ref[:, ls].astype(BF16), wp_ref[g], preferred_element_type=F32)
        ys.append((y * ps_ref[:, ls]).astype(BF16))
    return out + jnp.dot(jnp.concatenate(ys, axis=-1), wo_ref[ATTN_WIDTH:, :], preferred_element_type=F32)


POOL_PAD = HIST_PAD + SUBLANES


def _pool_diff_tile(ext_ref, t1_ref, t2_ref, tm, pos, d_ref):
    lo = SUBLANES
    n_all = POOL_PAD - lo + tm
    t1_ref[0:lo, :] = jnp.zeros((lo, POOL_GROUP_W), F32)
    t2_ref[0:lo, :] = jnp.zeros((lo, POOL_GROUP_W), F32)
    for g, w in enumerate(POOL_WINDOWS):
        ls = slice(g * POOL_GROUP_W, (g + 1) * POOL_GROUP_W)
        src, lanes, k, stage = ext_ref, ls, 1, 0
        while 2 * k < w:
            dst = (t1_ref, t2_ref)[stage % 2]
            dst[pl.ds(lo, n_all), :] = src[pl.ds(lo, n_all), lanes] + src[pl.ds(lo - k, n_all), lanes]
            src, lanes, k, stage = dst, slice(None), 2 * k, stage + 1
        win = src[pl.ds(POOL_PAD, tm), lanes] + src[pl.ds(POOL_PAD - k, tm), lanes]
        cnt = jnp.minimum(pos + 1, w).astype(F32)
        d_ref[:, ls] = win / cnt - ext_ref[pl.ds(POOL_PAD, tm), ls]


def _mid_prompt_kernel(a_ref, u_ref, uprev_ref, x_ref, wp_ref, ps_ref, wo_ref, g2_ref,
                       x1_ref, h2_ref, hist_ref, ext_ref, d_ref, t1_ref, t2_ref, *, tm, tiles_per_seq):
    t_in_seq = lax.rem(pl.program_id(0), tiles_per_seq)
    first = POOL_PAD - HIST_PAD
    ext_ref[0:first, :] = jnp.zeros((first, POOL_WIDTH), F32)
    ext_ref[first:POOL_PAD, :] = jnp.where(t_in_seq == 0, 0.0, uprev_ref[...])
    ext_ref[POOL_PAD:POOL_PAD + tm, :] = u_ref[...]
    pos = t_in_seq * tm + lax.broadcasted_iota(jnp.int32, (tm, 1), 0)
    _pool_diff_tile(ext_ref, t1_ref, t2_ref, tm, pos, d_ref)
    a = jnp.concatenate([a_ref[hd] for hd in range(N_HEADS)], axis=-1)
    x1 = x_ref[...] + _mix_out(a, d_ref, wp_ref, ps_ref, wo_ref)
    x1_ref[...] = x1
    h2_ref[...] = _rmsnorm(x1, g2_ref[...]).astype(BF16)

    @pl.when(t_in_seq == tiles_per_seq - 1)
    def _():
        hist_ref[...] = ext_ref[pl.ds(POOL_PAD + tm - POOL_HIST, POOL_HIST), :]


def _mid_prompt(a, u, x, wp_bf, ps, wo_bf, g2, batch, seq, tm=512):
    m = x.shape[0]
    tiles_per_seq = seq // tm
    row = lambda i: (i, 0)
    const2 = lambda i: (0, 0)
    return pl.pallas_call(
        functools.partial(_mid_prompt_kernel, tm=tm, tiles_per_seq=tiles_per_seq),
        grid=(m // tm,),
        in_specs=[
            pl.BlockSpec((N_HEADS, tm, HEAD_DIM), lambda i: (0, i, 0)),
            pl.BlockSpec((tm, POOL_WIDTH), row),
            pl.BlockSpec((HIST_PAD, POOL_WIDTH), lambda i: (jnp.maximum(i * (tm // HIST_PAD) - 1, 0), 0)),
            pl.BlockSpec((tm, D_MODEL), row),
            pl.BlockSpec(wp_bf.shape, lambda i: (0, 0, 0), pipeline_mode=pl.Buffered(1)),
            pl.BlockSpec((1, POOL_WIDTH), const2),
            pl.BlockSpec((D_MODEL, D_MODEL), const2, pipeline_mode=pl.Buffered(1)),
            pl.BlockSpec((1, D_MODEL), const2),
        ],
        out_specs=[
            pl.BlockSpec((tm, D_MODEL), row),
            pl.BlockSpec((tm, D_MODEL), row),
            pl.BlockSpec((None, POOL_HIST, POOL_WIDTH), lambda i: (i // tiles_per_seq, 0, 0)),
        ],
        out_shape=[
            jax.ShapeDtypeStruct((m, D_MODEL), F32),
            jax.ShapeDtypeStruct((m, D_MODEL), BF16),
            jax.ShapeDtypeStruct((batch, POOL_HIST, POOL_WIDTH), F32),
        ],
        scratch_shapes=[pltpu.VMEM((POOL_PAD + tm, POOL_WIDTH), F32), pltpu.VMEM((tm, POOL_WIDTH), F32),
                        pltpu.VMEM((POOL_PAD + tm, POOL_GROUP_W), F32), pltpu.VMEM((POOL_PAD + tm, POOL_GROUP_W), F32)],
        compiler_params=_params("arbitrary"),
        name="mid_prompt",
    )(a, u, u, x, wp_bf, ps, wo_bf, g2)


def _mid_sample_kernel(a_ref, u_ref, hist_ref, x_ref, wp_ref, ps_ref, wo_ref, g2_ref,
                       x1_ref, h2_ref, newhist_ref, ext_ref, d_ref, *, n_seq, t_new, start_pos):
    stride = POOL_HIST + t_new
    stride += -stride % SUBLANES
    pos = start_pos + lax.broadcasted_iota(jnp.int32, (t_new, 1), 0)
    for b in range(n_seq):
        base = b * stride
        ext_ref[base:base + POOL_HIST, :] = hist_ref[b]
        ext_ref[base + POOL_HIST:base + POOL_HIST + t_new, :] = u_ref[b * t_new:(b + 1) * t_new, :]
        _pool_diff(ext_ref, base + POOL_HIST, t_new, pos, d_ref, b * t_new)
        newhist_ref[b] = ext_ref[pl.ds(base + t_new, POOL_HIST), :]
    x1 = x_ref[...] + _mix_out(a_ref[...].astype(BF16), d_ref, wp_ref, ps_ref, wo_ref)
    x1_ref[...] = x1
    h2_ref[...] = _rmsnorm(x1, g2_ref[...]).astype(BF16)


def _mid_sample(a, u, hist, x, wp_bf, ps, wo_bf, g2, n_seq, t_new, start_pos):
    m = x.shape[0]
    stride = POOL_HIST + t_new
    stride += -stride % SUBLANES
    return pl.pallas_call(
        functools.partial(_mid_sample_kernel, n_seq=n_seq, t_new=t_new, start_pos=start_pos),
        out_shape=[
            jax.ShapeDtypeStruct((m, D_MODEL), F32),
            jax.ShapeDtypeStruct((m, D_MODEL), BF16),
            jax.ShapeDtypeStruct((n_seq, POOL_HIST, POOL_WIDTH), F32),
        ],
        scratch_shapes=[pltpu.VMEM((n_seq * stride, POOL_WIDTH), F32), pltpu.VMEM((m, POOL_WIDTH), F32)],
        compiler_params=pltpu.CompilerParams(vmem_limit_bytes=VMEM_LIMIT_BYTES),
        name="mid_sample",
    )(a, u, hist, x, wp_bf, ps, wo_bf, g2)


def _store_block_mean(o_ref, jb, tot):
    mean = tot * (1.0 / MOBA_BLOCK)
    for hd in range(N_HEADS):
        o_ref[jb:jb + 1, hd * HEAD_DIM:(hd + 1) * HEAD_DIM] = mean[hd:hd + 1, :]


def _ffn_kernel(*refs, n_pages):
    if n_pages:
        refs = refs[1:]
    h2_ref, wg_ref, wu_ref, wd_ref, x1_ref, gf_ref = refs[:6]
    page_refs = refs[6:6 + n_pages]
    y_ref = refs[6 + n_pages]
    km_ref = refs[7 + n_pages] if n_pages else None
    j = pl.program_id(1)

    @pl.when(j == 0)
    def _():
        y_ref[...] = x1_ref[...]

    n_slots = 6
    sums = {}

    def side_job(slot):
        for r in range(slot, n_pages, n_slots):
            sums[r] = jnp.sum(page_refs[r][...], axis=0)
            jb, pos = divmod(r, PAGES_PER_BLOCK)
            if pos == PAGES_PER_BLOCK - 1:
                tot = sums[jb * PAGES_PER_BLOCK]
                for rr in range(jb * PAGES_PER_BLOCK + 1, r + 1):
                    tot = tot + sums[rr]
                _store_block_mean(km_ref, jb, tot)

    h2 = h2_ref[...]
    tf = wg_ref.shape[1]
    gate, up = [], []
    for c, cols in enumerate((slice(0, tf // 2), slice(tf // 2, tf))):
        gate.append(jnp.dot(h2, wg_ref[:, cols], preferred_element_type=F32))
        side_job(c)
    for c, cols in enumerate((slice(0, tf // 2), slice(tf // 2, tf))):
        up.append(jnp.dot(h2, wu_ref[:, cols], preferred_element_type=F32))
        side_job(2 + c)
    gate = jnp.concatenate(gate, axis=-1)
    act = (gate * jax.nn.sigmoid(gate) * jnp.concatenate(up, axis=-1)).astype(BF16)
    for c, cols in enumerate((slice(0, D_MODEL // 2), slice(D_MODEL // 2, D_MODEL))):
        y_ref[:, cols] += jnp.dot(act, wd_ref[:, cols], preferred_element_type=F32)
        side_job(4 + c)

    @pl.when(j == pl.num_programs(1) - 1)
    def _():
        y_ref[...] = _rmsnorm(y_ref[...], gf_ref[...])


def _ffn(h2, wg_bf, wu_bf, wd_bf, x1, gf, tm, tf=512, paged_k=None):
    m = h2.shape[0]
    d_ff = wg_bf.shape[1]
    ni, nj = m // tm, d_ff // tf
    in_specs = [
        pl.BlockSpec((tm, D_MODEL), lambda i, j, *_: (i, 0)),
        pl.BlockSpec((D_MODEL, tf), lambda i, j, *_: (0, j)),
        pl.BlockSpec((D_MODEL, tf), lambda i, j, *_: (0, j)),
        pl.BlockSpec((tf, D_MODEL), lambda i, j, *_: (j, 0)),
        pl.BlockSpec((tm, D_MODEL), lambda i, j, *_: (i, 0)),
        pl.BlockSpec((1, D_MODEL), lambda i, j, *_: (0, 0)),
    ]
    y_spec = pl.BlockSpec((tm, D_MODEL), lambda i, j, *_: (i, 0))
    y_shape = jax.ShapeDtypeStruct((m, D_MODEL), F32)
    if paged_k is None:
        return pl.pallas_call(
            functools.partial(_ffn_kernel, n_pages=0),
            grid=(ni, nj), in_specs=in_specs, out_specs=y_spec, out_shape=y_shape,
            compiler_params=_params("arbitrary", "arbitrary"), name="ffn",
        )(h2, wg_bf, wu_bf, wd_bf, x1, gf)

    page_table_flat, cache_k = paged_k
    total = page_table_flat.shape[0]
    pps = PAGES_PER_BLOCK * pl.cdiv(total, PAGES_PER_BLOCK * ni * nj)
    n_events = pl.cdiv(total, pps)

    def page_spec(r):
        def index_map(i, j, pt):
            event = jnp.minimum(i * nj + j, n_events - 1)
            return (0, pt[jnp.minimum(event * pps + r, total - 1)], 0, 0, 0)
        return pl.BlockSpec((None, None, PAGE_SIZE, N_HEADS, HEAD_DIM), index_map)

    bps = pps // PAGES_PER_BLOCK
    km_spec = pl.BlockSpec((None, bps, ATTN_WIDTH), lambda i, j, pt: (jnp.minimum(i * nj + j, n_events - 1), 0, 0))
    y, km = pl.pallas_call(
        functools.partial(_ffn_kernel, n_pages=pps),
        grid_spec=pltpu.PrefetchScalarGridSpec(
            num_scalar_prefetch=1, grid=(ni, nj),
            in_specs=in_specs + [page_spec(r) for r in range(pps)],
            out_specs=[y_spec, km_spec]),
        out_shape=[y_shape, jax.ShapeDtypeStruct((n_events, bps, ATTN_WIDTH), F32)],
        compiler_params=_params("arbitrary", "arbitrary"), name="ffn_kmean",
    )(page_table_flat, h2, wg_bf, wu_bf, wd_bf, x1, gf, *([cache_k] * pps))
    return y, km.reshape(n_events * bps, ATTN_WIDTH)[:total // PAGES_PER_BLOCK]


def _select_sample_kernel(qh_ref, km_ref, idx_ref, sc_ref, *, n_seq, t_new, n_past_blocks):
    lanes = sc_ref.shape[1]
    lane = lax.broadcasted_iota(jnp.int32, sc_ref.shape, 1)
    sc_ref[...] = jnp.where(lane == n_past_blocks, NEG, -jnp.inf)
    for b in range(n_seq):
        for hd in range(N_HEADS):
            q_hl = jnp.concatenate(_split_bf16(qh_ref[hd, b * t_new:(b + 1) * t_new, :]), axis=0)
            km_hi, km_lo = _split_bf16(km_ref[b, :, hd * HEAD_DIM:(hd + 1) * HEAD_DIM])
            by_hi = lax.dot_general(q_hl, km_hi, _NT, preferred_element_type=F32)
            r0 = (b * N_HEADS + hd) * t_new
            sc_ref[r0:r0 + t_new, 0:n_past_blocks] = (
                by_hi[0:t_new] + by_hi[t_new:2 * t_new]
                + lax.dot_general(q_hl[0:t_new], km_lo, _NT, preferred_element_type=F32))
    s = sc_ref[...]
    for r in range(MOBA_TOPK):
        mx = jnp.max(s, axis=-1, keepdims=True)
        am = jnp.min(jnp.where(s == mx, lane, lanes), axis=-1, keepdims=True)
        idx_ref[:, r:r + 1] = am
        s = jnp.where(lane == am, -jnp.inf, s)


def _select_sample(qh, km, n_seq, t_new):
    n_past_blocks = km.shape[1]
    rows = n_seq * N_HEADS * t_new
    return pl.pallas_call(
        functools.partial(_select_sample_kernel, n_seq=n_seq, t_new=t_new, n_past_blocks=n_past_blocks),
        out_shape=jax.ShapeDtypeStruct((rows, MOBA_TOPK), jnp.int32),
        scratch_shapes=[pltpu.VMEM((rows, LANES), F32)],
        compiler_params=pltpu.CompilerParams(vmem_limit_bytes=VMEM_LIMIT_BYTES),
        name="select_sample",
    )(qh, km)


GATHER_DEPTH = 2


def _attn_sample_kernel(pt_ref, idx_ref, slopes_ref, qh_ref, kh_ref, vh_ref, ck_hbm, cv_hbm, o_ref,
                        kbuf, vbuf, sem, *, n_seq, t_new, n_pages):
    n_past_blocks = n_pages // PAGES_PER_BLOCK
    past_len = n_pages * PAGE_SIZE
    n_slots = t_new * MOBA_TOPK
    n_keys = n_slots * MOBA_BLOCK
    n_steps = n_seq * N_HEADS
    n_bufs = GATHER_DEPTH + 1
    b = pl.program_id(0)
    hd = pl.program_id(1)
    step = b * N_HEADS + hd

    def block_of(st, qj):
        return idx_ref[st * n_slots + qj]

    def gather(st, start):
        bb, hh, sl = st // N_HEADS, lax.rem(st, N_HEADS), lax.rem(st, n_bufs)
        for qj in range(n_slots):
            blk = jnp.minimum(block_of(st, qj), n_past_blocks - 1)
            for half in range(PAGES_PER_BLOCK):
                phys = pt_ref[bb * n_pages + blk * PAGES_PER_BLOCK + half]
                rows = pl.ds(qj * MOBA_BLOCK + half * PAGE_SIZE, PAGE_SIZE)
                for src, dst, s in ((ck_hbm, kbuf, 0), (cv_hbm, vbuf, 1)):
                    cp = pltpu.make_async_copy(src.at[0, phys, :, hh, :], dst.at[sl, rows, :], sem.at[s, sl])
                    if start:
                        cp.start()
                    else:
                        cp.wait()

    @pl.when(step == 0)
    def _():
        for d in range(min(GATHER_DEPTH, n_steps)):
            gather(step + d, True)

    @pl.when(step + GATHER_DEPTH < n_steps)
    def _():
        gather(step + GATHER_DEPTH, True)

    gather(step, False)

    buf = lax.rem(step, n_bufs)
    slope = slopes_ref[hd]
    rows = pl.ds(b * t_new, t_new)
    qs = (qh_ref[rows, :] * ATTN_SCALE).astype(BF16)

    s = lax.dot_general(qs, kbuf[buf].astype(BF16), _NT, preferred_element_type=F32)
    row = lax.broadcasted_iota(jnp.int32, (t_new, n_keys), 0)
    col = lax.broadcasted_iota(jnp.int32, (t_new, n_keys), 1)
    shift = jnp.zeros((t_new, n_keys), jnp.int32)
    valid = jnp.zeros((t_new, n_keys), jnp.int32)
    for qj in range(n_slots):
        blk = block_of(step, qj)
        in_slot = jnp.logical_and(col >= qj * MOBA_BLOCK, col < (qj + 1) * MOBA_BLOCK)
        shift = jnp.where(in_slot, blk * MOBA_BLOCK - qj * MOBA_BLOCK, shift)
        valid = jnp.where(in_slot, (blk < n_past_blocks).astype(jnp.int32), valid)
    dist = (past_len + row - (col + shift)).astype(F32)
    q_lo = row * (MOBA_TOPK * MOBA_BLOCK)
    mine = jnp.logical_and(col >= q_lo, col < q_lo + MOBA_TOPK * MOBA_BLOCK)
    s = jnp.where(jnp.logical_and(mine, valid > 0), s - slope * dist, NEG)
    own_rel = (lax.broadcasted_iota(jnp.int32, (t_new, t_new), 0)
               - lax.broadcasted_iota(jnp.int32, (t_new, t_new), 1))
    s_own = lax.dot_general(qs, kh_ref[rows, :].astype(BF16), _NT, preferred_element_type=F32)
    s_own = jnp.where(own_rel >= 0, s_own - slope * own_rel.astype(F32), NEG)

    m = jnp.maximum(jnp.max(s_own, axis=-1, keepdims=True), jnp.max(s, axis=-1, keepdims=True))
    p_own = jnp.exp(s_own - m)
    p = jnp.exp(s - m)
    l = jnp.sum(p_own, axis=-1, keepdims=True) + jnp.sum(p, axis=-1, keepdims=True)
    acc = (jnp.dot(p_own.astype(BF16), vh_ref[rows, :].astype(BF16), preferred_element_type=F32)
           + jnp.dot(p.astype(BF16), vbuf[buf].astype(BF16), preferred_element_type=F32))
    o_ref[...] = acc / l


def _attn_sample(page_table_flat, idx_flat, slopes, qh, kh, vh, cache_k, cache_v, n_seq, t_new, n_pages):
    n_slots = t_new * MOBA_TOPK
    head_blk = pl.BlockSpec((None, n_seq * t_new, HEAD_DIM), lambda b, h, pt, ix, sl: (h, 0, 0))
    grid_spec = pltpu.PrefetchScalarGridSpec(
        num_scalar_prefetch=3,
        grid=(n_seq, N_HEADS),
        in_specs=[head_blk, head_blk, head_blk,
                  pl.BlockSpec(memory_space=pl.ANY), pl.BlockSpec(memory_space=pl.ANY)],
        out_specs=pl.BlockSpec((None, t_new, HEAD_DIM), lambda b, h, pt, ix, sl: (b, 0, h)),
        scratch_shapes=[
            pltpu.VMEM((GATHER_DEPTH + 1, n_slots * MOBA_BLOCK, HEAD_DIM), F32),
            pltpu.VMEM((GATHER_DEPTH + 1, n_slots * MOBA_BLOCK, HEAD_DIM), F32),
            pltpu.SemaphoreType.DMA((2, GATHER_DEPTH + 1)),
        ],
    )
    return pl.pallas_call(
        functools.partial(_attn_sample_kernel, n_seq=n_seq, t_new=t_new, n_pages=n_pages),
        grid_spec=grid_spec,
        out_shape=jax.ShapeDtypeStruct((n_seq, t_new, ATTN_WIDTH), F32),
        compiler_params=_params("arbitrary", "arbitrary"),
        name="attn_sample",
    )(page_table_flat, idx_flat, slopes, qh, kh, vh, cache_k, cache_v)


def kernel(x_prompt, x_sample, cache_k, cache_v, state_pool, page_table, norm1_g, w_in, w_pool, pool_scale,
           w_out, norm2_g, w_gate, w_up, w_down, norm_f_g):
    depth = w_in.shape[0]
    assert depth == 1, "single-layer step"
    batch, seq, _ = x_prompt.shape
    n_seq, t_new, _ = x_sample.shape
    n_pages = page_table.shape[1]
    past_len = n_pages * PAGE_SIZE
    assert seq % MOBA_BLOCK == 0 and past_len % MOBA_BLOCK == 0 and t_new <= MOBA_BLOCK

    slopes = jnp.asarray(_alibi_slopes())
    g1 = norm1_g[0].reshape(1, D_MODEL)
    g2 = norm2_g[0].reshape(1, D_MODEL)
    gf = norm_f_g.reshape(1, D_MODEL)
    ps = pool_scale[0].reshape(1, POOL_WIDTH)
    w_pool_bf = w_pool[0].astype(BF16)

    xp = x_prompt.reshape(batch * seq, D_MODEL)
    pt_flat = page_table.reshape(-1)
    k, v, u, qh_p, kbh, vbh, km = _proj_prompt(xp, g1, w_in[0])
    km = km.reshape(batch, seq // MOBA_BLOCK, ATTN_WIDTH)
    a, (w_out_bf, w_gate_bf, w_up_bf, w_down_bf) = _attn_prompt(
        qh_p, kbh, vbh, km, _alibi_key_table(seq), batch, seq,
        (w_out[0], w_gate[0], w_up[0], w_down[0]))
    x1, h2, hist_p = _mid_prompt(a, u, xp, w_pool_bf, ps, w_out_bf, g2, batch, seq)
    y_p, km_s = _ffn(h2, w_gate_bf, w_up_bf, w_down_bf, x1, gf, tm=512, paged_k=(pt_flat, cache_k))

    xs = x_sample.reshape(n_seq * t_new, D_MODEL)
    k_s, v_s, u_s, qh, kh, vh = _proj_sample(xs, g1, w_in[0])
    km_s = km_s.reshape(n_seq, n_pages // PAGES_PER_BLOCK, ATTN_WIDTH)
    idx = _select_sample(qh, km_s, n_seq, t_new)
    a_s = _attn_sample(pt_flat, idx.reshape(-1), slopes, qh, kh, vh, cache_k, cache_v, n_seq, t_new, n_pages)
    x1_s, h2_s, hist_s = _mid_sample(a_s.reshape(n_seq * t_new, ATTN_WIDTH), u_s, state_pool[0], xs,
                                     w_pool_bf, ps, w_out_bf, g2, n_seq, t_new, past_len)
    y_s = _ffn(h2_s, w_gate_bf, w_up_bf, w_down_bf, x1_s, gf, tm=n_seq * t_new)

    kv_p = (depth, batch, seq, N_HEADS, HEAD_DIM)
    kv_s = (depth, n_seq, t_new, N_HEADS, HEAD_DIM)
    return (y_p.reshape(batch, seq, D_MODEL), y_s.reshape(n_seq, t_new, D_MODEL),
            k.reshape(kv_p), v.reshape(kv_p), hist_p[None],
            k_s.reshape(kv_s), v_s.reshape(kv_s), hist_s[None])
```
